```python
import math
import jax, jax.numpy as jnp
from jax import lax
import numpy as np

D_MODEL = 1024
BATCH = 8
SEQ = 2048
DEPTH = 4

N_META = 16
MIX_W = D_MODEL
ATT_W = MIX_W // 2
REC_W = MIX_W - ATT_W
ATT_HEAD_DIM = 64
N_ATT_HEADS = ATT_W // ATT_HEAD_DIM
N_REC_BLOCKS = 8
REC_BLOCK = REC_W // N_REC_BLOCKS
CONV_WIDTH = 4
RG_C = 8.0
D_FF = 4 * D_MODEL
Q_BLOCK = 128
NORM_EPS = 1e-6
D_IN = 3 * ATT_W + N_ATT_HEADS + 2 * REC_W

kernel_name = "hymba_fox_rglru_hybrid"


def rmsnorm(x, g):
    xf = x.astype(jnp.float32)
    y = xf * lax.rsqrt(jnp.mean(xf * xf, axis=-1, keepdims=True) + NORM_EPS)
    return (y * g.astype(jnp.float32)).astype(x.dtype)


def forgetting_attention(q, k, v, log_f):
    T = q.shape[1]
    scale = 1.0 / math.sqrt(q.shape[-1])
    c = jnp.cumsum(log_f, axis=1)
    c_bh = jnp.transpose(c, (0, 2, 1))
    starts = [0] + list(range(N_META, T, Q_BLOCK))
    ends = starts[1:] + [T]
    pos = jnp.arange(T)
    outs = []
    for qs, qe in zip(starts, ends):
        qb = q[:, qs:qe]
        kb = k[:, :qe]
        vb = v[:, :qe]
        s = jnp.einsum('bqhd,bkhd->bhqk', qb, kb).astype(jnp.float32) * scale
        bias = c_bh[:, :, qs:qe, None] - c_bh[:, :, None, :qe]
        mask = pos[qs:qe, None] >= pos[None, :qe]
        s = jnp.where(mask[None, None], s + bias, -jnp.inf)
        p = jax.nn.softmax(s, axis=-1).astype(vb.dtype)
        outs.append(jnp.einsum('bhqk,bkhd->bqhd', p, vb))
    return jnp.concatenate(outs, axis=1)


def causal_depthwise_conv(x, w, b):
    C = x.shape[-1]
    y = lax.conv_general_dilated(
        x, w[:, None, :].astype(x.dtype), window_strides=(1,),
        padding=[(CONV_WIDTH - 1, 0)], dimension_numbers=('NWC', 'WIO', 'NWC'),
        feature_group_count=C)
    return y + b.astype(x.dtype)


def block_diag_linear(x, w, b):
    B, T, C = x.shape
    xb = x.reshape(B, T, N_REC_BLOCKS, REC_BLOCK)
    y = jnp.einsum('btnd,nde->btne', xb, w.astype(x.dtype)).reshape(B, T, C)
    return y + b.astype(x.dtype)


def rg_lru(x, w_ga, b_ga, w_gx, b_gx, lru_L):
    r = jax.nn.sigmoid(block_diag_linear(x, w_ga, b_ga).astype(jnp.float32))
    i = jax.nn.sigmoid(block_diag_linear(x, w_gx, b_gx).astype(jnp.float32))
    log_a = RG_C * r * jax.nn.log_sigmoid(lru_L.astype(jnp.float32))
    a = jnp.exp(log_a)
    mult = jnp.sqrt(-jnp.expm1(2.0 * log_a))
    u = mult * i * x.astype(jnp.float32)

    def combine(e1, e2):
        a1, b1 = e1
        a2, b2 = e2
        return a1 * a2, a2 * b1 + b2

    _, h = lax.associative_scan(combine, (a, u), axis=1)
    return h.astype(x.dtype)


def setup_inputs(seed: int = 0) -> dict:
    key = jax.random.key(seed)
    ks = jax.random.split(key, 24)
    f32 = jnp.float32
    L = DEPTH
    nrm = lambda k, shape, s: (jax.random.normal(k, shape, f32) * s)
    x = jax.random.normal(ks[0], (BATCH, SEQ, D_MODEL), f32)
    meta = nrm(ks[1], (N_META, D_MODEL), 1.0)
    attn_norm_g = 1.0 + nrm(ks[2], (L, D_MODEL), 0.02)
    w_in = nrm(ks[3], (L, D_MODEL, D_IN), D_MODEL ** -0.5)
    b_f = jax.random.uniform(ks[4], (L, N_ATT_HEADS), f32, 1.0, 5.0)
    conv_w = nrm(ks[5], (L, CONV_WIDTH, REC_W), CONV_WIDTH ** -0.5)
    conv_b = nrm(ks[6], (L, REC_W), 0.02)
    w_gate_a = nrm(ks[7], (L, N_REC_BLOCKS, REC_BLOCK, REC_BLOCK), REC_BLOCK ** -0.5)
    b_gate_a = nrm(ks[8], (L, REC_W), 0.02)
    w_gate_x = nrm(ks[9], (L, N_REC_BLOCKS, REC_BLOCK, REC_BLOCK), REC_BLOCK ** -0.5)
    b_gate_x = nrm(ks[10], (L, REC_W), 0.02)
    u = jax.random.uniform(ks[11], (L, REC_W), f32, 0.9, 0.999)
    a_base = u ** (1.0 / RG_C)
    lru_L = jnp.log(a_base) - jnp.log1p(-a_base)
    attn_out_g = 1.0 + nrm(ks[12], (L, ATT_W), 0.02)
    rec_out_g = 1.0 + nrm(ks[13], (L, REC_W), 0.02)
    w_out = nrm(ks[14], (L, MIX_W, D_MODEL), (MIX_W * 2 * DEPTH) ** -0.5)
    mlp_norm_g = 1.0 + nrm(ks[15], (L, D_MODEL), 0.02)
    w_up = nrm(ks[16], (L, D_MODEL, D_FF), D_MODEL ** -0.5)
    w_down = nrm(ks[17], (L, D_FF, D_MODEL), (D_FF * 2 * DEPTH) ** -0.5)
    final_g = 1.0 + nrm(ks[18], (D_MODEL,), 0.02)
    return {"x": x, "meta": meta, "attn_norm_g": attn_norm_g, "w_in": w_in,
            "b_f": b_f, "conv_w": conv_w, "conv_b": conv_b,
            "w_gate_a": w_gate_a, "b_gate_a": b_gate_a,
            "w_gate_x": w_gate_x, "b_gate_x": b_gate_x, "lru_L": lru_L,
            "attn_out_g": attn_out_g, "rec_out_g": rec_out_g, "w_out": w_out,
            "mlp_norm_g": mlp_norm_g, "w_up": w_up, "w_down": w_down,
            "final_g": final_g}


def reference(x, meta, attn_norm_g, w_in, b_f, conv_w, conv_b, w_gate_a, b_gate_a,
              w_gate_x, b_gate_x, lru_L, attn_out_g, rec_out_g, w_out,
              mlp_norm_g, w_up, w_down, final_g):
    B, S, D = x.shape
    meta_b = jnp.broadcast_to(meta.astype(x.dtype)[None], (B, N_META, D))
    h = jnp.concatenate([meta_b, x], axis=1)
    T = h.shape[1]
    splits = np.cumsum([ATT_W, ATT_W, ATT_W, N_ATT_HEADS, REC_W]).tolist()
    for l in range(DEPTH):
        z = rmsnorm(h, attn_norm_g[l])
        proj = z @ w_in[l]
        q, k, v, f_logit, xr, yr = jnp.split(proj, splits, axis=-1)
        log_f = jax.nn.log_sigmoid(f_logit.astype(jnp.float32) + b_f[l].astype(jnp.float32))
        qh = q.reshape(B, T, N_ATT_HEADS, ATT_HEAD_DIM)
        kh = k.reshape(B, T, N_ATT_HEADS, ATT_HEAD_DIM)
        vh = v.reshape(B, T, N_ATT_HEADS, ATT_HEAD_DIM)
        attn = forgetting_attention(qh, kh, vh, log_f).reshape(B, T, ATT_W)
        xc = causal_depthwise_conv(xr, conv_w[l], conv_b[l])
        hr = rg_lru(xc, w_gate_a[l], b_gate_a[l], w_gate_x[l], b_gate_x[l], lru_L[l])
        rec = hr * jax.nn.gelu(yr)
        mix = jnp.concatenate([rmsnorm(attn, attn_out_g[l]), rmsnorm(rec, rec_out_g[l])], axis=-1)
        h = h + mix @ w_out[l]
        z = rmsnorm(h, mlp_norm_g[l])
        u = jax.nn.relu(z @ w_up[l])
        h = h + (u * u) @ w_down[l]
    h = rmsnorm(h, final_g)
    return h[:, N_META:]
```

```python
import functools
import math

import jax
import jax.numpy as jnp
from jax import lax
from jax.experimental import pallas as pl
from jax.experimental.pallas import tpu as pltpu

F32 = jnp.float32
BF16 = jnp.bfloat16

N_META = 16
META_PAD = 128
N_HEADS = 8
HEAD_DIM = 64
ATT_W = N_HEADS * HEAD_DIM
REC_W = 512
CONV_WIDTH = 4
RG_C = 8.0
NORM_EPS = 1e-6
LANES = 128
SUBLANES = 8
F_ROWS = 16
VMEM_LIMIT = 56 * 1024 * 1024


def _rms(x, g):
    ms = jnp.mean(x * x, axis=-1, keepdims=True)
    return x * lax.rsqrt(ms + NORM_EPS) * g


def _log_sigmoid(x):
    return jnp.minimum(x, 0.0) - jnp.log1p(jnp.exp(-jnp.abs(x)))


def _gelu_tanh(x):
    c = math.sqrt(2.0 / math.pi)
    return x * (0.5 * (1.0 + jnp.tanh(c * (x + 0.044715 * (x * x * x)))))


def _dot(a, b):
    return jnp.dot(a, b, preferred_element_type=F32)


def _dot_nt(a, b):
    return lax.dot_general(a, b, (((1,), (1,)), ((), ())), preferred_element_type=F32)


def _dot01(x, m01):
    hi = x.astype(BF16)
    r1 = x - hi.astype(F32)
    mid = r1.astype(BF16)
    lo = (r1 - mid.astype(F32)).astype(BF16)
    return _dot(hi, m01) + _dot(mid, m01) + _dot(lo, m01)


def _upper_ones(n):
    r = lax.broadcasted_iota(jnp.int32, (n, n), 0)
    c = lax.broadcasted_iota(jnp.int32, (n, n), 1)
    return (r <= c).astype(BF16)


def _const_spec(shape):
    nd = len(shape)
    return pl.BlockSpec(shape, lambda *_: (0,) * nd, pipeline_mode=pl.Buffered(1))


def _inproj_kernel(h_ref, g_ref, wqkv_ref, wf_ref, bf_ref, wxy_ref,
                   qkv_ref, logf_ref, xy_ref):
    z = _rms(h_ref[...], g_ref[...]).astype(BF16)
    qkv_ref[...] = _dot(z, wqkv_ref[...]).astype(BF16)
    xy_ref[...] = _dot(z, wxy_ref[...])
    f = _dot_nt(wf_ref[...], z)
    logf_ref[0] = _log_sigmoid(f + bf_ref[...])


def _inproj(h, g, wqkv, wf, bf, wxy, *, seq, tm):
    n, d = h.shape
    per_seq = seq // tm
    return pl.pallas_call(
        _inproj_kernel,
        grid=(n // tm,),
        in_specs=[
            pl.BlockSpec((tm, d), lambda i: (i, 0)),
            _const_spec(g.shape),
            _const_spec(wqkv.shape),
            _const_spec(wf.shape),
            _const_spec(bf.shape),
            _const_spec(wxy.shape),
        ],
        out_specs=[
            pl.BlockSpec((tm, 3 * ATT_W), lambda i: (i, 0)),
            pl.BlockSpec((1, F_ROWS, tm), lambda i: (i // per_seq, 0, i % per_seq)),
            pl.BlockSpec((tm, 2 * REC_W), lambda i: (i, 0)),
        ],
        out_shape=[
            jax.ShapeDtypeStruct((n, 3 * ATT_W), BF16),
            jax.ShapeDtypeStruct((n // seq, F_ROWS, seq), F32),
            jax.ShapeDtypeStruct((n, 2 * REC_W), F32),
        ],
        compiler_params=pltpu.CompilerParams(
            dimension_semantics=("arbitrary",), vmem_limit_bytes=VMEM_LIMIT),
        name="inproj",
    )(h, g, wqkv, wf, bf, wxy)


def _softmax_step(s, m, l, acc, v):
    m_new = jnp.maximum(m, jnp.max(s, axis=-1, keepdims=True))
    alpha = jnp.exp(m - m_new)
    p = jnp.exp(s - m_new)
    l_new = alpha * l + jnp.sum(p, axis=-1, keepdims=True)
    acc_new = alpha * acc + _dot(p.astype(BF16), v)
    return m_new, l_new, acc_new


def _meta_attn_kernel(q_ref, k_ref, v_ref, lf_ref, g_ref, o_ref, bm_ref, o_s):
    t = q_ref.shape[0]
    c = _dot01(lf_ref[0], _upper_ones(t))
    col = lax.broadcasted_iota(jnp.int32, c.shape, 1)
    bm_ref[...] = jnp.where(col < N_META, c[:, N_META - 1:N_META] - c, -jnp.inf)
    row = lax.broadcasted_iota(jnp.int32, (t, t), 0)
    colk = lax.broadcasted_iota(jnp.int32, (t, t), 1)
    causal = row >= colk
    for h in range(N_HEADS):
        sl = slice(h * HEAD_DIM, (h + 1) * HEAD_DIM)
        s = _dot_nt(q_ref[:, sl], k_ref[:, sl]) - c[h:h + 1, :]
        s = jnp.where(causal, s, -jnp.inf)
        m = jnp.max(s, axis=-1, keepdims=True)
        p = jnp.exp(s - m)
        l = jnp.sum(p, axis=-1, keepdims=True)
        o_s[:, sl] = _dot(p.astype(BF16), v_ref[:, sl]) / l
    o_ref[...] = _rms(o_s[...], g_ref[...]).astype(BF16)


def _meta_attn(qkv, logf, g):
    t = qkv.shape[0]
    return pl.pallas_call(
        _meta_attn_kernel,
        grid=(1,),
        in_specs=[
            pl.BlockSpec((t, ATT_W), lambda i: (0, 0)),
            pl.BlockSpec((t, ATT_W), lambda i: (0, 1)),
            pl.BlockSpec((t, ATT_W), lambda i: (0, 2)),
            pl.BlockSpec((1, F_ROWS, t), lambda i: (0, 0, 0)),
            pl.BlockSpec(g.shape, lambda i: (0, 0)),
        ],
        out_specs=[
            pl.BlockSpec((t, ATT_W), lambda i: (0, 0)),
            pl.BlockSpec((F_ROWS, t), lambda i: (0, 0)),
        ],
        out_shape=[
            jax.ShapeDtypeStruct((t, ATT_W), BF16),
            jax.ShapeDtypeStruct((F_ROWS, t), F32),
        ],
        scratch_shapes=[pltpu.VMEM((t, ATT_W), F32)],
        name="meta_attn",
    )(qkv, qkv, qkv, logf, g)


def _attn_kernel(q_ref, k_ref, v_ref, lf_ref, km_ref, vm_ref, bm_ref, g_ref,
                 o_ref, nb_s, o_s, *, tq):
    j = pl.program_id(1)
    seq = k_ref.shape[0]

    @pl.when(j == 0)
    def _():
        upper = _upper_ones(LANES)
        carry = jnp.zeros((F_ROWS, 1), F32)
        per_tile = tq // LANES
        for c in range(seq // LANES):
            cs = _dot01(lf_ref[0, :, c * LANES:(c + 1) * LANES], upper) + carry
            lo = (c % per_tile) * LANES
            nb_s[c // per_tile, :, lo:lo + LANES] = -cs
            carry = cs[:, LANES - 1:LANES]

    row = lax.broadcasted_iota(jnp.int32, (tq, tq), 0)
    col = lax.broadcasted_iota(jnp.int32, (tq, tq), 1)
    causal = row >= col
    diag = pl.multiple_of(j * tq, tq)

    for h in range(N_HEADS):
        sl = slice(h * HEAD_DIM, (h + 1) * HEAD_DIM)
        qh = q_ref[:, sl]

        s0 = _dot_nt(qh, km_ref[:, sl]) + bm_ref[h:h + 1, :]
        m = jnp.max(s0, axis=-1, keepdims=True)
        p0 = jnp.exp(s0 - m)
        l = jnp.sum(p0, axis=-1, keepdims=True)
        acc = _dot(p0.astype(BF16), vm_ref[:, sl])

        def body(i, carry, qh=qh, sl=sl, h=h):
            off = pl.multiple_of(i * tq, tq)
            s = _dot_nt(qh, k_ref[pl.ds(off, tq), sl]) + nb_s[i, h:h + 1, :]
            return _softmax_step(s, *carry, v_ref[pl.ds(off, tq), sl])

        m, l, acc = lax.fori_loop(0, j, body, (m, l, acc))

        s = _dot_nt(qh, k_ref[pl.ds(diag, tq), sl]) + nb_s[j, h:h + 1, :]
        s = jnp.where(causal, s, -jnp.inf)
        m, l, acc = _softmax_step(s, m, l, acc, v_ref[pl.ds(diag, tq), sl])
        o_s[:, sl] = acc / l

    o_ref[...] = _rms(o_s[...], g_ref[...]).astype(BF16)


def _attn(qkv, logf, qkv_meta, bias_meta, g, *, seq, tq):
    n = qkv.shape[0]
    nb = n // seq
    per_seq = seq // tq
    tmeta = qkv_meta.shape[0]
    return pl.pallas_call(
        functools.partial(_attn_kernel, tq=tq),
        grid=(nb, per_seq),
        in_specs=[
            pl.BlockSpec((tq, ATT_W), lambda b, j: (b * per_seq + j, 0)),
            pl.BlockSpec((seq, ATT_W), lambda b, j: (b, 1)),
            pl.BlockSpec((seq, ATT_W), lambda b, j: (b, 2)),
            pl.BlockSpec((1, F_ROWS, seq), lambda b, j: (b, 0, 0)),
            pl.BlockSpec((tmeta, ATT_W), lambda b, j: (0, 1)),
            pl.BlockSpec((tmeta, ATT_W), lambda b, j: (0, 2)),
            pl.BlockSpec(bias_meta.shape, lambda b, j: (0, 0)),
            pl.BlockSpec(g.shape, lambda b, j: (0, 0)),
        ],
        out_specs=pl.BlockSpec((tq, ATT_W), lambda b, j: (b * per_seq + j, 0)),
        out_shape=jax.ShapeDtypeStruct((n, ATT_W), BF16),
        scratch_shapes=[
            pltpu.VMEM((per_seq, F_ROWS, tq), F32),
            pltpu.VMEM((tq, ATT_W), F32),
        ],
        compiler_params=pltpu.CompilerParams(
            dimension_semantics=("arbitrary", "arbitrary"), vmem_limit_bytes=VMEM_LIMIT),
        name="attn",
    )(qkv, qkv, qkv, logf, qkv_meta, qkv_meta, bias_meta, g)


def _rec_kernel(xr_ref, yr_ref, cw_ref, cb_ref, wg_ref, bg_ref, lru_ref, g_ref,
                h0_ref, tail0_ref, out_ref, hn_ref, tailn_ref,
                ext_s, a_s, u_s, h_s, *, nb, tc, pitch, state_t):
    step = pl.program_id(0)
    n_slab = REC_W // LANES
    half = REC_W // 2

    @pl.when(step == 0)
    def _():
        for b in range(nb):
            ext_s[b, tc:tc + SUBLANES, :] = tail0_ref[0]
        h_s[...] = jnp.broadcast_to(h0_ref[...], h_s.shape)

    log_sig_l = _log_sigmoid(lru_ref[...])

    for b in range(nb):
        ext_s[b, 0:SUBLANES, :] = ext_s[b, tc:tc + SUBLANES, :]
        ext_s[b, SUBLANES:SUBLANES + tc, :] = xr_ref[b]
        xc = cb_ref[...]
        for k in range(CONV_WIDTH):
            lo = SUBLANES - (CONV_WIDTH - 1) + k
            xc = xc + cw_ref[k:k + 1, :] * ext_s[b, lo:lo + tc, :]
        xcb = xc.astype(BF16)
        g0 = _dot(xcb[:, :half], wg_ref[0])
        g1 = _dot(xcb[:, half:], wg_ref[1])
        ra = jnp.concatenate([g0[:, :half], g1[:, :half]], axis=-1) + bg_ref[0:1, :]
        rx = jnp.concatenate([g0[:, half:], g1[:, half:]], axis=-1) + bg_ref[1:2, :]
        r = jax.nn.sigmoid(ra)
        i = jax.nn.sigmoid(rx)
        log_a = RG_C * r * log_sig_l
        a = jnp.exp(log_a)
        th = jnp.tanh(log_a)
        mult = jnp.sqrt(-2.0 * th / (1.0 - th))
        u = mult * i * xc
        for s in range(n_slab):
            a_s[s, b * pitch:b * pitch + tc, :] = a[:, s * LANES:(s + 1) * LANES]
            u_s[s, b * pitch:b * pitch + tc, :] = u[:, s * LANES:(s + 1) * LANES]

    def rows(t):
        return pl.ds(t, nb, stride=pitch) if nb > 1 else pl.ds(t, 1)

    def scan_body(t, hs):
        out = []
        for s in range(n_slab):
            hn = a_s[s, rows(t), :] * hs[s] + u_s[s, rows(t), :]
            u_s[s, rows(t), :] = hn
            out.append(hn)
        return tuple(out)

    hs = tuple(h_s[:, s * LANES:(s + 1) * LANES] for s in range(n_slab))
    hs = lax.fori_loop(0, tc, scan_body, hs, unroll=8)
    h_s[...] = jnp.concatenate(hs, axis=-1)

    hn_ref[...] = jnp.concatenate(
        [u_s[s, rows(state_t), :] for s in range(n_slab)], axis=-1)
    for b in range(nb):
        tailn_ref[b] = ext_s[b, state_t + 1:state_t + 1 + SUBLANES, :]
        hb = jnp.concatenate(
            [u_s[s, b * pitch:b * pitch + tc, :] for s in range(n_slab)], axis=-1)
        rec = hb * _gelu_tanh(yr_ref[b])
        out_ref[b] = _rms(rec, g_ref[...]).astype(BF16)


def _rec(xy, cw, cb, wg, bg, lru, g, h0, tail0, *, nb, seq, tc, state_t):
    xy3 = xy.reshape(nb, seq, 2 * REC_W)
    pitch = tc + SUBLANES
    n_slab = REC_W // LANES
    out, hn, tailn = pl.pallas_call(
        functools.partial(_rec_kernel, nb=nb, tc=tc, pitch=pitch, state_t=state_t),
        grid=(seq // tc,),
        in_specs=[
            pl.BlockSpec((nb, tc, REC_W), lambda t: (0, t, 0)),
            pl.BlockSpec((nb, tc, REC_W), lambda t: (0, t, 1)),
            _const_spec(cw.shape),
            _const_spec(cb.shape),
            _const_spec(wg.shape),
            _const_spec(bg.shape),
            _const_spec(lru.shape),
            _const_spec(g.shape),
            _const_spec(h0.shape),
            _const_spec(tail0.shape),
        ],
        out_specs=[
            pl.BlockSpec((nb, tc, REC_W), lambda t: (0, t, 0)),
            pl.BlockSpec((nb, REC_W), lambda t: (0, 0)),
            pl.BlockSpec((nb, SUBLANES, REC_W), lambda t: (0, 0, 0)),
        ],
        out_shape=[
            jax.ShapeDtypeStruct((nb, seq, REC_W), BF16),
            jax.ShapeDtypeStruct((nb, REC_W), F32),
            jax.ShapeDtypeStruct((nb, SUBLANES, REC_W), F32),
        ],
        scratch_shapes=[
            pltpu.VMEM((nb, tc + SUBLANES, REC_W), F32),
            pltpu.VMEM((n_slab, nb * pitch, LANES), F32),
            pltpu.VMEM((n_slab, nb * pitch, LANES), F32),
            pltpu.VMEM((nb, REC_W), F32),
        ],
        compiler_params=pltpu.CompilerParams(
            dimension_semantics=("arbitrary",), vmem_limit_bytes=VMEM_LIMIT),
        name="rec",
    )(xy3, xy3, cw, cb, wg, bg, lru, g, h0, tail0)
    return out.reshape(nb * seq, REC_W), hn, tailn


def _post_kernel(h_ref, an_ref, rn_ref, wo_ref, g_ref, wu_ref, wd_ref, fg_ref, o_ref,
                 *, final, ff_chunk):
    h1 = (h_ref[...] + _dot(an_ref[...], wo_ref[0:ATT_W, :])
          + _dot(rn_ref[...], wo_ref[ATT_W:ATT_W + REC_W, :]))
    z = _rms(h1, g_ref[...]).astype(BF16)
    down = None
    for c in range(wu_ref.shape[1] // ff_chunk):
        u = jnp.maximum(_dot(z, wu_ref[:, c * ff_chunk:(c + 1) * ff_chunk]), 0.0)
        part = _dot((u * u).astype(BF16), wd_ref[c * ff_chunk:(c + 1) * ff_chunk, :])
        down = part if down is None else down + part
    acc = h1 + down
    if final:
        acc = _rms(acc, fg_ref[...])
    o_ref[...] = acc


def _post(h, an, rn, wo, g, wu, wd, fg, *, tm, final):
    n, d = h.shape
    return pl.pallas_call(
        functools.partial(_post_kernel, final=final, ff_chunk=1024),
        grid=(n // tm,),
        in_specs=[
            pl.BlockSpec((tm, d), lambda i: (i, 0)),
            pl.BlockSpec((tm, ATT_W), lambda i: (i, 0)),
            pl.BlockSpec((tm, REC_W), lambda i: (i, 0)),
            _const_spec(wo.shape),
            _const_spec(g.shape),
            _const_spec(wu.shape),
            _const_spec(wd.shape),
            _const_spec(fg.shape),
        ],
        out_specs=pl.BlockSpec((tm, d), lambda i: (i, 0)),
        out_shape=jax.ShapeDtypeStruct((n, d), F32),
        compiler_params=pltpu.CompilerParams(
            dimension_semantics=("arbitrary",), vmem_limit_bytes=VMEM_LIMIT),
        name="post",
    )(h, an, rn, wo, g, wu, wd, fg)


def _block_diag_gates(w_a, w_x):
    nblk, blk, _ = w_a.shape
    per = nblk // 2
    eye = jnp.eye(per, dtype=w_a.dtype)

    def bd(w):
        w4 = w.reshape(2, per, blk, blk)
        full = jnp.einsum('gpij,pq->gpiqj', w4, eye)
        return full.reshape(2, per * blk, per * blk)

    return jnp.concatenate([bd(w_a), bd(w_x)], axis=-1).astype(BF16)


def kernel(x, meta, attn_norm_g, w_in, b_f, conv_w, conv_b, w_gate_a, b_gate_a,
           w_gate_x, b_gate_x, lru_L, attn_out_g, rec_out_g, w_out, mlp_norm_g,
           w_up, w_down, final_g):
    nb, seq, d = x.shape
    depth = w_in.shape[0]
    scale = 1.0 / math.sqrt(HEAD_DIM)
    c_q, c_f, c_x = ATT_W, 3 * ATT_W, 3 * ATT_W + N_HEADS

    h = x.reshape(nb * seq, d)
    hm = jnp.concatenate([meta, jnp.zeros((META_PAD - N_META, d), F32)], axis=0)
    fg = final_g.reshape(1, d)

    for l in range(depth):
        wl = w_in[l]
        wqkv = jnp.concatenate([wl[:, :c_q] * scale, wl[:, c_q:c_f]], axis=1).astype(BF16)
        wf = jnp.concatenate(
            [wl[:, c_f:c_x].T, jnp.zeros((F_ROWS - N_HEADS, d), F32)], axis=0).astype(BF16)
        bf = jnp.concatenate([b_f[l], jnp.zeros((F_ROWS - N_HEADS,), F32)]).reshape(F_ROWS, 1)
        wxy = wl[:, c_x:].astype(BF16)
        g_in = attn_norm_g[l].reshape(1, d)
        cw, cb = conv_w[l], conv_b[l].reshape(1, REC_W)
        wg = _block_diag_gates(w_gate_a[l], w_gate_x[l])
        bg = jnp.stack([b_gate_a[l], b_gate_x[l]])
        lru = lru_L[l].reshape(1, REC_W)
        g_att = attn_out_g[l].reshape(1, ATT_W)
        g_rec = rec_out_g[l].reshape(1, REC_W)
        wo = w_out[l].astype(BF16)
        g_mlp = mlp_norm_g[l].reshape(1, d)
        wu = w_up[l].astype(BF16)
        wd = w_down[l].astype(BF16)
        last = l == depth - 1

        qkv_m, logf_m, xy_m = _inproj(hm, g_in, wqkv, wf, bf, wxy, seq=META_PAD, tm=META_PAD)
        an_m, bias_m = _meta_attn(qkv_m, logf_m, g_att)
        rn_m, h_state, x_tail = _rec(
            xy_m, cw, cb, wg, bg, lru, g_rec,
            jnp.zeros((1, REC_W), F32), jnp.zeros((1, SUBLANES, REC_W), F32),
            nb=1, seq=META_PAD, tc=META_PAD, state_t=N_META - 1)
        if not last:
            hm = _post(hm, an_m, rn_m, wo, g_mlp, wu, wd, fg, tm=META_PAD, final=False)

        qkv, logf, xy = _inproj(h, g_in, wqkv, wf, bf, wxy, seq=seq, tm=512)
        an = _attn(qkv, logf, qkv_m, bias_m, g_att, seq=seq, tq=512)
        tc = 128
        rn, _, _ = _rec(xy, cw, cb, wg, bg, lru, g_rec, h_state, x_tail,
                        nb=nb, seq=seq, tc=tc, state_t=tc - 1)
        h = _post(h, an, rn, wo, g_mlp, wu, wd, fg, tm=512, final=last)

    return h.reshape(nb, seq, d)
```

```python
import functools
import math

import jax
import jax.numpy as jnp
from jax import lax
from jax.experimental import pallas as pl
from jax.experimental.pallas import tpu as pltpu

F32 = jnp.float32
BF16 = jnp.bfloat16

N_META = 16
META_PAD = 128
N_HEADS = 8
HEAD_DIM = 64
ATT_W = N_HEADS * HEAD_DIM
REC_W = 512
CONV_WIDTH = 4
RG_C = 8.0
NORM_EPS = 1e-6
LANES = 128
SUBLANES = 8
LOG2E = math.log2(math.e)
ATT_TILE = 512
VMEM_LIMIT = 56 * 1024 * 1024


def _rms(x, g):
    ms = jnp.mean(x * x, axis=-1, keepdims=True)
    return x * lax.rsqrt(ms + NORM_EPS) * g


def _log_sigmoid(x):
    return jnp.minimum(x, 0.0) - jnp.log1p(jnp.exp(-jnp.abs(x)))


def _gelu_tanh(x):
    c = math.sqrt(2.0 / math.pi)
    return x * (0.5 * (1.0 + jnp.tanh(c * (x + 0.044715 * (x * x * x)))))


def _dot(a, b):
    return jnp.dot(a, b, preferred_element_type=F32)


def _dot_nt(a, b):
    return lax.dot_general(a, b, (((1,), (1,)), ((), ())), preferred_element_type=F32)


def _const_spec(shape):
    nd = len(shape)
    return pl.BlockSpec(shape, lambda *_: (0,) * nd, pipeline_mode=pl.Buffered(1))


def _inproj_kernel(h_ref, g_ref, wqk_ref, wvt_ref, wf_ref, bf_ref, wxy_ref,
                   qk_ref, vt_ref, logf_ref, xy_ref):
    z = _rms(h_ref[...], g_ref[...]).astype(BF16)
    qk_ref[...] = _dot(z, wqk_ref[...]).astype(BF16)
    vt_ref[0] = _dot_nt(wvt_ref[...], z).astype(BF16)
    xy_ref[...] = _dot(z, wxy_ref[...])
    logf_ref[...] = _log_sigmoid(_dot(z, wf_ref[...]) + bf_ref[...])


def _inproj(h, g, wqk, wvt, wf, bf, wxy, *, tm):
    n, d = h.shape
    return pl.pallas_call(
        _inproj_kernel,
        grid=(n // tm,),
        in_specs=[
            pl.BlockSpec((tm, d), lambda i: (i, 0)),
            _const_spec(g.shape),
            _const_spec(wqk.shape),
            _const_spec(wvt.shape),
            _const_spec(wf.shape),
            _const_spec(bf.shape),
            _const_spec(wxy.shape),
        ],
        out_specs=[
            pl.BlockSpec((tm, 2 * ATT_W), lambda i: (i, 0)),
            pl.BlockSpec((1, ATT_W, tm), lambda i: (i, 0, 0)),
            pl.BlockSpec((tm, LANES), lambda i: (i, 0)),
            pl.BlockSpec((tm, 2 * REC_W), lambda i: (i, 0)),
        ],
        out_shape=[
            jax.ShapeDtypeStruct((n, 2 * ATT_W), BF16),
            jax.ShapeDtypeStruct((n // tm, ATT_W, tm), BF16),
            jax.ShapeDtypeStruct((n, LANES), F32),
            jax.ShapeDtypeStruct((n, 2 * REC_W), F32),
        ],
        compiler_params=pltpu.CompilerParams(
            dimension_semantics=("arbitrary",), vmem_limit_bytes=VMEM_LIMIT),
        name="inproj",
    )(h, g, wqk, wvt, wf, bf, wxy)


def _neg_cumsum_cols(lf_ref, out_ref):
    r = lax.broadcasted_iota(jnp.int32, (LANES, LANES), 0)
    c = lax.broadcasted_iota(jnp.int32, (LANES, LANES), 1)
    lower = (r >= c).astype(BF16)
    carry = jnp.zeros((1, LANES), F32)
    for blk in range(lf_ref.shape[0] // LANES):
        x = lf_ref[blk * LANES:(blk + 1) * LANES, :]
        hi = x.astype(BF16)
        r1 = x - hi.astype(F32)
        mid = r1.astype(BF16)
        lo = (r1 - mid.astype(F32)).astype(BF16)
        cs = _dot(lower, hi) + _dot(lower, mid) + _dot(lower, lo) + carry
        out_ref[blk * LANES:(blk + 1) * LANES, :] = cs * (-LOG2E)
        carry = cs[LANES - 1:LANES, :]


def _scores_t(kh, qh, bias_col, mask):
    st = _dot_nt(kh, qh) + bias_col
    if mask is not None:
        st = jnp.where(mask, st, -jnp.inf)
    return st


def _softmax_first(st, vth):
    m = jnp.max(st, axis=0, keepdims=True)
    p = jnp.exp2(st - m)
    l = jnp.sum(p, axis=0, keepdims=True)
    return m, l, _dot(vth, p.astype(BF16))


def _softmax_next(st, vth, m, l, acc):
    m_new = jnp.maximum(m, jnp.max(st, axis=0, keepdims=True))
    alpha = jnp.exp2(m - m_new)
    p = jnp.exp2(st - m_new)
    l_new = alpha * l + jnp.sum(p, axis=0, keepdims=True)
    return m_new, l_new, alpha * acc + _dot(vth, p.astype(BF16))


def _meta_attn_kernel(q_ref, k_ref, vt_ref, lf_ref, g_ref, o_ref, bm_ref, b_s, ot_s):
    t = q_ref.shape[0]
    _neg_cumsum_cols(lf_ref, b_s)
    row = lax.broadcasted_iota(jnp.int32, (t, LANES), 0)
    bm_ref[...] = jnp.where(row < N_META, b_s[...] - b_s[N_META - 1:N_META, :], -jnp.inf)
    key = lax.broadcasted_iota(jnp.int32, (t, t), 0)
    qry = lax.broadcasted_iota(jnp.int32, (t, t), 1)
    causal = key <= qry
    for h in range(N_HEADS):
        sl = slice(h * HEAD_DIM, (h + 1) * HEAD_DIM)
        st = _scores_t(k_ref[:, sl], q_ref[:, sl], b_s[:, h:h + 1], causal)
        _, l, acc = _softmax_first(st, vt_ref[0, sl, :])
        ot_s[sl, :] = acc / l
    o_ref[...] = _rms(ot_s[...].T, g_ref[...]).astype(BF16)


def _meta_attn(qk, vt, logf, g):
    t = qk.shape[0]
    return pl.pallas_call(
        _meta_attn_kernel,
        grid=(1,),
        in_specs=[
            pl.BlockSpec((t, ATT_W), lambda i: (0, 0)),
            pl.BlockSpec((t, ATT_W), lambda i: (0, 1)),
            pl.BlockSpec((1, ATT_W, t), lambda i: (0, 0, 0)),
            pl.BlockSpec((t, LANES), lambda i: (0, 0)),
            pl.BlockSpec(g.shape, lambda i: (0, 0)),
        ],
        out_specs=[
            pl.BlockSpec((t, ATT_W), lambda i: (0, 0)),
            pl.BlockSpec((t, LANES), lambda i: (0, 0)),
        ],
        out_shape=[
            jax.ShapeDtypeStruct((t, ATT_W), BF16),
            jax.ShapeDtypeStruct((t, LANES), F32),
        ],
        scratch_shapes=[pltpu.VMEM((t, LANES), F32), pltpu.VMEM((ATT_W, t), F32)],
        name="meta_attn",
    )(qk, qk, vt, logf, g)


def _attn_kernel(q_ref, k_ref, vt_ref, lf_ref, km_ref, vtm_ref, bm_ref, g_ref,
                 o_ref, b_s, m_s, l_s, acc_s, ot_s, *, tq):
    j = pl.program_id(1)

    @pl.when(j == 0)
    def _():
        _neg_cumsum_cols(lf_ref, b_s)

    heads = [slice(h * HEAD_DIM, (h + 1) * HEAD_DIM) for h in range(N_HEADS)]

    for h, sl in enumerate(heads):
        st = _scores_t(km_ref[:, sl], q_ref[:, sl], bm_ref[:, h:h + 1], None)
        m, l, acc = _softmax_first(st, vtm_ref[0, sl, :])
        m_s[h:h + 1, :] = m
        l_s[h:h + 1, :] = l
        acc_s[sl, :] = acc

    def tile(i, mask, finish):
        off = pl.multiple_of(i * tq, tq)
        for h, sl in enumerate(heads):
            st = _scores_t(k_ref[pl.ds(off, tq), sl], q_ref[:, sl],
                           b_s[pl.ds(off, tq), h:h + 1], mask)
            m, l, acc = _softmax_next(st, vt_ref[i, sl, :],
                                      m_s[h:h + 1, :], l_s[h:h + 1, :], acc_s[sl, :])
            if finish:
                ot_s[sl, :] = acc / l
            else:
                m_s[h:h + 1, :] = m
                l_s[h:h + 1, :] = l
                acc_s[sl, :] = acc

    def body(i, carry):
        tile(i, None, False)
        return carry

    lax.fori_loop(0, j, body, 0)

    key = lax.broadcasted_iota(jnp.int32, (tq, tq), 0)
    qry = lax.broadcasted_iota(jnp.int32, (tq, tq), 1)
    tile(j, key <= qry, True)
    o_ref[...] = _rms(ot_s[...].T, g_ref[...]).astype(BF16)


def _attn(qk, vt, logf, qk_meta, vt_meta, bias_meta, g, *, seq, tq):
    n = qk.shape[0]
    nb = n // seq
    per_seq = seq // tq
    tmeta = qk_meta.shape[0]
    return pl.pallas_call(
        functools.partial(_attn_kernel, tq=tq),
        grid=(nb, per_seq),
        in_specs=[
            pl.BlockSpec((tq, ATT_W), lambda b, j: (b * per_seq + j, 0)),
            pl.BlockSpec((seq, ATT_W), lambda b, j: (b, 1)),
            pl.BlockSpec((per_seq, ATT_W, tq), lambda b, j: (b, 0, 0)),
            pl.BlockSpec((seq, LANES), lambda b, j: (b, 0)),
            pl.BlockSpec((tmeta, ATT_W), lambda b, j: (0, 1)),
            pl.BlockSpec((1, ATT_W, tmeta), lambda b, j: (0, 0, 0)),
            pl.BlockSpec(bias_meta.shape, lambda b, j: (0, 0)),
            pl.BlockSpec(g.shape, lambda b, j: (0, 0)),
        ],
        out_specs=pl.BlockSpec((tq, ATT_W), lambda b, j: (b * per_seq + j, 0)),
        out_shape=jax.ShapeDtypeStruct((n, ATT_W), BF16),
        scratch_shapes=[
            pltpu.VMEM((seq, LANES), F32),
            pltpu.VMEM((N_HEADS, tq), F32),
            pltpu.VMEM((N_HEADS, tq), F32),
            pltpu.VMEM((ATT_W, tq), F32),
            pltpu.VMEM((ATT_W, tq), F32),
        ],
        compiler_params=pltpu.CompilerParams(
            dimension_semantics=("arbitrary", "arbitrary"), vmem_limit_bytes=VMEM_LIMIT),
        name="attn",
    )(qk, qk, vt, logf, qk_meta, vt_meta, bias_meta, g)


def _rec_kernel(xr_ref, yr_ref, cw_ref, cb_ref, wg_ref, bg_ref, lru_ref, g_ref,
                h0_ref, tail0_ref, out_ref, hn_ref, tailn_ref,
                ext_s, a_s, u_s, h_s, *, nb, tc, pitch, state_t):
    step = pl.program_id(0)
    n_slab = REC_W // LANES
    half = REC_W // 2

    @pl.when(step == 0)
    def _():
        for b in range(nb):
            ext_s[b, tc:tc + SUBLANES, :] = tail0_ref[0]
        h_s[...] = jnp.broadcast_to(h0_ref[...], h_s.shape)

    log_sig_l = _log_sigmoid(lru_ref[...])

    for b in range(nb):
        ext_s[b, 0:SUBLANES, :] = ext_s[b, tc:tc + SUBLANES, :]
        ext_s[b, SUBLANES:SUBLANES + tc, :] = xr_ref[b]
        xc = cb_ref[...]
        for k in range(CONV_WIDTH):
            lo = SUBLANES - (CONV_WIDTH - 1) + k
            xc = xc + cw_ref[k:k + 1, :] * ext_s[b, lo:lo + tc, :]
        xcb = xc.astype(BF16)
        g0 = _dot(xcb[:, :half], wg_ref[0])
        g1 = _dot(xcb[:, half:], wg_ref[1])
        ra = jnp.concatenate([g0[:, :half], g1[:, :half]], axis=-1) + bg_ref[0:1, :]
        rx = jnp.concatenate([g0[:, half:], g1[:, half:]], axis=-1) + bg_ref[1:2, :]
        r = jax.nn.sigmoid(ra)
        i = jax.nn.sigmoid(rx)
        log_a = RG_C * r * log_sig_l
        a = jnp.exp(log_a)
        th = jnp.tanh(log_a)
        mult = jnp.sqrt(-2.0 * th / (1.0 - th))
        u = mult * i * xc
        for s in range(n_slab):
            a_s[s, b * pitch:b * pitch + tc, :] = a[:, s * LANES:(s + 1) * LANES]
            u_s[s, b * pitch:b * pitch + tc, :] = u[:, s * LANES:(s + 1) * LANES]

    def rows(t):
        return pl.ds(t, nb, stride=pitch) if nb > 1 else pl.ds(t, 1)

    def scan_body(t, hs):
        out = []
        for s in range(n_slab):
            hn = a_s[s, rows(t), :] * hs[s] + u_s[s, rows(t), :]
            u_s[s, rows(t), :] = hn
            out.append(hn)
        return tuple(out)

    hs = tuple(h_s[:, s * LANES:(s + 1) * LANES] for s in range(n_slab))
    hs = lax.fori_loop(0, tc, scan_body, hs, unroll=8)
    h_s[...] = jnp.concatenate(hs, axis=-1)

    hn_ref[...] = jnp.concatenate(
        [u_s[s, rows(state_t), :] for s in range(n_slab)], axis=-1)
    for b in range(nb):
        tailn_ref[b] = ext_s[b, state_t + 1:state_t + 1 + SUBLANES, :]
        hb = jnp.concatenate(
            [u_s[s, b * pitch:b * pitch + tc, :] for s in range(n_slab)], axis=-1)
        rec = hb * _gelu_tanh(yr_ref[b])
        out_ref[b] = _rms(rec, g_ref[...]).astype(BF16)


def _rec(xy, cw, cb, wg, bg, lru, g, h0, tail0, *, nb, seq, tc, state_t):
    xy3 = xy.reshape(nb, seq, 2 * REC_W)
    pitch = tc + SUBLANES
    n_slab = REC_W // LANES
    out, hn, tailn = pl.pallas_call(
        functools.partial(_rec_kernel, nb=nb, tc=tc, pitch=pitch, state_t=state_t),
        grid=(seq // tc,),
        in_specs=[
            pl.BlockSpec((nb, tc, REC_W), lambda t: (0, t, 0)),
            pl.BlockSpec((nb, tc, REC_W), lambda t: (0, t, 1)),
            _const_spec(cw.shape),
            _const_spec(cb.shape),
            _const_spec(wg.shape),
            _const_spec(bg.shape),
            _const_spec(lru.shape),
            _const_spec(g.shape),
            _const_spec(h0.shape),
            _const_spec(tail0.shape),
        ],
        out_specs=[
            pl.BlockSpec((nb, tc, REC_W), lambda t: (0, t, 0)),
            pl.BlockSpec((nb, REC_W), lambda t: (0, 0)),
            pl.BlockSpec((nb, SUBLANES, REC_W), lambda t: (0, 0, 0)),
        ],
        out_shape=[
            jax.ShapeDtypeStruct((nb, seq, REC_W), BF16),
            jax.ShapeDtypeStruct((nb, REC_W), F32),
            jax.ShapeDtypeStruct((nb, SUBLANES, REC_W), F32),
        ],
        scratch_shapes=[
            pltpu.VMEM((nb, tc + SUBLANES, REC_W), F32),
            pltpu.VMEM((n_slab, nb * pitch, LANES), F32),
            pltpu.VMEM((n_slab, nb * pitch, LANES), F32),
            pltpu.VMEM((nb, REC_W), F32),
        ],
        compiler_params=pltpu.CompilerParams(
            dimension_semantics=("arbitrary",), vmem_limit_bytes=VMEM_LIMIT),
        name="rec",
    )(xy3, xy3, cw, cb, wg, bg, lru, g, h0, tail0)
    return out.reshape(nb * seq, REC_W), hn, tailn


def _post_kernel(h_ref, an_ref, rn_ref, wo_ref, g_ref, wu_ref, wd_ref, fg_ref, o_ref,
                 *, final, ff_chunk):
    h1 = (h_ref[...] + _dot(an_ref[...], wo_ref[0:ATT_W, :])
          + _dot(rn_ref[...], wo_ref[ATT_W:ATT_W + REC_W, :]))
    z = _rms(h1, g_ref[...]).astype(BF16)
    down = None
    for c in range(wu_ref.shape[1] // ff_chunk):
        u = jnp.maximum(_dot(z, wu_ref[:, c * ff_chunk:(c + 1) * ff_chunk]), 0.0)
        part = _dot((u * u).astype(BF16), wd_ref[c * ff_chunk:(c + 1) * ff_chunk, :])
        down = part if down is None else down + part
    acc = h1 + down
    if final:
        acc = _rms(acc, fg_ref[...])
    o_ref[...] = acc


def _post(h, an, rn, wo, g, wu, wd, fg, *, tm, final):
    n, d = h.shape
    return pl.pallas_call(
        functools.partial(_post_kernel, final=final, ff_chunk=1024),
        grid=(n // tm,),
        in_specs=[
            pl.BlockSpec((tm, d), lambda i: (i, 0)),
            pl.BlockSpec((tm, ATT_W), lambda i: (i, 0)),
            pl.BlockSpec((tm, REC_W), lambda i: (i, 0)),
            _const_spec(wo.shape),
            _const_spec(g.shape),
            _const_spec(wu.shape),
            _const_spec(wd.shape),
            _const_spec(fg.shape),
        ],
        out_specs=pl.BlockSpec((tm, d), lambda i: (i, 0)),
        out_shape=jax.ShapeDtypeStruct((n, d), F32),
        compiler_params=pltpu.CompilerParams(
            dimension_semantics=("arbitrary",), vmem_limit_bytes=VMEM_LIMIT),
        name="post",
    )(h, an, rn, wo, g, wu, wd, fg)


def _block_diag_gates(w_a, w_x):
    nblk, blk, _ = w_a.shape
    per = nblk // 2
    eye = jnp.eye(per, dtype=w_a.dtype)

    def bd(w):
        w4 = w.reshape(2, per, blk, blk)
        full = jnp.einsum('gpij,pq->gpiqj', w4, eye)
        return full.reshape(2, per * blk, per * blk)

    return jnp.concatenate([bd(w_a), bd(w_x)], axis=-1).astype(BF16)


def kernel(x, meta, attn_norm_g, w_in, b_f, conv_w, conv_b, w_gate_a, b_gate_a,
           w_gate_x, b_gate_x, lru_L, attn_out_g, rec_out_g, w_out, mlp_norm_g,
           w_up, w_down, final_g):
    nb, seq, d = x.shape
    depth = w_in.shape[0]
    scale = 1.0 / math.sqrt(HEAD_DIM)
    c_q, c_f, c_x = ATT_W, 3 * ATT_W, 3 * ATT_W + N_HEADS

    h = x.reshape(nb * seq, d)
    hm = jnp.concatenate([meta, jnp.zeros((META_PAD - N_META, d), F32)], axis=0)
    fg = final_g.reshape(1, d)

    for l in range(depth):
        wl = w_in[l]
        wqk = jnp.concatenate(
            [wl[:, :c_q] * (scale * LOG2E), wl[:, c_q:2 * c_q]], axis=1).astype(BF16)
        wvt = wl[:, 2 * c_q:c_f].T.astype(BF16)
        wf = jnp.concatenate(
            [wl[:, c_f:c_x], jnp.zeros((d, LANES - N_HEADS), F32)], axis=1).astype(BF16)
        bf = jnp.concatenate([b_f[l], jnp.zeros((LANES - N_HEADS,), F32)]).reshape(1, LANES)
        wxy = wl[:, c_x:].astype(BF16)
        g_in = attn_norm_g[l].reshape(1, d)
        cw, cb = conv_w[l], conv_b[l].reshape(1, REC_W)
        wg = _block_diag_gates(w_gate_a[l], w_gate_x[l])
        bg = jnp.stack([b_gate_a[l], b_gate_x[l]])
        lru = lru_L[l].reshape(1, REC_W)
        g_att = attn_out_g[l].reshape(1, ATT_W)
        g_rec = rec_out_g[l].reshape(1, REC_W)
        wo = w_out[l].astype(BF16)
        g_mlp = mlp_norm_g[l].reshape(1, d)
        wu = w_up[l].astype(BF16)
        wd = w_down[l].astype(BF16)
        last = l == depth - 1

        qk_m, vt_m, logf_m, xy_m = _inproj(hm, g_in, wqk, wvt, wf, bf, wxy, tm=META_PAD)
        an_m, bias_m = _meta_attn(qk_m, vt_m, logf_m, g_att)
        rn_m, h_state, x_tail = _rec(
            xy_m, cw, cb, wg, bg, lru, g_rec,
            jnp.zeros((1, REC_W), F32), jnp.zeros((1, SUBLANES, REC_W), F32),
            nb=1, seq=META_PAD, tc=META_PAD, state_t=N_META - 1)
        if not last:
            hm = _post(hm, an_m, rn_m, wo, g_mlp, wu, wd, fg, tm=META_PAD, final=False)

        qk, vt, logf, xy = _inproj(h, g_in, wqk, wvt, wf, bf, wxy, tm=ATT_TILE)
        an = _attn(qk, vt, logf, qk_m, vt_m, bias_m, g_att, seq=seq, tq=ATT_TILE)
        tc = 128
        rn, _, _ = _rec(xy, cw, cb, wg, bg, lru, g_rec, h_state, x_tail,
                        nb=nb, seq=seq, tc=tc, state_t=tc - 1)
        h = _post(h, an, rn, wo, g_mlp, wu, wd, fg, tm=512, final=last)

    return h.reshape(nb, seq, d)
```

```python
import functools
import math

import jax
import jax.numpy as jnp
from jax import lax
from jax.experimental import pallas as pl
from jax.experimental.pallas import tpu as pltpu

F32 = jnp.float32
BF16 = jnp.bfloat16

N_META = 16
META_PAD = 128
N_HEADS = 8
HEAD_DIM = 64
ATT_W = N_HEADS * HEAD_DIM
REC_W = 512
CONV_WIDTH = 4
RG_C = 8.0
NORM_EPS = 1e-6
LANES = 128
SUBLANES = 8
LOG2E = math.log2(math.e)
ATT_TILE = 512
ROW_TILE = 512
REC_CHUNK = 128
VMEM_LIMIT = 56 * 1024 * 1024


def _rms(x, g):
    ms = jnp.mean(x * x, axis=-1, keepdims=True)
    return x * lax.rsqrt(ms + NORM_EPS) * g


def _log_sigmoid(x):
    return jnp.minimum(x, 0.0) - jnp.log1p(jnp.exp(-jnp.abs(x)))


def _gelu_tanh(x):
    c = math.sqrt(2.0 / math.pi)
    return x * (0.5 * (1.0 + jnp.tanh(c * (x + 0.044715 * (x * x * x)))))


def _dot(a, b):
    return jnp.dot(a, b, preferred_element_type=F32)


def _dot_nt(a, b):
    return lax.dot_general(a, b, (((1,), (1,)), ((), ())), preferred_element_type=F32)


def _const_spec(shape):
    nd = len(shape)
    return pl.BlockSpec(shape, lambda *_: (0,) * nd, pipeline_mode=pl.Buffered(1))


def _inproj_kernel(h_ref, g_ref, wk_ref, wqvt_ref, wf_ref, bf_ref, wxy_ref,
                   k_ref, qvt_ref, logf_ref, xy_ref):
    z = _rms(h_ref[...], g_ref[...]).astype(BF16)
    k_ref[...] = _dot(z, wk_ref[...]).astype(BF16)
    qvt = _dot_nt(wqvt_ref[...], z).astype(BF16)
    tile = qvt_ref.shape[2]
    for s in range(qvt_ref.shape[0]):
        qvt_ref[s] = qvt[:, s * tile:(s + 1) * tile]
    xy_ref[...] = _dot(z, wxy_ref[...])
    logf_ref[...] = _log_sigmoid(_dot(z, wf_ref[...]) + bf_ref[...])


def _inproj(h, g, wk, wqvt, wf, bf, wxy, *, tm, tile):
    n, d = h.shape
    per = tm // tile
    return pl.pallas_call(
        _inproj_kernel,
        grid=(n // tm,),
        in_specs=[
            pl.BlockSpec((tm, d), lambda i: (i, 0)),
            _const_spec(g.shape),
            _const_spec(wk.shape),
            _const_spec(wqvt.shape),
            _const_spec(wf.shape),
            _const_spec(bf.shape),
            _const_spec(wxy.shape),
        ],
        out_specs=[
            pl.BlockSpec((tm, ATT_W), lambda i: (i, 0)),
            pl.BlockSpec((per, 2 * ATT_W, tile), lambda i: (i, 0, 0)),
            pl.BlockSpec((tm, LANES), lambda i: (i, 0)),
            pl.BlockSpec((tm, 2 * REC_W), lambda i: (i, 0)),
        ],
        out_shape=[
            jax.ShapeDtypeStruct((n, ATT_W), BF16),
            jax.ShapeDtypeStruct((n // tile, 2 * ATT_W, tile), BF16),
            jax.ShapeDtypeStruct((n, LANES), F32),
            jax.ShapeDtypeStruct((n, 2 * REC_W), F32),
        ],
        compiler_params=pltpu.CompilerParams(
            dimension_semantics=("arbitrary",), vmem_limit_bytes=VMEM_LIMIT),
        name="inproj",
    )(h, g, wk, wqvt, wf, bf, wxy)


def _neg_cumsum_cols(lf_ref, out_ref):
    r = lax.broadcasted_iota(jnp.int32, (LANES, LANES), 0)
    c = lax.broadcasted_iota(jnp.int32, (LANES, LANES), 1)
    lower = (r >= c).astype(BF16)
    carry = jnp.zeros((1, LANES), F32)
    for blk in range(lf_ref.shape[0] // LANES):
        x = lf_ref[blk * LANES:(blk + 1) * LANES, :]
        hi = x.astype(BF16)
        r1 = x - hi.astype(F32)
        mid = r1.astype(BF16)
        lo = (r1 - mid.astype(F32)).astype(BF16)
        cs = _dot(lower, hi) + _dot(lower, mid) + _dot(lower, lo) + carry
        out_ref[blk * LANES:(blk + 1) * LANES, :] = cs * (-LOG2E)
        carry = cs[LANES - 1:LANES, :]


def _scores_t(kh, qth, bias_col, mask):
    st = _dot(kh, qth) + bias_col
    if mask is not None:
        st = jnp.where(mask, st, -jnp.inf)
    return st


def _softmax_first(st, vth):
    m = jnp.max(st, axis=0, keepdims=True)
    p = jnp.exp2(st - m)
    l = jnp.sum(p, axis=0, keepdims=True)
    return m, l, _dot(vth, p.astype(BF16))


def _softmax_next(st, vth, m, l, acc):
    m_new = jnp.maximum(m, jnp.max(st, axis=0, keepdims=True))
    alpha = jnp.exp2(m - m_new)
    p = jnp.exp2(st - m_new)
    l_new = alpha * l + jnp.sum(p, axis=0, keepdims=True)
    return m_new, l_new, alpha * acc + _dot(vth, p.astype(BF16))


def _meta_attn_kernel(k_ref, qt_ref, vt_ref, lf_ref, g_ref, o_ref, bm_ref, b_s, ot_s):
    t = k_ref.shape[0]
    _neg_cumsum_cols(lf_ref, b_s)
    row = lax.broadcasted_iota(jnp.int32, (t, LANES), 0)
    bm_ref[...] = jnp.where(row < N_META, b_s[...] - b_s[N_META - 1:N_META, :], -jnp.inf)
    key = lax.broadcasted_iota(jnp.int32, (t, t), 0)
    qry = lax.broadcasted_iota(jnp.int32, (t, t), 1)
    causal = key <= qry
    for h in range(N_HEADS):
        sl = slice(h * HEAD_DIM, (h + 1) * HEAD_DIM)
        st = _scores_t(k_ref[:, sl], qt_ref[0, sl, :], b_s[:, h:h + 1], causal)
        _, l, acc = _softmax_first(st, vt_ref[0, sl, :])
        ot_s[sl, :] = acc / l
    o_ref[...] = _rms(ot_s[...].T, g_ref[...]).astype(BF16)


def _meta_attn(k, qvt, logf, g):
    t = k.shape[0]
    return pl.pallas_call(
        _meta_attn_kernel,
        grid=(1,),
        in_specs=[
            pl.BlockSpec((t, ATT_W), lambda i: (0, 0)),
            pl.BlockSpec((1, ATT_W, t), lambda i: (0, 0, 0)),
            pl.BlockSpec((1, ATT_W, t), lambda i: (0, 1, 0)),
            pl.BlockSpec((t, LANES), lambda i: (0, 0)),
            pl.BlockSpec(g.shape, lambda i: (0, 0)),
        ],
        out_specs=[
            pl.BlockSpec((t, ATT_W), lambda i: (0, 0)),
            pl.BlockSpec((t, LANES), lambda i: (0, 0)),
        ],
        out_shape=[
            jax.ShapeDtypeStruct((t, ATT_W), BF16),
            jax.ShapeDtypeStruct((t, LANES), F32),
        ],
        scratch_shapes=[pltpu.VMEM((t, LANES), F32), pltpu.VMEM((ATT_W, t), F32)],
        name="meta_attn",
    )(k, qvt, qvt, logf, g)


def _attn_kernel(qt_ref, k_ref, vt_ref, lf_ref, km_ref, vtm_ref, bm_ref, g_ref,
                 o_ref, b_s, m_s, l_s, acc_s, ot_s, *, tq):
    j = pl.program_id(1)

    @pl.when(j == 0)
    def _():
        _neg_cumsum_cols(lf_ref, b_s)

    heads = [slice(h * HEAD_DIM, (h + 1) * HEAD_DIM) for h in range(N_HEADS)]

    sts = [_scores_t(km_ref[0:N_META, sl], qt_ref[0, sl, :], bm_ref[0:N_META, h:h + 1], None)
           for h, sl in enumerate(heads)]
    for h, sl in enumerate(heads):
        m, l, acc = _softmax_first(sts[h], vtm_ref[0, sl, 0:N_META])
        m_s[h:h + 1, :] = m
        l_s[h:h + 1, :] = l
        acc_s[sl, :] = acc

    def tile(i, mask, finish):
        off = pl.multiple_of(i * tq, tq)

        def scores(h):
            sl = heads[h]
            return _scores_t(k_ref[pl.ds(off, tq), sl], qt_ref[0, sl, :],
                             b_s[pl.ds(off, tq), h:h + 1], mask)

        sts = [scores(0)]
        for h, sl in enumerate(heads):
            if h + 1 < N_HEADS:
                sts.append(scores(h + 1))
            m, l, acc = _softmax_next(sts[h], vt_ref[i, sl, :],
                                      m_s[h:h + 1, :], l_s[h:h + 1, :], acc_s[sl, :])
            if finish:
                ot_s[sl, :] = acc / l
            else:
                m_s[h:h + 1, :] = m
                l_s[h:h + 1, :] = l
                acc_s[sl, :] = acc

    def body(i, carry):
        tile(i, None, False)
        return carry

    lax.fori_loop(0, j, body, 0)

    key = lax.broadcasted_iota(jnp.int32, (tq, tq), 0)
    qry = lax.broadcasted_iota(jnp.int32, (tq, tq), 1)
    tile(j, key <= qry, True)
    o_ref[...] = _rms(ot_s[...].T, g_ref[...]).astype(BF16)


def _attn(k, qvt, logf, k_meta, qvt_meta, bias_meta, g, *, seq, tq):
    n = k.shape[0]
    nb = n // seq
    per_seq = seq // tq
    tmeta = k_meta.shape[0]
    return pl.pallas_call(
        functools.partial(_attn_kernel, tq=tq),
        grid=(nb, per_seq),
        in_specs=[
            pl.BlockSpec((1, ATT_W, tq), lambda b, j: (b * per_seq + j, 0, 0)),
            pl.BlockSpec((seq, ATT_W), lambda b, j: (b, 0)),
            pl.BlockSpec((per_seq, ATT_W, tq), lambda b, j: (b, 1, 0)),
            pl.BlockSpec((seq, LANES), lambda b, j: (b, 0)),
            pl.BlockSpec((tmeta, ATT_W), lambda b, j: (0, 0)),
            pl.BlockSpec((1, ATT_W, tmeta), lambda b, j: (0, 1, 0)),
            pl.BlockSpec(bias_meta.shape, lambda b, j: (0, 0)),
            pl.BlockSpec(g.shape, lambda b, j: (0, 0)),
        ],
        out_specs=pl.BlockSpec((tq, ATT_W), lambda b, j: (b * per_seq + j, 0)),
        out_shape=jax.ShapeDtypeStruct((n, ATT_W), BF16),
        scratch_shapes=[
            pltpu.VMEM((seq, LANES), F32),
            pltpu.VMEM((N_HEADS, tq), F32),
            pltpu.VMEM((N_HEADS, tq), F32),
            pltpu.VMEM((ATT_W, tq), F32),
            pltpu.VMEM((ATT_W, tq), F32),
        ],
        compiler_params=pltpu.CompilerParams(
            dimension_semantics=("arbitrary", "arbitrary"), vmem_limit_bytes=VMEM_LIMIT),
        name="attn",
    )(qvt, k, qvt, logf, k_meta, qvt_meta, bias_meta, g)


def _rec_kernel(xr_ref, yr_ref, cw_ref, cb_ref, wg_ref, bg_ref, lru_ref, g_ref,
                h0_ref, tail0_ref, out_ref, hn_ref, tailn_ref,
                ext_s, a_s, u_s, h_s, *, nb, tc, pitch, state_t):
    step = pl.program_id(0)
    n_slab = REC_W // LANES
    half = REC_W // 2

    @pl.when(step == 0)
    def _():
        for b in range(nb):
            ext_s[b, tc:tc + SUBLANES, :] = tail0_ref[0]
        h_s[...] = jnp.broadcast_to(h0_ref[...], h_s.shape)

    log_sig_l = _log_sigmoid(lru_ref[...])

    for b in range(nb):
        ext_s[b, 0:SUBLANES, :] = ext_s[b, tc:tc + SUBLANES, :]
        ext_s[b, SUBLANES:SUBLANES + tc, :] = xr_ref[b]
        xc = cb_ref[...]
        for k in range(CONV_WIDTH):
            lo = SUBLANES - (CONV_WIDTH - 1) + k
            xc = xc + cw_ref[k:k + 1, :] * ext_s[b, lo:lo + tc, :]
        xcb = xc.astype(BF16)
        g0 = _dot(xcb[:, :half], wg_ref[0])
        g1 = _dot(xcb[:, half:], wg_ref[1])
        ra = jnp.concatenate([g0[:, :half], g1[:, :half]], axis=-1) + bg_ref[0:1, :]
        rx = jnp.concatenate([g0[:, half:], g1[:, half:]], axis=-1) + bg_ref[1:2, :]
        r = jax.nn.sigmoid(ra)
        i = jax.nn.sigmoid(rx)
        log_a = RG_C * r * log_sig_l
        a = jnp.exp(log_a)
        th = jnp.tanh(log_a)
        mult = jnp.sqrt(-2.0 * th / (1.0 - th))
        u = mult * i * xc
        for s in range(n_slab):
            a_s[s, b * pitch:b * pitch + tc, :] = a[:, s * LANES:(s + 1) * LANES]
            u_s[s, b * pitch:b * pitch + tc, :] = u[:, s * LANES:(s + 1) * LANES]

    def rows(t):
        return pl.ds(t, nb, stride=pitch) if nb > 1 else pl.ds(t, 1)

    def scan_body(t, hs):
        out = []
        for s in range(n_slab):
            hn = a_s[s, rows(t), :] * hs[s] + u_s[s, rows(t), :]
            u_s[s, rows(t), :] = hn
            out.append(hn)
        return tuple(out)

    hs = tuple(h_s[:, s * LANES:(s + 1) * LANES] for s in range(n_slab))
    hs = lax.fori_loop(0, tc, scan_body, hs, unroll=8)
    h_s[...] = jnp.concatenate(hs, axis=-1)

    hn_ref[...] = jnp.concatenate(
        [u_s[s, rows(state_t), :] for s in range(n_slab)], axis=-1)
    for b in range(nb):
        tailn_ref[b] = ext_s[b, state_t + 1:state_t + 1 + SUBLANES, :]
        hb = jnp.concatenate(
            [u_s[s, b * pitch:b * pitch + tc, :] for s in range(n_slab)], axis=-1)
        rec = hb * _gelu_tanh(yr_ref[b])
        out_ref[b] = _rms(rec, g_ref[...]).astype(BF16)


def _rec(xy, cw, cb, wg, bg, lru, g, h0, tail0, *, nb, seq, tc, state_t):
    xy3 = xy.reshape(nb, seq, 2 * REC_W)
    pitch = tc + SUBLANES
    n_slab = REC_W // LANES
    out, hn, tailn = pl.pallas_call(
        functools.partial(_rec_kernel, nb=nb, tc=tc, pitch=pitch, state_t=state_t),
        grid=(seq // tc,),
        in_specs=[
            pl.BlockSpec((nb, tc, REC_W), lambda t: (0, t, 0)),
            pl.BlockSpec((nb, tc, REC_W), lambda t: (0, t, 1)),
            _const_spec(cw.shape),
            _const_spec(cb.shape),
            _const_spec(wg.shape),
            _const_spec(bg.shape),
            _const_spec(lru.shape),
            _const_spec(g.shape),
            _const_spec(h0.shape),
            _const_spec(tail0.shape),
        ],
        out_specs=[
            pl.BlockSpec((nb, tc, REC_W), lambda t: (0, t, 0)),
            pl.BlockSpec((nb, REC_W), lambda t: (0, 0)),
            pl.BlockSpec((nb, SUBLANES, REC_W), lambda t: (0, 0, 0)),
        ],
        out_shape=[
            jax.ShapeDtypeStruct((nb, seq, REC_W), BF16),
            jax.ShapeDtypeStruct((nb, REC_W), F32),
            jax.ShapeDtypeStruct((nb, SUBLANES, REC_W), F32),
        ],
        scratch_shapes=[
            pltpu.VMEM((nb, tc + SUBLANES, REC_W), F32),
            pltpu.VMEM((n_slab, nb * pitch, LANES), F32),
            pltpu.VMEM((n_slab, nb * pitch, LANES), F32),
            pltpu.VMEM((nb, REC_W), F32),
        ],
        compiler_params=pltpu.CompilerParams(
            dimension_semantics=("arbitrary",), vmem_limit_bytes=VMEM_LIMIT),
        name="rec",
    )(xy3, xy3, cw, cb, wg, bg, lru, g, h0, tail0)
    return out.reshape(nb * seq, REC_W), hn, tailn


def _post_kernel(h_ref, an_ref, rn_ref, wo_ref, g_ref, wu_ref, wd_ref, fg_ref, o_ref,
                 *, final, ff_chunk):
    h1 = (h_ref[...] + _dot(an_ref[...], wo_ref[0:ATT_W, :])
          + _dot(rn_ref[...], wo_ref[ATT_W:ATT_W + REC_W, :]))
    z = _rms(h1, g_ref[...]).astype(BF16)
    down = None
    for c in range(wu_ref.shape[1] // ff_chunk):
        u = jnp.maximum(_dot(z, wu_ref[:, c * ff_chunk:(c + 1) * ff_chunk]), 0.0)
        part = _dot((u * u).astype(BF16), wd_ref[c * ff_chunk:(c + 1) * ff_chunk, :])
        down = part if down is None else down + part
    acc = h1 + down
    if final:
        acc = _rms(acc, fg_ref[...])
    o_ref[...] = acc


def _post(h, an, rn, wo, g, wu, wd, fg, *, tm, final):
    n, d = h.shape
    return pl.pallas_call(
        functools.partial(_post_kernel, final=final, ff_chunk=1024),
        grid=(n // tm,),
        in_specs=[
            pl.BlockSpec((tm, d), lambda i: (i, 0)),
            pl.BlockSpec((tm, ATT_W), lambda i: (i, 0)),
            pl.BlockSpec((tm, REC_W), lambda i: (i, 0)),
            _const_spec(wo.shape),
            _const_spec(g.shape),
            _const_spec(wu.shape),
            _const_spec(wd.shape),
            _const_spec(fg.shape),
        ],
        out_specs=pl.BlockSpec((tm, d), lambda i: (i, 0)),
        out_shape=jax.ShapeDtypeStruct((n, d), F32),
        compiler_params=pltpu.CompilerParams(
            dimension_semantics=("arbitrary",), vmem_limit_bytes=VMEM_LIMIT),
        name="post",
    )(h, an, rn, wo, g, wu, wd, fg)


def _block_diag_gates(w_a, w_x):
    nblk, blk, _ = w_a.shape
    per = nblk // 2
    eye = jnp.eye(per, dtype=w_a.dtype)

    def bd(w):
        w4 = w.reshape(2, per, blk, blk)
        full = jnp.einsum('gpij,pq->gpiqj', w4, eye)
        return full.reshape(2, per * blk, per * blk)

    return jnp.concatenate([bd(w_a), bd(w_x)], axis=-1).astype(BF16)


def kernel(x, meta, attn_norm_g, w_in, b_f, conv_w, conv_b, w_gate_a, b_gate_a,
           w_gate_x, b_gate_x, lru_L, attn_out_g, rec_out_g, w_out, mlp_norm_g,
           w_up, w_down, final_g):
    nb, seq, d = x.shape
    depth = w_in.shape[0]
    scale = 1.0 / math.sqrt(HEAD_DIM)
    c_q, c_f, c_x = ATT_W, 3 * ATT_W, 3 * ATT_W + N_HEADS

    h = x.reshape(nb * seq, d)
    hm = jnp.concatenate([meta, jnp.zeros((META_PAD - N_META, d), F32)], axis=0)
    fg = final_g.reshape(1, d)

    for l in range(depth):
        wl = w_in[l]
        wk = wl[:, c_q:2 * c_q].astype(BF16)
        wqvt = jnp.concatenate(
            [wl[:, :c_q] * (scale * LOG2E), wl[:, 2 * c_q:c_f]], axis=1).T.astype(BF16)
        wf = jnp.concatenate(
            [wl[:, c_f:c_x], jnp.zeros((d, LANES - N_HEADS), F32)], axis=1).astype(BF16)
        bf = jnp.concatenate([b_f[l], jnp.zeros((LANES - N_HEADS,), F32)]).reshape(1, LANES)
        wxy = wl[:, c_x:].astype(BF16)
        g_in = attn_norm_g[l].reshape(1, d)
        cw, cb = conv_w[l], conv_b[l].reshape(1, REC_W)
        wg = _block_diag_gates(w_gate_a[l], w_gate_x[l])
        bg = jnp.stack([b_gate_a[l], b_gate_x[l]])
        lru = lru_L[l].reshape(1, REC_W)
        g_att = attn_out_g[l].reshape(1, ATT_W)
        g_rec = rec_out_g[l].reshape(1, REC_W)
        wo = w_out[l].astype(BF16)
        g_mlp = mlp_norm_g[l].reshape(1, d)
        wu = w_up[l].astype(BF16)
        wd = w_down[l].astype(BF16)
        last = l == depth - 1

        k_m, qvt_m, logf_m, xy_m = _inproj(hm, g_in, wk, wqvt, wf, bf, wxy,
                                           tm=META_PAD, tile=META_PAD)
        an_m, bias_m = _meta_attn(k_m, qvt_m, logf_m, g_att)
        rn_m, h_state, x_tail = _rec(
            xy_m, cw, cb, wg, bg, lru, g_rec,
            jnp.zeros((1, REC_W), F32), jnp.zeros((1, SUBLANES, REC_W), F32),
            nb=1, seq=META_PAD, tc=META_PAD, state_t=N_META - 1)
        if not last:
            hm = _post(hm, an_m, rn_m, wo, g_mlp, wu, wd, fg, tm=META_PAD, final=False)

        k, qvt, logf, xy = _inproj(h, g_in, wk, wqvt, wf, bf, wxy, tm=ROW_TILE, tile=ATT_TILE)
        an = _attn(k, qvt, logf, k_m, qvt_m, bias_m, g_att, seq=seq, tq=ATT_TILE)
        rn, _, _ = _rec(xy, cw, cb, wg, bg, lru, g_rec, h_state, x_tail,
                        nb=nb, seq=seq, tc=REC_CHUNK, state_t=REC_CHUNK - 1)
        h = _post(h, an, rn, wo, g_mlp, wu, wd, fg, tm=ROW_TILE, final=last)

    return h.reshape(nb, seq, d)
```

```python
import functools
import math

import jax
import jax.numpy as jnp
from jax import lax
from jax.experimental import pallas as pl
from jax.experimental.pallas import tpu as pltpu

F32 = jnp.float32
BF16 = jnp.bfloat16

N_META = 16
META_PAD = 128
N_HEADS = 8
HEAD_DIM = 64
ATT_W = N_HEADS * HEAD_DIM
REC_W = 512
CONV_WIDTH = 4
RG_C = 8.0
NORM_EPS = 1e-6
LANES = 128
SUBLANES = 8
LOG2E = math.log2(math.e)
ATT_TILE = 512
ROW_TILE = 512
REC_CHUNK = 128
W_CHUNKS = 8
SUM_ROWS = 16
ACC_ROWS = HEAD_DIM + SUM_ROWS
VMEM_LIMIT = 56 * 1024 * 1024


def _rms(x, g):
    ms = jnp.mean(x * x, axis=-1, keepdims=True)
    return x * lax.rsqrt(ms + NORM_EPS) * g


def _log_sigmoid(x):
    return jnp.minimum(x, 0.0) - jnp.log1p(jnp.exp(-jnp.abs(x)))


def _gelu_tanh(x):
    c = math.sqrt(2.0 / math.pi)
    return x * (0.5 * (1.0 + jnp.tanh(c * (x + 0.044715 * (x * x * x)))))


def _dot(a, b):
    return jnp.dot(a, b, preferred_element_type=F32)


def _dot_nt(a, b):
    return lax.dot_general(a, b, (((1,), (1,)), ((), ())), preferred_element_type=F32)


def _const_spec(shape):
    nd = len(shape)
    return pl.BlockSpec(shape, lambda *_: (0,) * nd, pipeline_mode=pl.Buffered(1))


def _inproj_kernel(h_ref, g_ref, wk_ref, wqvt_ref, wf_ref, bf_ref, wxy_ref,
                   k_ref, qvt_ref, logf_ref, xy_ref):
    z = _rms(h_ref[...], g_ref[...]).astype(BF16)
    k_ref[...] = _dot(z, wk_ref[...]).astype(BF16)
    qvt = _dot_nt(wqvt_ref[...], z).astype(BF16)
    tile = qvt_ref.shape[2]
    for s in range(qvt_ref.shape[0]):
        qvt_ref[s] = qvt[:, s * tile:(s + 1) * tile]
    xy_ref[...] = _dot(z, wxy_ref[...])
    logf_ref[...] = _log_sigmoid(_dot(z, wf_ref[...]) + bf_ref[...])


def _inproj(h, g, wk, wqvt, wf, bf, wxy, *, tm, tile):
    n, d = h.shape
    per = tm // tile
    return pl.pallas_call(
        _inproj_kernel,
        grid=(n // tm,),
        in_specs=[
            pl.BlockSpec((tm, d), lambda i: (i, 0)),
            _const_spec(g.shape),
            _const_spec(wk.shape),
            _const_spec(wqvt.shape),
            _const_spec(wf.shape),
            _const_spec(bf.shape),
            _const_spec(wxy.shape),
        ],
        out_specs=[
            pl.BlockSpec((tm, ATT_W), lambda i: (i, 0)),
            pl.BlockSpec((per, 2 * ATT_W, tile), lambda i: (i, 0, 0)),
            pl.BlockSpec((tm, LANES), lambda i: (i, 0)),
            pl.BlockSpec((tm, 2 * REC_W), lambda i: (i, 0)),
        ],
        out_shape=[
            jax.ShapeDtypeStruct((n, ATT_W), BF16),
            jax.ShapeDtypeStruct((n // tile, 2 * ATT_W, tile), BF16),
            jax.ShapeDtypeStruct((n, LANES), F32),
            jax.ShapeDtypeStruct((n, 2 * REC_W), F32),
        ],
        compiler_params=pltpu.CompilerParams(
            dimension_semantics=("arbitrary",), vmem_limit_bytes=VMEM_LIMIT),
        name="inproj",
    )(h, g, wk, wqvt, wf, bf, wxy)


def _neg_cumsum_cols(lf_ref, out_ref):
    r = lax.broadcasted_iota(jnp.int32, (LANES, LANES), 0)
    c = lax.broadcasted_iota(jnp.int32, (LANES, LANES), 1)
    lower = (r >= c).astype(BF16)
    carry = jnp.zeros((1, LANES), F32)
    for blk in range(lf_ref.shape[0] // LANES):
        x = lf_ref[blk * LANES:(blk + 1) * LANES, :]
        hi = x.astype(BF16)
        r1 = x - hi.astype(F32)
        mid = r1.astype(BF16)
        lo = (r1 - mid.astype(F32)).astype(BF16)
        cs = _dot(lower, hi) + _dot(lower, mid) + _dot(lower, lo) + carry
        out_ref[blk * LANES:(blk + 1) * LANES, :] = cs * (-LOG2E)
        carry = cs[LANES - 1:LANES, :]


def _scores_t(kh, qth, bias_col, mask):
    st = _dot(kh, qth) + bias_col
    if mask is not None:
        st = jnp.where(mask, st, -jnp.inf)
    return st


def _pv_and_sum(vth, p):
    ones = jnp.ones((SUM_ROWS, vth.shape[1]), BF16)
    return _dot(jnp.concatenate([vth, ones], axis=0), p)


def _softmax_first(st, vth):
    m = jnp.max(st, axis=0, keepdims=True)
    return m, _pv_and_sum(vth, jnp.exp2(st - m).astype(BF16))


def _softmax_next(st, vth, m, acc):
    m_new = jnp.maximum(m, jnp.max(st, axis=0, keepdims=True))
    alpha = jnp.exp2(m - m_new)
    return m_new, alpha * acc + _pv_and_sum(vth, jnp.exp2(st - m_new).astype(BF16))


def _normalized(acc):
    return acc[:HEAD_DIM, :] / acc[HEAD_DIM:HEAD_DIM + 1, :]


def _meta_attn_kernel(k_ref, qt_ref, vt_ref, lf_ref, g_ref, o_ref, bm_ref, b_s, ot_s):
    t = k_ref.shape[0]
    _neg_cumsum_cols(lf_ref, b_s)
    row = lax.broadcasted_iota(jnp.int32, (t, LANES), 0)
    bm_ref[...] = jnp.where(row < N_META, b_s[...] - b_s[N_META - 1:N_META, :], -jnp.inf)
    key = lax.broadcasted_iota(jnp.int32, (t, t), 0)
    qry = lax.broadcasted_iota(jnp.int32, (t, t), 1)
    causal = key <= qry
    for h in range(N_HEADS):
        sl = slice(h * HEAD_DIM, (h + 1) * HEAD_DIM)
        st = _scores_t(k_ref[:, sl], qt_ref[0, sl, :], b_s[:, h:h + 1], causal)
        _, acc = _softmax_first(st, vt_ref[0, sl, :])
        ot_s[sl, :] = _normalized(acc)
    o_ref[...] = _rms(ot_s[...].T, g_ref[...]).astype(BF16)


def _meta_attn(k, qvt, logf, g):
    t = k.shape[0]
    return pl.pallas_call(
        _meta_attn_kernel,
        grid=(1,),
        in_specs=[
            pl.BlockSpec((t, ATT_W), lambda i: (0, 0)),
            pl.BlockSpec((1, ATT_W, t), lambda i: (0, 0, 0)),
            pl.BlockSpec((1, ATT_W, t), lambda i: (0, 1, 0)),
            pl.BlockSpec((t, LANES), lambda i: (0, 0)),
            pl.BlockSpec(g.shape, lambda i: (0, 0)),
        ],
        out_specs=[
            pl.BlockSpec((t, ATT_W), lambda i: (0, 0)),
            pl.BlockSpec((t, LANES), lambda i: (0, 0)),
        ],
        out_shape=[
            jax.ShapeDtypeStruct((t, ATT_W), BF16),
            jax.ShapeDtypeStruct((t, LANES), F32),
        ],
        scratch_shapes=[pltpu.VMEM((t, LANES), F32), pltpu.VMEM((ATT_W, t), F32)],
        name="meta_attn",
    )(k, qvt, qvt, logf, g)


def _attn_kernel(qt_ref, k_ref, vt_ref, lf_ref, km_ref, vtm_ref, bm_ref, g_ref,
                 o_ref, b_s, m_s, acc_s, ot_s, *, tq):
    j = pl.program_id(1)

    @pl.when(j == 0)
    def _():
        _neg_cumsum_cols(lf_ref, b_s)

    heads = [slice(h * HEAD_DIM, (h + 1) * HEAD_DIM) for h in range(N_HEADS)]
    accs = [slice(h * ACC_ROWS, (h + 1) * ACC_ROWS) for h in range(N_HEADS)]

    sts = [_scores_t(km_ref[0:N_META, sl], qt_ref[0, sl, :], bm_ref[0:N_META, h:h + 1], None)
           for h, sl in enumerate(heads)]
    for h, sl in enumerate(heads):
        m, acc = _softmax_first(sts[h], vtm_ref[0, sl, 0:N_META])
        m_s[h:h + 1, :] = m
        acc_s[accs[h], :] = acc

    def tile(i, mask, finish):
        off = pl.multiple_of(i * tq, tq)

        def scores(h):
            sl = heads[h]
            return _scores_t(k_ref[pl.ds(off, tq), sl], qt_ref[0, sl, :],
                             b_s[pl.ds(off, tq), h:h + 1], mask)

        sts = [scores(0)]
        for h, sl in enumerate(heads):
            if h + 1 < N_HEADS:
                sts.append(scores(h + 1))
            m, acc = _softmax_next(sts[h], vt_ref[i, sl, :], m_s[h:h + 1, :], acc_s[accs[h], :])
            if finish:
                ot_s[sl, :] = _normalized(acc)
            else:
                m_s[h:h + 1, :] = m
                acc_s[accs[h], :] = acc

    def body(i, carry):
        tile(i, None, False)
        return carry

    lax.fori_loop(0, j, body, 0)

    key = lax.broadcasted_iota(jnp.int32, (tq, tq), 0)
    qry = lax.broadcasted_iota(jnp.int32, (tq, tq), 1)
    tile(j, key <= qry, True)
    o_ref[...] = _rms(ot_s[...].T, g_ref[...]).astype(BF16)


def _attn(k, qvt, logf, k_meta, qvt_meta, bias_meta, g, *, seq, tq):
    n = k.shape[0]
    nb = n // seq
    per_seq = seq // tq
    tmeta = k_meta.shape[0]
    return pl.pallas_call(
        functools.partial(_attn_kernel, tq=tq),
        grid=(nb, per_seq),
        in_specs=[
            pl.BlockSpec((1, ATT_W, tq), lambda b, j: (b * per_seq + j, 0, 0)),
            pl.BlockSpec((seq, ATT_W), lambda b, j: (b, 0)),
            pl.BlockSpec((per_seq, ATT_W, tq), lambda b, j: (b, 1, 0)),
            pl.BlockSpec((seq, LANES), lambda b, j: (b, 0)),
            pl.BlockSpec((tmeta, ATT_W), lambda b, j: (0, 0)),
            pl.BlockSpec((1, ATT_W, tmeta), lambda b, j: (0, 1, 0)),
            pl.BlockSpec(bias_meta.shape, lambda b, j: (0, 0)),
            pl.BlockSpec(g.shape, lambda b, j: (0, 0)),
        ],
        out_specs=pl.BlockSpec((tq, ATT_W), lambda b, j: (b * per_seq + j, 0)),
        out_shape=jax.ShapeDtypeStruct((n, ATT_W), BF16),
        scratch_shapes=[
            pltpu.VMEM((seq, LANES), F32),
            pltpu.VMEM((N_HEADS, tq), F32),
            pltpu.VMEM((N_HEADS * ACC_ROWS, tq), F32),
            pltpu.VMEM((ATT_W, tq), F32),
        ],
        compiler_params=pltpu.CompilerParams(
            dimension_semantics=("arbitrary", "arbitrary"), vmem_limit_bytes=VMEM_LIMIT),
        name="attn",
    )(qvt, k, qvt, logf, k_meta, qvt_meta, bias_meta, g)


def _rec_kernel(xr_ref, yr_ref, cw_ref, cb_ref, wg_ref, bg_ref, lru_ref, g_ref,
                h0_ref, tail0_ref, out_ref, hn_ref, tailn_ref,
                ext_s, a_s, u_s, h_s, *, nb, tc, pitch, state_t):
    step = pl.program_id(0)
    n_slab = REC_W // LANES
    half = REC_W // 2

    @pl.when(step == 0)
    def _():
        for b in range(nb):
            ext_s[b, tc:tc + SUBLANES, :] = tail0_ref[0]
        h_s[...] = jnp.broadcast_to(h0_ref[...], h_s.shape)

    log_sig_l = _log_sigmoid(lru_ref[...])

    for b in range(nb):
        ext_s[b, 0:SUBLANES, :] = ext_s[b, tc:tc + SUBLANES, :]
        ext_s[b, SUBLANES:SUBLANES + tc, :] = xr_ref[b]
        ext = ext_s[b]
        run = cw_ref[0:1, :] * ext
        for k in range(1, CONV_WIDTH):
            run = pltpu.roll(run, 1, 0) + cw_ref[k:k + 1, :] * ext
        xc = run[SUBLANES:, :] + cb_ref[...]
        xcb = xc.astype(BF16)
        g0 = _dot(xcb[:, :half], wg_ref[0])
        g1 = _dot(xcb[:, half:], wg_ref[1])
        ra = jnp.concatenate([g0[:, :half], g1[:, :half]], axis=-1) + bg_ref[0:1, :]
        rx = jnp.concatenate([g0[:, half:], g1[:, half:]], axis=-1) + bg_ref[1:2, :]
        r = jax.nn.sigmoid(ra)
        i = jax.nn.sigmoid(rx)
        log_a = RG_C * r * log_sig_l
        a = jnp.exp(log_a)
        th = jnp.tanh(log_a)
        om = -2.0 * th / (1.0 - th)
        mult = jnp.where(om > 0.0, om * lax.rsqrt(om), 0.0)
        u = mult * i * xc
        for s in range(n_slab):
            a_s[s, b * pitch:b * pitch + tc, :] = a[:, s * LANES:(s + 1) * LANES]
            u_s[s, b * pitch:b * pitch + tc, :] = u[:, s * LANES:(s + 1) * LANES]

    def rows(t):
        return pl.ds(t, nb, stride=pitch) if nb > 1 else pl.ds(t, 1)

    def scan_body(t, hs):
        out = []
        for s in range(n_slab):
            hn = a_s[s, rows(t), :] * hs[s] + u_s[s, rows(t), :]
            u_s[s, rows(t), :] = hn
            out.append(hn)
        return tuple(out)

    hs = tuple(h_s[:, s * LANES:(s + 1) * LANES] for s in range(n_slab))
    hs = lax.fori_loop(0, tc, scan_body, hs, unroll=8)
    h_s[...] = jnp.concatenate(hs, axis=-1)

    hn_ref[...] = jnp.concatenate(
        [u_s[s, rows(state_t), :] for s in range(n_slab)], axis=-1)
    for b in range(nb):
        tailn_ref[b] = ext_s[b, state_t + 1:state_t + 1 + SUBLANES, :]
        hb = jnp.concatenate(
            [u_s[s, b * pitch:b * pitch + tc, :] for s in range(n_slab)], axis=-1)
        rec = hb * _gelu_tanh(yr_ref[b])
        out_ref[b] = _rms(rec, g_ref[...]).astype(BF16)


def _rec(xy, cw, cb, wg, bg, lru, g, h0, tail0, *, nb, seq, tc, state_t):
    xy3 = xy.reshape(nb, seq, 2 * REC_W)
    pitch = tc + SUBLANES
    n_slab = REC_W // LANES
    out, hn, tailn = pl.pallas_call(
        functools.partial(_rec_kernel, nb=nb, tc=tc, pitch=pitch, state_t=state_t),
        grid=(seq // tc,),
        in_specs=[
            pl.BlockSpec((nb, tc, REC_W), lambda t: (0, t, 0)),
            pl.BlockSpec((nb, tc, REC_W), lambda t: (0, t, 1)),
            _const_spec(cw.shape),
            _const_spec(cb.shape),
            _const_spec(wg.shape),
            _const_spec(bg.shape),
            _const_spec(lru.shape),
            _const_spec(g.shape),
            _const_spec(h0.shape),
            _const_spec(tail0.shape),
        ],
        out_specs=[
            pl.BlockSpec((nb, tc, REC_W), lambda t: (0, t, 0)),
            pl.BlockSpec((nb, REC_W), lambda t: (0, 0)),
            pl.BlockSpec((nb, SUBLANES, REC_W), lambda t: (0, 0, 0)),
        ],
        out_shape=[
            jax.ShapeDtypeStruct((nb, seq, REC_W), BF16),
            jax.ShapeDtypeStruct((nb, REC_W), F32),
            jax.ShapeDtypeStruct((nb, SUBLANES, REC_W), F32),
        ],
        scratch_shapes=[
            pltpu.VMEM((nb, tc + SUBLANES, REC_W), F32),
            pltpu.VMEM((n_slab, nb * pitch, LANES), F32),
            pltpu.VMEM((n_slab, nb * pitch, LANES), F32),
            pltpu.VMEM((nb, REC_W), F32),
        ],
        compiler_params=pltpu.CompilerParams(
            dimension_semantics=("arbitrary",), vmem_limit_bytes=VMEM_LIMIT),
        name="rec",
    )(xy3, xy3, cw, cb, wg, bg, lru, g, h0, tail0)
    return out.reshape(nb * seq, REC_W), hn, tailn


def _post_kernel(h_ref, an_ref, rn_ref, wo_ref, g_ref, wu_ref, wd_ref, fg_ref, o_ref,
                 wo_s, wu_s, wd_s, *, final):
    i = pl.program_id(0)
    rows_o = wo_ref.shape[0]
    rows_d = wd_ref.shape[0]

    @pl.when(i < W_CHUNKS)
    def _():
        wo_s[pl.ds(pl.multiple_of(i * rows_o, rows_o), rows_o), :] = wo_ref[...].astype(BF16)
        wu_s[i] = wu_ref[...].astype(BF16)
        wd_s[pl.ds(pl.multiple_of(i * rows_d, rows_d), rows_d), :] = wd_ref[...].astype(BF16)

    @pl.when(i >= W_CHUNKS)
    def _():
        h1 = (h_ref[...] + _dot(an_ref[...], wo_s[0:ATT_W, :])
              + _dot(rn_ref[...], wo_s[ATT_W:ATT_W + REC_W, :]))
        z = _rms(h1, g_ref[...]).astype(BF16)
        down = None
        for c in range(W_CHUNKS):
            u = jnp.maximum(_dot(z, wu_s[c]), 0.0)
            part = _dot((u * u).astype(BF16), wd_s[c * rows_d:(c + 1) * rows_d, :])
            down = part if down is None else down + part
        acc = h1 + down
        if final:
            acc = _rms(acc, fg_ref[...])
        o_ref[...] = acc


def _post(h, an, rn, w_out, w_up, w_down, layer, g, fg, *, tm, final):
    n, d = h.shape
    d_mix, d_ff = w_out.shape[1], w_up.shape[2]
    rows_o, cols_u, rows_d = d_mix // W_CHUNKS, d_ff // W_CHUNKS, d_ff // W_CHUNKS

    def row(i):
        return (jnp.maximum(i - W_CHUNKS, 0), 0)

    def chunk(i):
        return jnp.minimum(i, W_CHUNKS - 1)

    return pl.pallas_call(
        functools.partial(_post_kernel, final=final),
        grid=(W_CHUNKS + n // tm,),
        in_specs=[
            pl.BlockSpec((tm, d), row),
            pl.BlockSpec((tm, ATT_W), row),
            pl.BlockSpec((tm, REC_W), row),
            pl.BlockSpec((None, rows_o, d), lambda i: (layer, chunk(i), 0)),
            _const_spec(g.shape),
            pl.BlockSpec((None, d, cols_u), lambda i: (layer, 0, chunk(i))),
            pl.BlockSpec((None, rows_d, d), lambda i: (layer, chunk(i), 0)),
            _const_spec(fg.shape),
        ],
        out_specs=pl.BlockSpec((tm, d), row),
        out_shape=jax.ShapeDtypeStruct((n, d), F32),
        scratch_shapes=[
            pltpu.VMEM((d_mix, d), BF16),
            pltpu.VMEM((W_CHUNKS, d, cols_u), BF16),
            pltpu.VMEM((d_ff, d), BF16),
        ],
        compiler_params=pltpu.CompilerParams(
            dimension_semantics=("arbitrary",), vmem_limit_bytes=VMEM_LIMIT),
        name="post",
    )(h, an, rn, w_out, g, w_up, w_down, fg)


def _block_diag_gates(w_a, w_x):
    nblk, blk, _ = w_a.shape
    per = nblk // 2
    eye = jnp.eye(per, dtype=w_a.dtype)

    def bd(w):
        w4 = w.reshape(2, per, blk, blk)
        full = jnp.einsum('gpij,pq->gpiqj', w4, eye)
        return full.reshape(2, per * blk, per * blk)

    return jnp.concatenate([bd(w_a), bd(w_x)], axis=-1).astype(BF16)


def kernel(x, meta, attn_norm_g, w_in, b_f, conv_w, conv_b, w_gate_a, b_gate_a,
           w_gate_x, b_gate_x, lru_L, attn_out_g, rec_out_g, w_out, mlp_norm_g,
           w_up, w_down, final_g):
    nb, seq, d = x.shape
    depth = w_in.shape[0]
    scale = 1.0 / math.sqrt(HEAD_DIM)
    c_q, c_f, c_x = ATT_W, 3 * ATT_W, 3 * ATT_W + N_HEADS

    h = x.reshape(nb * seq, d)
    hm = jnp.concatenate([meta, jnp.zeros((META_PAD - N_META, d), F32)], axis=0)
    fg = final_g.reshape(1, d)

    for l in range(depth):
        wl = w_in[l]
        wk = wl[:, c_q:2 * c_q].astype(BF16)
        wqvt = jnp.concatenate(
            [wl[:, :c_q] * (scale * LOG2E), wl[:, 2 * c_q:c_f]], axis=1).T.astype(BF16)
        wf = jnp.concatenate(
            [wl[:, c_f:c_x], jnp.zeros((d, LANES - N_HEADS), F32)], axis=1).astype(BF16)
        bf = jnp.concatenate([b_f[l], jnp.zeros((LANES - N_HEADS,), F32)]).reshape(1, LANES)
        wxy = wl[:, c_x:].astype(BF16)
        g_in = attn_norm_g[l].reshape(1, d)
        cw, cb = conv_w[l], conv_b[l].reshape(1, REC_W)
        wg = _block_diag_gates(w_gate_a[l], w_gate_x[l])
        bg = jnp.stack([b_gate_a[l], b_gate_x[l]])
        lru = lru_L[l].reshape(1, REC_W)
        g_att = attn_out_g[l].reshape(1, ATT_W)
        g_rec = rec_out_g[l].reshape(1, REC_W)
        g_mlp = mlp_norm_g[l].reshape(1, d)
        last = l == depth - 1

        k_m, qvt_m, logf_m, xy_m = _inproj(hm, g_in, wk, wqvt, wf, bf, wxy,
                                           tm=META_PAD, tile=META_PAD)
        an_m, bias_m = _meta_attn(k_m, qvt_m, logf_m, g_att)
        rn_m, h_state, x_tail = _rec(
            xy_m, cw, cb, wg, bg, lru, g_rec,
            jnp.zeros((1, REC_W), F32), jnp.zeros((1, SUBLANES, REC_W), F32),
            nb=1, seq=META_PAD, tc=META_PAD, state_t=N_META - 1)
        if not last:
            hm = _post(hm, an_m, rn_m, w_out, w_up, w_down, l, g_mlp, fg,
                       tm=META_PAD, final=False)

        k, qvt, logf, xy = _inproj(h, g_in, wk, wqvt, wf, bf, wxy, tm=ROW_TILE, tile=ATT_TILE)
        an = _attn(k, qvt, logf, k_m, qvt_m, bias_m, g_att, seq=seq, tq=ATT_TILE)
        rn, _, _ = _rec(xy, cw, cb, wg, bg, lru, g_rec, h_state, x_tail,
                        nb=nb, seq=seq, tc=REC_CHUNK, state_t=REC_CHUNK - 1)
        h = _post(h, an, rn, w_out, w_up, w_down, l, g_mlp, fg, tm=ROW_TILE, final=last)

    return h.reshape(nb, seq, d)
```

```python
import functools
import math

import jax
import jax.numpy as jnp
from jax import lax
from jax.experimental import pallas as pl
from jax.experimental.pallas import tpu as pltpu

F32 = jnp.float32
BF16 = jnp.bfloat16

N_META = 16
META_PAD = 128
N_HEADS = 8
HEAD_DIM = 64
ATT_W = N_HEADS * HEAD_DIM
REC_W = 512
CONV_WIDTH = 4
RG_C = 8.0
NORM_EPS = 1e-6
LANES = 128
SUBLANES = 8
LOG2E = math.log2(math.e)
ATT_TILE = 512
ROW_TILE = 512
REC_CHUNK = 128
W_CHUNKS = 8
IN_CHUNKS = 4
SUM_ROWS = 16
ACC_ROWS = HEAD_DIM + SUM_ROWS
VMEM_LIMIT = 56 * 1024 * 1024


def _rms(x, g):
    ms = jnp.mean(x * x, axis=-1, keepdims=True)
    return x * lax.rsqrt(ms + NORM_EPS) * g


def _log_sigmoid(x):
    return jnp.minimum(x, 0.0) - jnp.log1p(jnp.exp(-jnp.abs(x)))


def _gelu_tanh(x):
    c = math.sqrt(2.0 / math.pi)
    return x * (0.5 * (1.0 + jnp.tanh(c * (x + 0.044715 * (x * x * x)))))


def _dot(a, b):
    return jnp.dot(a, b, preferred_element_type=F32)


def _dot_nt(a, b):
    return lax.dot_general(a, b, (((1,), (1,)), ((), ())), preferred_element_type=F32)


def _whole(arr):
    nd = arr.ndim
    return pl.BlockSpec(arr.shape, lambda *_: (0,) * nd, pipeline_mode=pl.Buffered(1))


def _of_layer(arr, layer):
    nd = arr.ndim - 1
    return pl.BlockSpec((None,) + arr.shape[1:], lambda *_: (layer,) + (0,) * nd,
                        pipeline_mode=pl.Buffered(1))


def _inproj_kernel(h_ref, hm_ref, g_ref, wq_ref, wk_ref, wv_ref, wf_ref, wxy_ref, bf_ref,
                   k_ref, qvt_ref, logf_ref, xy_ref, km_ref, qvtm_ref, logfm_ref, xym_ref,
                   wk_s, wqvt_s, wf_s, wxy_s, *, n_tiles):
    i = pl.program_id(0)

    @pl.when(i < IN_CHUNKS)
    def _():
        cols = wq_ref.shape[1]
        rows = wk_ref.shape[0]
        c0 = pl.multiple_of(i * cols, cols)
        r0 = pl.multiple_of(i * rows, rows)
        wq = wq_ref[...] * (LOG2E / math.sqrt(HEAD_DIM))
        wqvt_s[pl.ds(c0, cols), :] = wq.T.astype(BF16)
        wqvt_s[pl.ds(ATT_W + c0, cols), :] = wv_ref[...].T.astype(BF16)
        wk_s[pl.ds(r0, rows), :] = wk_ref[...].astype(BF16)
        wf_s[pl.ds(r0, rows), :] = wf_ref[...].astype(BF16)
        wxy_s[pl.ds(r0, rows), :] = wxy_ref[...].astype(BF16)

    def project(h_tile, k_out, qvt_out, logf_out, xy_out):
        z = _rms(h_tile, g_ref[...]).astype(BF16)
        k_out[...] = _dot(z, wk_s[...]).astype(BF16)
        qvt = _dot_nt(wqvt_s[...], z).astype(BF16)
        tile = qvt_out.shape[2]
        for s in range(qvt_out.shape[0]):
            qvt_out[s] = qvt[:, s * tile:(s + 1) * tile]
        xy_out[...] = _dot(z, wxy_s[...])
        logf_out[...] = _log_sigmoid(_dot(z, wf_s[...]) + bf_ref[...])

    @pl.when(jnp.logical_and(i >= IN_CHUNKS, i < IN_CHUNKS + n_tiles))
    def _():
        project(h_ref[...], k_ref, qvt_ref, logf_ref, xy_ref)

    @pl.when(i == IN_CHUNKS + n_tiles)
    def _():
        project(hm_ref[...], km_ref, qvtm_ref, logfm_ref, xym_ref)


def _inproj(h, hm, g, w_in, wf, wxy, bf, layer, *, tm, tile):
    n, d = h.shape
    tmeta = hm.shape[0]
    n_tiles = n // tm
    per = tm // tile
    cols = ATT_W // IN_CHUNKS
    rows = d // IN_CHUNKS
    q_blk, k_blk, v_blk = 0, 1, 2 * (ATT_W // cols)

    def chunk(i):
        return jnp.minimum(i, IN_CHUNKS - 1)

    def row(i):
        return jnp.clip(i - IN_CHUNKS, 0, n_tiles - 1)

    return pl.pallas_call(
        functools.partial(_inproj_kernel, n_tiles=n_tiles),
        grid=(IN_CHUNKS + n_tiles + 1,),
        in_specs=[
            pl.BlockSpec((tm, d), lambda i: (row(i), 0)),
            _whole(hm),
            _of_layer(g, layer),
            pl.BlockSpec((None, d, cols), lambda i: (layer, 0, q_blk + chunk(i))),
            pl.BlockSpec((None, rows, ATT_W), lambda i: (layer, chunk(i), k_blk)),
            pl.BlockSpec((None, d, cols), lambda i: (layer, 0, v_blk + chunk(i))),
            pl.BlockSpec((None, rows, LANES), lambda i: (layer, chunk(i), 0)),
            pl.BlockSpec((None, rows, 2 * REC_W), lambda i: (layer, chunk(i), 0)),
            _of_layer(bf, layer),
        ],
        out_specs=[
            pl.BlockSpec((tm, ATT_W), lambda i: (row(i), 0)),
            pl.BlockSpec((per, 2 * ATT_W, tile), lambda i: (row(i), 0, 0)),
            pl.BlockSpec((tm, LANES), lambda i: (row(i), 0)),
            pl.BlockSpec((tm, 2 * REC_W), lambda i: (row(i), 0)),
            pl.BlockSpec((tmeta, ATT_W), lambda i: (0, 0)),
            pl.BlockSpec((1, 2 * ATT_W, tmeta), lambda i: (0, 0, 0)),
            pl.BlockSpec((tmeta, LANES), lambda i: (0, 0)),
            pl.BlockSpec((tmeta, 2 * REC_W), lambda i: (0, 0)),
        ],
        out_shape=[
            jax.ShapeDtypeStruct((n, ATT_W), BF16),
            jax.ShapeDtypeStruct((n // tile, 2 * ATT_W, tile), BF16),
            jax.ShapeDtypeStruct((n, LANES), F32),
            jax.ShapeDtypeStruct((n, 2 * REC_W), F32),
            jax.ShapeDtypeStruct((tmeta, ATT_W), BF16),
            jax.ShapeDtypeStruct((1, 2 * ATT_W, tmeta), BF16),
            jax.ShapeDtypeStruct((tmeta, LANES), F32),
            jax.ShapeDtypeStruct((tmeta, 2 * REC_W), F32),
        ],
        scratch_shapes=[
            pltpu.VMEM((d, ATT_W), BF16),
            pltpu.VMEM((2 * ATT_W, d), BF16),
            pltpu.VMEM((d, LANES), BF16),
            pltpu.VMEM((d, 2 * REC_W), BF16),
        ],
        compiler_params=pltpu.CompilerParams(
            dimension_semantics=("arbitrary",), vmem_limit_bytes=VMEM_LIMIT),
        name="inproj",
    )(h, hm, g, w_in, w_in, w_in, wf, wxy, bf)


def _neg_cumsum_cols(lf_ref, out_ref):
    r = lax.broadcasted_iota(jnp.int32, (LANES, LANES), 0)
    c = lax.broadcasted_iota(jnp.int32, (LANES, LANES), 1)
    lower = (r >= c).astype(BF16)
    carry = jnp.zeros((1, LANES), F32)
    for blk in range(lf_ref.shape[0] // LANES):
        x = lf_ref[blk * LANES:(blk + 1) * LANES, :]
        hi = x.astype(BF16)
        r1 = x - hi.astype(F32)
        mid = r1.astype(BF16)
        lo = (r1 - mid.astype(F32)).astype(BF16)
        cs = _dot(lower, hi) + _dot(lower, mid) + _dot(lower, lo) + carry
        out_ref[blk * LANES:(blk + 1) * LANES, :] = cs * (-LOG2E)
        carry = cs[LANES - 1:LANES, :]


def _scores_t(kh, qth, bias_col, mask):
    st = _dot(kh, qth) + bias_col
    if mask is not None:
        st = jnp.where(mask, st, -jnp.inf)
    return st


def _pv_and_sum(vth, p):
    ones = jnp.ones((SUM_ROWS, vth.shape[1]), BF16)
    return _dot(jnp.concatenate([vth, ones], axis=0), p)


def _softmax_first(st, vth):
    m = jnp.max(st, axis=0, keepdims=True)
    return m, _pv_and_sum(vth, jnp.exp2(st - m).astype(BF16))


def _softmax_next(st, vth, m, acc):
    m_new = jnp.maximum(m, jnp.max(st, axis=0, keepdims=True))
    alpha = jnp.exp2(m - m_new)
    return m_new, alpha * acc + _pv_and_sum(vth, jnp.exp2(st - m_new).astype(BF16))


def _normalized(acc):
    return acc[:HEAD_DIM, :] / acc[HEAD_DIM:HEAD_DIM + 1, :]


def _meta_attn_kernel(k_ref, qt_ref, vt_ref, lf_ref, g_ref, o_ref, bm_ref, b_s, ot_s):
    t = k_ref.shape[0]
    _neg_cumsum_cols(lf_ref, b_s)
    row = lax.broadcasted_iota(jnp.int32, (t, LANES), 0)
    bm_ref[...] = jnp.where(row < N_META, b_s[...] - b_s[N_META - 1:N_META, :], -jnp.inf)
    key = lax.broadcasted_iota(jnp.int32, (t, t), 0)
    qry = lax.broadcasted_iota(jnp.int32, (t, t), 1)
    causal = key <= qry
    for h in range(N_HEADS):
        sl = slice(h * HEAD_DIM, (h + 1) * HEAD_DIM)
        st = _scores_t(k_ref[:, sl], qt_ref[0, sl, :], b_s[:, h:h + 1], causal)
        _, acc = _softmax_first(st, vt_ref[0, sl, :])
        ot_s[sl, :] = _normalized(acc)
    o_ref[...] = _rms(ot_s[...].T, g_ref[...]).astype(BF16)


def _meta_attn(k, qvt, logf, g, layer):
    t = k.shape[0]
    return pl.pallas_call(
        _meta_attn_kernel,
        grid=(1,),
        in_specs=[
            pl.BlockSpec((t, ATT_W), lambda i: (0, 0)),
            pl.BlockSpec((1, ATT_W, t), lambda i: (0, 0, 0)),
            pl.BlockSpec((1, ATT_W, t), lambda i: (0, 1, 0)),
            pl.BlockSpec((t, LANES), lambda i: (0, 0)),
            _of_layer(g, layer),
        ],
        out_specs=[
            pl.BlockSpec((t, ATT_W), lambda i: (0, 0)),
            pl.BlockSpec((t, LANES), lambda i: (0, 0)),
        ],
        out_shape=[
            jax.ShapeDtypeStruct((t, ATT_W), BF16),
            jax.ShapeDtypeStruct((t, LANES), F32),
        ],
        scratch_shapes=[pltpu.VMEM((t, LANES), F32), pltpu.VMEM((ATT_W, t), F32)],
        name="meta_attn",
    )(k, qvt, qvt, logf, g)


def _attn_kernel(qt_ref, k_ref, vt_ref, lf_ref, km_ref, vtm_ref, bm_ref, g_ref,
                 o_ref, b_s, m_s, acc_s, ot_s, *, tq):
    j = pl.program_id(1)

    @pl.when(j == 0)
    def _():
        _neg_cumsum_cols(lf_ref, b_s)

    heads = [slice(h * HEAD_DIM, (h + 1) * HEAD_DIM) for h in range(N_HEADS)]
    accs = [slice(h * ACC_ROWS, (h + 1) * ACC_ROWS) for h in range(N_HEADS)]

    sts = [_scores_t(km_ref[0:N_META, sl], qt_ref[0, sl, :], bm_ref[0:N_META, h:h + 1], None)
           for h, sl in enumerate(heads)]
    for h, sl in enumerate(heads):
        m, acc = _softmax_first(sts[h], vtm_ref[0, sl, 0:N_META])
        m_s[h:h + 1, :] = m
        acc_s[accs[h], :] = acc

    def tile(i, mask, finish):
        off = pl.multiple_of(i * tq, tq)

        def scores(h):
            sl = heads[h]
            return _scores_t(k_ref[pl.ds(off, tq), sl], qt_ref[0, sl, :],
                             b_s[pl.ds(off, tq), h:h + 1], mask)

        sts = [scores(0)]
        for h, sl in enumerate(heads):
            if h + 1 < N_HEADS:
                sts.append(scores(h + 1))
            m, acc = _softmax_next(sts[h], vt_ref[i, sl, :], m_s[h:h + 1, :], acc_s[accs[h], :])
            if finish:
                ot_s[sl, :] = _normalized(acc)
            else:
                m_s[h:h + 1, :] = m
                acc_s[accs[h], :] = acc

    def body(i, carry):
        tile(i, None, False)
        return carry

    lax.fori_loop(0, j, body, 0)

    key = lax.broadcasted_iota(jnp.int32, (tq, tq), 0)
    qry = lax.broadcasted_iota(jnp.int32, (tq, tq), 1)
    tile(j, key <= qry, True)
    o_ref[...] = _rms(ot_s[...].T, g_ref[...]).astype(BF16)


def _attn(k, qvt, logf, k_meta, qvt_meta, bias_meta, g, layer, *, seq, tq):
    n = k.shape[0]
    nb = n // seq
    per_seq = seq // tq
    tmeta = k_meta.shape[0]
    return pl.pallas_call(
        functools.partial(_attn_kernel, tq=tq),
        grid=(nb, per_seq),
        in_specs=[
            pl.BlockSpec((1, ATT_W, tq), lambda b, j: (b * per_seq + j, 0, 0)),
            pl.BlockSpec((seq, ATT_W), lambda b, j: (b, 0)),
            pl.BlockSpec((per_seq, ATT_W, tq), lambda b, j: (b, 1, 0)),
            pl.BlockSpec((seq, LANES), lambda b, j: (b, 0)),
            pl.BlockSpec((tmeta, ATT_W), lambda b, j: (0, 0)),
            pl.BlockSpec((1, ATT_W, tmeta), lambda b, j: (0, 1, 0)),
            _whole(bias_meta),
            _of_layer(g, layer),
        ],
        out_specs=pl.BlockSpec((tq, ATT_W), lambda b, j: (b * per_seq + j, 0)),
        out_shape=jax.ShapeDtypeStruct((n, ATT_W), BF16),
        scratch_shapes=[
            pltpu.VMEM((seq, LANES), F32),
            pltpu.VMEM((N_HEADS, tq), F32),
            pltpu.VMEM((N_HEADS * ACC_ROWS, tq), F32),
            pltpu.VMEM((ATT_W, tq), F32),
        ],
        compiler_params=pltpu.CompilerParams(
            dimension_semantics=("arbitrary", "arbitrary"), vmem_limit_bytes=VMEM_LIMIT),
        name="attn",
    )(qvt, k, qvt, logf, k_meta, qvt_meta, bias_meta, g)


def _rec_kernel(xr_ref, yr_ref, cw_ref, cb_ref, wg_ref, bg_ref, lru_ref, g_ref,
                h0_ref, tail0_ref, out_ref, hn_ref, tailn_ref,
                ext_s, a_s, u_s, h_s, *, nb, tc, pitch, state_t):
    step = pl.program_id(0)
    n_slab = REC_W // LANES
    half = REC_W // 2

    @pl.when(step == 0)
    def _():
        for b in range(nb):
            ext_s[b, tc:tc + SUBLANES, :] = tail0_ref[0]
        h_s[...] = jnp.broadcast_to(h0_ref[...], h_s.shape)

    log_sig_l = _log_sigmoid(lru_ref[...])

    for b in range(nb):
        ext_s[b, 0:SUBLANES, :] = ext_s[b, tc:tc + SUBLANES, :]
        ext_s[b, SUBLANES:SUBLANES + tc, :] = xr_ref[b]
        ext = ext_s[b]
        run = cw_ref[0:1, :] * ext
        for k in range(1, CONV_WIDTH):
            run = pltpu.roll(run, 1, 0) + cw_ref[k:k + 1, :] * ext
        xc = run[SUBLANES:, :] + cb_ref[...]
        xcb = xc.astype(BF16)
        g0 = _dot(xcb[:, :half], wg_ref[0])
        g1 = _dot(xcb[:, half:], wg_ref[1])
        ra = jnp.concatenate([g0[:, :half], g1[:, :half]], axis=-1) + bg_ref[0:1, :]
        rx = jnp.concatenate([g0[:, half:], g1[:, half:]], axis=-1) + bg_ref[1:2, :]
        r = jax.nn.sigmoid(ra)
        i = jax.nn.sigmoid(rx)
        log_a = RG_C * r * log_sig_l
        a = jnp.exp(log_a)
        th = jnp.tanh(log_a)
        om = -2.0 * th / (1.0 - th)
        mult = jnp.where(om > 0.0, om * lax.rsqrt(om), 0.0)
        u = mult * i * xc
        for s in range(n_slab):
            a_s[s, b * pitch:b * pitch + tc, :] = a[:, s * LANES:(s + 1) * LANES]
            u_s[s, b * pitch:b * pitch + tc, :] = u[:, s * LANES:(s + 1) * LANES]

    def rows(t):
        return pl.ds(t, nb, stride=pitch) if nb > 1 else pl.ds(t, 1)

    def scan_body(t, hs):
        out = []
        for s in range(n_slab):
            hn = a_s[s, rows(t), :] * hs[s] + u_s[s, rows(t), :]
            u_s[s, rows(t), :] = hn
            out.append(hn)
        return tuple(out)

    hs = tuple(h_s[:, s * LANES:(s + 1) * LANES] for s in range(n_slab))
    hs = lax.fori_loop(0, tc, scan_body, hs, unroll=8)
    h_s[...] = jnp.concatenate(hs, axis=-1)

    hn_ref[...] = jnp.concatenate(
        [u_s[s, rows(state_t), :] for s in range(n_slab)], axis=-1)
    for b in range(nb):
        tailn_ref[b] = ext_s[b, state_t + 1:state_t + 1 + SUBLANES, :]
        hb = jnp.concatenate(
            [u_s[s, b * pitch:b * pitch + tc, :] for s in range(n_slab)], axis=-1)
        rec = hb * _gelu_tanh(yr_ref[b])
        out_ref[b] = _rms(rec, g_ref[...]).astype(BF16)


def _rec(xy, cw, cb, wg, bg, lru, g, layer, h0, tail0, *, nb, seq, tc, state_t):
    xy3 = xy.reshape(nb, seq, 2 * REC_W)
    pitch = tc + SUBLANES
    n_slab = REC_W // LANES
    out, hn, tailn = pl.pallas_call(
        functools.partial(_rec_kernel, nb=nb, tc=tc, pitch=pitch, state_t=state_t),
        grid=(seq // tc,),
        in_specs=[
            pl.BlockSpec((nb, tc, REC_W), lambda t: (0, t, 0)),
            pl.BlockSpec((nb, tc, REC_W), lambda t: (0, t, 1)),
            _of_layer(cw, layer),
            _of_layer(cb, layer),
            _of_layer(wg, layer),
            _of_layer(bg, layer),
            _of_layer(lru, layer),
            _of_layer(g, layer),
            _whole(h0),
            _whole(tail0),
        ],
        out_specs=[
            pl.BlockSpec((nb, tc, REC_W), lambda t: (0, t, 0)),
            pl.BlockSpec((nb, REC_W), lambda t: (0, 0)),
            pl.BlockSpec((nb, SUBLANES, REC_W), lambda t: (0, 0, 0)),
        ],
        out_shape=[
            jax.ShapeDtypeStruct((nb, seq, REC_W), BF16),
            jax.ShapeDtypeStruct((nb, REC_W), F32),
            jax.ShapeDtypeStruct((nb, SUBLANES, REC_W), F32),
        ],
        scratch_shapes=[
            pltpu.VMEM((nb, tc + SUBLANES, REC_W), F32),
            pltpu.VMEM((n_slab, nb * pitch, LANES), F32),
            pltpu.VMEM((n_slab, nb * pitch, LANES), F32),
            pltpu.VMEM((nb, REC_W), F32),
        ],
        compiler_params=pltpu.CompilerParams(
            dimension_semantics=("arbitrary",), vmem_limit_bytes=VMEM_LIMIT),
        name="rec",
    )(xy3, xy3, cw, cb, wg, bg, lru, g, h0, tail0)
    return out.reshape(nb * seq, REC_W), hn, tailn


def _post_kernel(h_ref, an_ref, rn_ref, hm_ref, anm_ref, rnm_ref,
                 wo_ref, g_ref, wu_ref, wd_ref, fg_ref, o_ref, *rest, n_tiles, final):
    om_ref = None if final else rest[0]
    wo_s, wu_s, wd_s = rest[-3:]
    i = pl.program_id(0)
    rows_o = wo_ref.shape[0]
    rows_d = wd_ref.shape[0]

    @pl.when(i < W_CHUNKS)
    def _():
        wo_s[pl.ds(pl.multiple_of(i * rows_o, rows_o), rows_o), :] = wo_ref[...].astype(BF16)
        wu_s[i] = wu_ref[...].astype(BF16)
        wd_s[pl.ds(pl.multiple_of(i * rows_d, rows_d), rows_d), :] = wd_ref[...].astype(BF16)

    def mix_and_mlp(h_tile, an, rn):
        h1 = h_tile + _dot(an, wo_s[0:ATT_W, :]) + _dot(rn, wo_s[ATT_W:ATT_W + REC_W, :])
        z = _rms(h1, g_ref[...]).astype(BF16)
        down = None
        for c in range(W_CHUNKS):
            u = jnp.maximum(_dot(z, wu_s[c]), 0.0)
            part = _dot((u * u).astype(BF16), wd_s[c * rows_d:(c + 1) * rows_d, :])
            down = part if down is None else down + part
        return h1 + down

    @pl.when(jnp.logical_and(i >= W_CHUNKS, i < W_CHUNKS + n_tiles))
    def _():
        acc = mix_and_mlp(h_ref[...], an_ref[...], rn_ref[...])
        if final:
            acc = _rms(acc, fg_ref[...])
        o_ref[...] = acc

    if not final:
        @pl.when(i == W_CHUNKS + n_tiles)
        def _():
            om_ref[...] = mix_and_mlp(hm_ref[...], anm_ref[...], rnm_ref[...])


def _post(h, an, rn, hm, an_m, rn_m, w_out, w_up, w_down, g, fg, layer, *, tm, final):
    n, d = h.shape
    n_tiles = n // tm
    d_mix, d_ff = w_out.shape[1], w_up.shape[2]
    rows_o, cols_u, rows_d = d_mix // W_CHUNKS, d_ff // W_CHUNKS, d_ff // W_CHUNKS

    def row(i):
        return (jnp.clip(i - W_CHUNKS, 0, n_tiles - 1), 0)

    def chunk(i):
        return jnp.minimum(i, W_CHUNKS - 1)

    out_specs = [pl.BlockSpec((tm, d), row)]
    out_shape = [jax.ShapeDtypeStruct((n, d), F32)]
    if not final:
        out_specs.append(pl.BlockSpec(hm.shape, lambda i: (0, 0)))
        out_shape.append(jax.ShapeDtypeStruct(hm.shape, F32))

    outs = pl.pallas_call(
        functools.partial(_post_kernel, n_tiles=n_tiles, final=final),
        grid=(W_CHUNKS + n_tiles + (0 if final else 1),),
        in_specs=[
            pl.BlockSpec((tm, d), row),
            pl.BlockSpec((tm, ATT_W), row),
            pl.BlockSpec((tm, REC_W), row),
            _whole(hm),
            _whole(an_m),
            _whole(rn_m),
            pl.BlockSpec((None, rows_o, d), lambda i: (layer, chunk(i), 0)),
            _of_layer(g, layer),
            pl.BlockSpec((None, d, cols_u), lambda i: (layer, 0, chunk(i))),
            pl.BlockSpec((None, rows_d, d), lambda i: (layer, chunk(i), 0)),
            _whole(fg),
        ],
        out_specs=out_specs,
        out_shape=out_shape,
        scratch_shapes=[
            pltpu.VMEM((d_mix, d), BF16),
            pltpu.VMEM((W_CHUNKS, d, cols_u), BF16),
            pltpu.VMEM((d_ff, d), BF16),
        ],
        compiler_params=pltpu.CompilerParams(
            dimension_semantics=("arbitrary",), vmem_limit_bytes=VMEM_LIMIT),
        name="post",
    )(h, an, rn, hm, an_m, rn_m, w_out, g, w_up, w_down, fg)
    return (outs[0], hm) if final else tuple(outs)


def _block_diag_gates(w_a, w_x):
    depth, nblk, blk, _ = w_a.shape
    per = nblk // 2
    eye = jnp.eye(per, dtype=w_a.dtype)

    def bd(w):
        w5 = w.reshape(depth, 2, per, blk, blk)
        full = jnp.einsum('lgpij,pq->lgpiqj', w5, eye)
        return full.reshape(depth, 2, per * blk, per * blk)

    return jnp.concatenate([bd(w_a), bd(w_x)], axis=-1).astype(BF16)


def kernel(x, meta, attn_norm_g, w_in, b_f, conv_w, conv_b, w_gate_a, b_gate_a,
           w_gate_x, b_gate_x, lru_L, attn_out_g, rec_out_g, w_out, mlp_norm_g,
           w_up, w_down, final_g):
    nb, seq, d = x.shape
    depth = w_in.shape[0]
    c_f, c_x = 3 * ATT_W, 3 * ATT_W + N_HEADS

    wf = jnp.pad(w_in[:, :, c_f:c_x], ((0, 0), (0, 0), (0, LANES - N_HEADS)))
    wxy = w_in[:, :, c_x:]
    bf = jnp.pad(b_f, ((0, 0), (0, LANES - N_HEADS))).reshape(depth, 1, LANES)
    g_in = attn_norm_g.reshape(depth, 1, d)
    cb = conv_b.reshape(depth, 1, REC_W)
    wg = _block_diag_gates(w_gate_a, w_gate_x)
    bg = jnp.stack([b_gate_a, b_gate_x], axis=1)
    lru = lru_L.reshape(depth, 1, REC_W)
    g_att = attn_out_g.reshape(depth, 1, ATT_W)
    g_rec = rec_out_g.reshape(depth, 1, REC_W)
    g_mlp = mlp_norm_g.reshape(depth, 1, d)
    fg = final_g.reshape(1, d)
    h0_meta = jnp.zeros((1, REC_W), F32)
    tail0_meta = jnp.zeros((1, SUBLANES, REC_W), F32)

    h = x.reshape(nb * seq, d)
    hm = jnp.pad(meta, ((0, META_PAD - N_META), (0, 0)))

    for l in range(depth):
        last = l == depth - 1
        k, qvt, logf, xy, k_m, qvt_m, logf_m, xy_m = _inproj(
            h, hm, g_in, w_in, wf, wxy, bf, l, tm=ROW_TILE, tile=ATT_TILE)

        an_m, bias_m = _meta_attn(k_m, qvt_m, logf_m, g_att, l)
        rn_m, h_state, x_tail = _rec(
            xy_m, conv_w, cb, wg, bg, lru, g_rec, l, h0_meta, tail0_meta,
            nb=1, seq=META_PAD, tc=META_PAD, state_t=N_META - 1)

        an = _attn(k, qvt, logf, k_m, qvt_m, bias_m, g_att, l, seq=seq, tq=ATT_TILE)
        rn, _, _ = _rec(xy, conv_w, cb, wg, bg, lru, g_rec, l, h_state, x_tail,
                        nb=nb, seq=seq, tc=REC_CHUNK, state_t=REC_CHUNK - 1)
        h, hm = _post(h, an, rn, hm, an_m, rn_m, w_out, w_up, w_down, g_mlp, fg, l,
                      tm=ROW_TILE, final=last)

    return h.reshape(nb, seq, d)
```

```python
import functools
import math

import jax
import jax.numpy as jnp
from jax import lax
from jax.experimental import pallas as pl
from jax.experimental.pallas import tpu as pltpu

F32 = jnp.float32
BF16 = jnp.bfloat16

N_META = 16
META_PAD = 128
N_HEADS = 8
HEAD_DIM = 64
ATT_W = N_HEADS * HEAD_DIM
REC_W = 512
CONV_WIDTH = 4
RG_C = 8.0
NORM_EPS = 1e-6
LANES = 128
SUBLANES = 8
LOG2E = math.log2(math.e)
ATT_TILE = 512
ROW_TILE = 512
REC_CHUNK = 128
W_CHUNKS = 8
IN_CHUNKS = 4
SUM_ROWS = 16
ACC_ROWS = HEAD_DIM + SUM_ROWS
VMEM_LIMIT = 56 * 1024 * 1024


def _rms(x, g):
    ms = jnp.mean(x * x, axis=-1, keepdims=True)
    return x * lax.rsqrt(ms + NORM_EPS) * g


def _log_sigmoid(x):
    return jnp.minimum(x, 0.0) - jnp.log1p(jnp.exp(-jnp.abs(x)))


def _gelu_tanh(x):
    c = math.sqrt(2.0 / math.pi)
    return x * (0.5 * (1.0 + jnp.tanh(c * (x + 0.044715 * (x * x * x)))))


def _dot(a, b):
    return jnp.dot(a, b, preferred_element_type=F32)


def _dot_nt(a, b):
    return lax.dot_general(a, b, (((1,), (1,)), ((), ())), preferred_element_type=F32)


def _whole(arr):
    nd = arr.ndim
    return pl.BlockSpec(arr.shape, lambda *_: (0,) * nd, pipeline_mode=pl.Buffered(1))


def _of_layer(arr, layer):
    nd = arr.ndim - 1
    return pl.BlockSpec((None,) + arr.shape[1:], lambda *_: (layer,) + (0,) * nd,
                        pipeline_mode=pl.Buffered(1))


def _inproj_kernel(h_ref, hm_ref, g_ref, wq_ref, wk_ref, wv_ref, wf_ref, wxy_ref, bf_ref,
                   cw_ref, cb_ref,
                   k_ref, qvt_ref, logf_ref, xy_ref, km_ref, qvtm_ref, logfm_ref, xym_ref,
                   wk_s, wqvt_s, wf_s, wxy_s, tail_s, mtail_s, *, tiles_per_seq):
    i = pl.program_id(0)

    @pl.when(i < IN_CHUNKS)
    def _():
        cols = wq_ref.shape[1]
        rows = wk_ref.shape[0]
        c0 = pl.multiple_of(i * cols, cols)
        r0 = pl.multiple_of(i * rows, rows)
        wq = wq_ref[...] * (LOG2E / math.sqrt(HEAD_DIM))
        wqvt_s[pl.ds(c0, cols), :] = wq.T.astype(BF16)
        wqvt_s[pl.ds(ATT_W + c0, cols), :] = wv_ref[...].T.astype(BF16)
        wk_s[pl.ds(r0, rows), :] = wk_ref[...].astype(BF16)
        wf_s[pl.ds(r0, rows), :] = wf_ref[...].astype(BF16)
        wxy_s[pl.ds(r0, rows), :] = wxy_ref[...].astype(BF16)

    def project(h_tile, tail, k_out, qvt_out, logf_out, xy_out):
        z = _rms(h_tile, g_ref[...]).astype(BF16)
        xy = _dot(z, wxy_s[...])
        k_out[...] = _dot(z, wk_s[...]).astype(BF16)
        qvt = _dot_nt(wqvt_s[...], z).astype(BF16)
        tile = qvt_out.shape[2]
        for s in range(qvt_out.shape[0]):
            qvt_out[s] = qvt[:, s * tile:(s + 1) * tile]
        logf_out[...] = _log_sigmoid(_dot(z, wf_s[...]) + bf_ref[...])
        xr = xy[:, :REC_W]
        ext = jnp.concatenate([tail, xr], axis=0)
        run = cw_ref[0:1, :] * ext
        for k in range(1, CONV_WIDTH):
            run = pltpu.roll(run, 1, 0) + cw_ref[k:k + 1, :] * ext
        xy_out[:, :REC_W] = run[SUBLANES:, :] + cb_ref[...]
        xy_out[:, REC_W:] = _gelu_tanh(xy[:, REC_W:])
        return xr

    @pl.when(i == IN_CHUNKS)
    def _():
        xr = project(hm_ref[...], jnp.zeros((SUBLANES, REC_W), F32),
                     km_ref, qvtm_ref, logfm_ref, xym_ref)
        mtail_s[...] = xr[N_META - SUBLANES:N_META, :]
        tail_s[...] = jnp.zeros_like(tail_s)

    @pl.when(i > IN_CHUNKS)
    def _():
        seq_start = lax.rem(i - (IN_CHUNKS + 1), tiles_per_seq) == 0
        tail = jnp.where(seq_start, mtail_s[...], tail_s[...])
        xr = project(h_ref[...], tail, k_ref, qvt_ref, logf_ref, xy_ref)
        tail_s[...] = xr[xr.shape[0] - SUBLANES:, :]


def _inproj(h, hm, g, wqkv, wf, wxy, bf, cw, cb, layer, *, seq, tm, tile):
    n, d = h.shape
    tmeta = hm.shape[0]
    n_tiles = n // tm
    per = tm // tile
    cols = ATT_W // IN_CHUNKS
    rows = d // IN_CHUNKS
    q_blk, k_blk, v_blk = 0, 1, 2 * (ATT_W // cols)

    def chunk(i):
        return jnp.minimum(i, IN_CHUNKS - 1)

    def row(i):
        return jnp.clip(i - (IN_CHUNKS + 1), 0, n_tiles - 1)

    return pl.pallas_call(
        functools.partial(_inproj_kernel, tiles_per_seq=seq // tm),
        grid=(IN_CHUNKS + 1 + n_tiles,),
        in_specs=[
            pl.BlockSpec((tm, d), lambda i: (row(i), 0)),
            _whole(hm),
            _of_layer(g, layer),
            pl.BlockSpec((None, d, cols), lambda i: (layer, 0, q_blk + chunk(i))),
            pl.BlockSpec((None, rows, ATT_W), lambda i: (layer, chunk(i), k_blk)),
            pl.BlockSpec((None, d, cols), lambda i: (layer, 0, v_blk + chunk(i))),
            pl.BlockSpec((None, rows, LANES), lambda i: (layer, chunk(i), 0)),
            pl.BlockSpec((None, rows, 2 * REC_W), lambda i: (layer, chunk(i), 0)),
            _of_layer(bf, layer),
            _of_layer(cw, layer),
            _of_layer(cb, layer),
        ],
        out_specs=[
            pl.BlockSpec((tm, ATT_W), lambda i: (row(i), 0)),
            pl.BlockSpec((per, 2 * ATT_W, tile), lambda i: (row(i), 0, 0)),
            pl.BlockSpec((tm, LANES), lambda i: (row(i), 0)),
            pl.BlockSpec((tm, 2 * REC_W), lambda i: (row(i), 0)),
            pl.BlockSpec((tmeta, ATT_W), lambda i: (0, 0)),
            pl.BlockSpec((1, 2 * ATT_W, tmeta), lambda i: (0, 0, 0)),
            pl.BlockSpec((tmeta, LANES), lambda i: (0, 0)),
            pl.BlockSpec((tmeta, 2 * REC_W), lambda i: (0, 0)),
        ],
        out_shape=[
            jax.ShapeDtypeStruct((n, ATT_W), BF16),
            jax.ShapeDtypeStruct((n // tile, 2 * ATT_W, tile), BF16),
            jax.ShapeDtypeStruct((n, LANES), F32),
            jax.ShapeDtypeStruct((n, 2 * REC_W), F32),
            jax.ShapeDtypeStruct((tmeta, ATT_W), BF16),
            jax.ShapeDtypeStruct((1, 2 * ATT_W, tmeta), BF16),
            jax.ShapeDtypeStruct((tmeta, LANES), F32),
            jax.ShapeDtypeStruct((tmeta, 2 * REC_W), F32),
        ],
        scratch_shapes=[
            pltpu.VMEM((d, ATT_W), BF16),
            pltpu.VMEM((2 * ATT_W, d), BF16),
            pltpu.VMEM((d, LANES), BF16),
            pltpu.VMEM((d, 2 * REC_W), BF16),
            pltpu.VMEM((SUBLANES, REC_W), F32),
            pltpu.VMEM((SUBLANES, REC_W), F32),
        ],
        compiler_params=pltpu.CompilerParams(
            dimension_semantics=("arbitrary",), vmem_limit_bytes=VMEM_LIMIT),
        name="inproj",
    )(h, hm, g, wqkv, wqkv, wqkv, wf, wxy, bf, cw, cb)


def _neg_cumsum_cols(lf_ref, out_ref):
    r = lax.broadcasted_iota(jnp.int32, (LANES, LANES), 0)
    c = lax.broadcasted_iota(jnp.int32, (LANES, LANES), 1)
    lower = (r >= c).astype(BF16)
    carry = jnp.zeros((1, LANES), F32)
    for blk in range(lf_ref.shape[0] // LANES):
        x = lf_ref[blk * LANES:(blk + 1) * LANES, :]
        hi = x.astype(BF16)
        r1 = x - hi.astype(F32)
        mid = r1.astype(BF16)
        lo = (r1 - mid.astype(F32)).astype(BF16)
        cs = _dot(lower, hi) + _dot(lower, mid) + _dot(lower, lo) + carry
        out_ref[blk * LANES:(blk + 1) * LANES, :] = cs * (-LOG2E)
        carry = cs[LANES - 1:LANES, :]


def _scores_t(kh, qth, bias_col, mask):
    st = _dot(kh, qth) + bias_col
    if mask is not None:
        st = jnp.where(mask, st, -jnp.inf)
    return st


def _pv_and_sum(vth, p):
    ones = jnp.ones((SUM_ROWS, vth.shape[1]), BF16)
    return _dot(jnp.concatenate([vth, ones], axis=0), p)


def _softmax_first(st, vth):
    m = jnp.max(st, axis=0, keepdims=True)
    return m, _pv_and_sum(vth, jnp.exp2(st - m).astype(BF16))


def _softmax_next(st, vth, m, acc):
    m_new = jnp.maximum(m, jnp.max(st, axis=0, keepdims=True))
    alpha = jnp.exp2(m - m_new)
    return m_new, alpha * acc + _pv_and_sum(vth, jnp.exp2(st - m_new).astype(BF16))


def _normalized(acc):
    return acc[:HEAD_DIM, :] / acc[HEAD_DIM:HEAD_DIM + 1, :]


def _meta_attn_kernel(k_ref, qt_ref, vt_ref, lf_ref, g_ref, o_ref, bm_ref, b_s, ot_s):
    t = k_ref.shape[0]
    _neg_cumsum_cols(lf_ref, b_s)
    row = lax.broadcasted_iota(jnp.int32, (t, LANES), 0)
    bm_ref[...] = jnp.where(row < N_META, b_s[...] - b_s[N_META - 1:N_META, :], -jnp.inf)
    key = lax.broadcasted_iota(jnp.int32, (t, t), 0)
    qry = lax.broadcasted_iota(jnp.int32, (t, t), 1)
    causal = key <= qry
    for h in range(N_HEADS):
        sl = slice(h * HEAD_DIM, (h + 1) * HEAD_DIM)
        st = _scores_t(k_ref[:, sl], qt_ref[0, sl, :], b_s[:, h:h + 1], causal)
        _, acc = _softmax_first(st, vt_ref[0, sl, :])
        ot_s[sl, :] = _normalized(acc)
    o_ref[...] = _rms(ot_s[...].T, g_ref[...]).astype(BF16)


def _meta_attn(k, qvt, logf, g, layer):
    t = k.shape[0]
    return pl.pallas_call(
        _meta_attn_kernel,
        grid=(1,),
        in_specs=[
            pl.BlockSpec((t, ATT_W), lambda i: (0, 0)),
            pl.BlockSpec((1, ATT_W, t), lambda i: (0, 0, 0)),
            pl.BlockSpec((1, ATT_W, t), lambda i: (0, 1, 0)),
            pl.BlockSpec((t, LANES), lambda i: (0, 0)),
            _of_layer(g, layer),
        ],
        out_specs=[
            pl.BlockSpec((t, ATT_W), lambda i: (0, 0)),
            pl.BlockSpec((t, LANES), lambda i: (0, 0)),
        ],
        out_shape=[
            jax.ShapeDtypeStruct((t, ATT_W), BF16),
            jax.ShapeDtypeStruct((t, LANES), F32),
        ],
        scratch_shapes=[pltpu.VMEM((t, LANES), F32), pltpu.VMEM((ATT_W, t), F32)],
        name="meta_attn",
    )(k, qvt, qvt, logf, g)


def _attn_kernel(qt_ref, k_ref, vt_ref, lf_ref, km_ref, vtm_ref, bm_ref, g_ref,
                 o_ref, b_s, m_s, acc_s, ot_s, *, tq):
    j = pl.program_id(1)

    @pl.when(j == 0)
    def _():
        _neg_cumsum_cols(lf_ref, b_s)

    heads = [slice(h * HEAD_DIM, (h + 1) * HEAD_DIM) for h in range(N_HEADS)]
    accs = [slice(h * ACC_ROWS, (h + 1) * ACC_ROWS) for h in range(N_HEADS)]

    sts = [_scores_t(km_ref[0:N_META, sl], qt_ref[0, sl, :], bm_ref[0:N_META, h:h + 1], None)
           for h, sl in enumerate(heads)]
    for h, sl in enumerate(heads):
        m, acc = _softmax_first(sts[h], vtm_ref[0, sl, 0:N_META])
        m_s[h:h + 1, :] = m
        acc_s[accs[h], :] = acc

    def tile(i, mask, finish):
        off = pl.multiple_of(i * tq, tq)

        def scores(h):
            sl = heads[h]
            return _scores_t(k_ref[pl.ds(off, tq), sl], qt_ref[0, sl, :],
                             b_s[pl.ds(off, tq), h:h + 1], mask)

        sts = [scores(0)]
        for h, sl in enumerate(heads):
            if h + 1 < N_HEADS:
                sts.append(scores(h + 1))
            m, acc = _softmax_next(sts[h], vt_ref[i, sl, :], m_s[h:h + 1, :], acc_s[accs[h], :])
            if finish:
                ot_s[sl, :] = _normalized(acc)
            else:
                m_s[h:h + 1, :] = m
                acc_s[accs[h], :] = acc

    def body(i, carry):
        tile(i, None, False)
        return carry

    lax.fori_loop(0, j, body, 0)

    key = lax.broadcasted_iota(jnp.int32, (tq, tq), 0)
    qry = lax.broadcasted_iota(jnp.int32, (tq, tq), 1)
    tile(j, key <= qry, True)
    o_ref[...] = _rms(ot_s[...].T, g_ref[...]).astype(BF16)


def _attn(k, qvt, logf, k_meta, qvt_meta, bias_meta, g, layer, *, seq, tq):
    n = k.shape[0]
    nb = n // seq
    per_seq = seq // tq
    tmeta = k_meta.shape[0]
    return pl.pallas_call(
        functools.partial(_attn_kernel, tq=tq),
        grid=(nb, per_seq),
        in_specs=[
            pl.BlockSpec((1, ATT_W, tq), lambda b, j: (b * per_seq + j, 0, 0)),
            pl.BlockSpec((seq, ATT_W), lambda b, j: (b, 0)),
            pl.BlockSpec((per_seq, ATT_W, tq), lambda b, j: (b, 1, 0)),
            pl.BlockSpec((seq, LANES), lambda b, j: (b, 0)),
            pl.BlockSpec((tmeta, ATT_W), lambda b, j: (0, 0)),
            pl.BlockSpec((1, ATT_W, tmeta), lambda b, j: (0, 1, 0)),
            _whole(bias_meta),
            _of_layer(g, layer),
        ],
        out_specs=pl.BlockSpec((tq, ATT_W), lambda b, j: (b * per_seq + j, 0)),
        out_shape=jax.ShapeDtypeStruct((n, ATT_W), BF16),
        scratch_shapes=[
            pltpu.VMEM((seq, LANES), F32),
            pltpu.VMEM((N_HEADS, tq), F32),
            pltpu.VMEM((N_HEADS * ACC_ROWS, tq), F32),
            pltpu.VMEM((ATT_W, tq), F32),
        ],
        compiler_params=pltpu.CompilerParams(
            dimension_semantics=("arbitrary", "arbitrary"), vmem_limit_bytes=VMEM_LIMIT),
        name="attn",
    )(qvt, k, qvt, logf, k_meta, qvt_meta, bias_meta, g)


def _rec_kernel(xc_ref, yg_ref, wg_ref, bg_ref, lru_ref, g_ref, h0_ref,
                out_ref, hn_ref, a_s, u_s, h_s, *, nb, tc, pitch, state_t):
    step = pl.program_id(0)
    n_slab = REC_W // LANES
    half = REC_W // 2

    @pl.when(step == 0)
    def _():
        h_s[...] = jnp.broadcast_to(h0_ref[...], h_s.shape)

    log_sig_l = _log_sigmoid(lru_ref[...])

    for b in range(nb):
        xc = xc_ref[b]
        xcb = xc.astype(BF16)
        g0 = _dot(xcb[:, :half], wg_ref[0])
        g1 = _dot(xcb[:, half:], wg_ref[1])
        ra = jnp.concatenate([g0[:, :half], g1[:, :half]], axis=-1) + bg_ref[0:1, :]
        rx = jnp.concatenate([g0[:, half:], g1[:, half:]], axis=-1) + bg_ref[1:2, :]
        r = jax.nn.sigmoid(ra)
        i = jax.nn.sigmoid(rx)
        log_a = RG_C * r * log_sig_l
        a = jnp.exp(log_a)
        th = jnp.tanh(log_a)
        om = -2.0 * th / (1.0 - th)
        mult = jnp.where(om > 0.0, om * lax.rsqrt(om), 0.0)
        u = mult * i * xc
        for s in range(n_slab):
            a_s[s, b * pitch:b * pitch + tc, :] = a[:, s * LANES:(s + 1) * LANES]
            u_s[s, b * pitch:b * pitch + tc, :] = u[:, s * LANES:(s + 1) * LANES]

    def rows(t):
        return pl.ds(t, nb, stride=pitch) if nb > 1 else pl.ds(t, 1)

    def scan_body(t, hs):
        out = []
        for s in range(n_slab):
            hn = a_s[s, rows(t), :] * hs[s] + u_s[s, rows(t), :]
            u_s[s, rows(t), :] = hn
            out.append(hn)
        return tuple(out)

    hs = tuple(h_s[:, s * LANES:(s + 1) * LANES] for s in range(n_slab))
    hs = lax.fori_loop(0, tc, scan_body, hs, unroll=8)
    h_s[...] = jnp.concatenate(hs, axis=-1)

    hn_ref[...] = jnp.concatenate(
        [u_s[s, rows(state_t), :] for s in range(n_slab)], axis=-1)
    for b in range(nb):
        hb = jnp.concatenate(
            [u_s[s, b * pitch:b * pitch + tc, :] for s in range(n_slab)], axis=-1)
        out_ref[b] = _rms(hb * yg_ref[b], g_ref[...]).astype(BF16)


def _rec(xy, wg, bg, lru, g, layer, h0, *, nb, seq, tc, state_t):
    xy3 = xy.reshape(nb, seq, 2 * REC_W)
    pitch = tc + SUBLANES
    n_slab = REC_W // LANES
    out, hn = pl.pallas_call(
        functools.partial(_rec_kernel, nb=nb, tc=tc, pitch=pitch, state_t=state_t),
        grid=(seq // tc,),
        in_specs=[
            pl.BlockSpec((nb, tc, REC_W), lambda t: (0, t, 0)),
            pl.BlockSpec((nb, tc, REC_W), lambda t: (0, t, 1)),
            _of_layer(wg, layer),
            _of_layer(bg, layer),
            _of_layer(lru, layer),
            _of_layer(g, layer),
            _whole(h0),
        ],
        out_specs=[
            pl.BlockSpec((nb, tc, REC_W), lambda t: (0, t, 0)),
            pl.BlockSpec((nb, REC_W), lambda t: (0, 0)),
        ],
        out_shape=[
            jax.ShapeDtypeStruct((nb, seq, REC_W), BF16),
            jax.ShapeDtypeStruct((nb, REC_W), F32),
        ],
        scratch_shapes=[
            pltpu.VMEM((n_slab, nb * pitch, LANES), F32),
            pltpu.VMEM((n_slab, nb * pitch, LANES), F32),
            pltpu.VMEM((nb, REC_W), F32),
        ],
        compiler_params=pltpu.CompilerParams(
            dimension_semantics=("arbitrary",), vmem_limit_bytes=VMEM_LIMIT),
        name="rec",
    )(xy3, xy3, wg, bg, lru, g, h0)
    return out.reshape(nb * seq, REC_W), hn


def _post_kernel(h_ref, an_ref, rn_ref, hm_ref, anm_ref, rnm_ref,
                 wo_ref, g_ref, wu_ref, wd_ref, fg_ref, o_ref, *rest, n_tiles, final):
    om_ref = None if final else rest[0]
    wo_s, wu_s, wd_s = rest[-3:]
    i = pl.program_id(0)
    rows_o = wo_ref.shape[0]
    rows_d = wd_ref.shape[0]

    @pl.when(i < W_CHUNKS)
    def _():
        wo_s[pl.ds(pl.multiple_of(i * rows_o, rows_o), rows_o), :] = wo_ref[...].astype(BF16)
        wu_s[i] = wu_ref[...].astype(BF16)
        wd_s[pl.ds(pl.multiple_of(i * rows_d, rows_d), rows_d), :] = wd_ref[...].astype(BF16)

    def mix_and_mlp(h_tile, an, rn):
        h1 = h_tile + _dot(an, wo_s[0:ATT_W, :]) + _dot(rn, wo_s[ATT_W:ATT_W + REC_W, :])
        z = _rms(h1, g_ref[...]).astype(BF16)
        down = None
        for c in range(W_CHUNKS):
            u = jnp.maximum(_dot(z, wu_s[c]), 0.0)
            part = _dot((u * u).astype(BF16), wd_s[c * rows_d:(c + 1) * rows_d, :])
            down = part if down is None else down + part
        return h1 + down

    @pl.when(jnp.logical_and(i >= W_CHUNKS, i < W_CHUNKS + n_tiles))
    def _():
        acc = mix_and_mlp(h_ref[...], an_ref[...], rn_ref[...])
        if final:
            acc = _rms(acc, fg_ref[...])
        o_ref[...] = acc

    if not final:
        @pl.when(i == W_CHUNKS + n_tiles)
        def _():
            om_ref[...] = mix_and_mlp(hm_ref[...], anm_ref[...], rnm_ref[...])


def _post(h, an, rn, hm, an_m, rn_m, w_out, w_up, w_down, g, fg, layer, *, tm, final):
    n, d = h.shape
    n_tiles = n // tm
    d_mix, d_ff = w_out.shape[1], w_up.shape[2]
    rows_o, cols_u, rows_d = d_mix // W_CHUNKS, d_ff // W_CHUNKS, d_ff // W_CHUNKS

    def row(i):
        return (jnp.clip(i - W_CHUNKS, 0, n_tiles - 1), 0)

    def chunk(i):
        return jnp.minimum(i, W_CHUNKS - 1)

    out_specs = [pl.BlockSpec((tm, d), row)]
    out_shape = [jax.ShapeDtypeStruct((n, d), F32)]
    if not final:
        out_specs.append(pl.BlockSpec(hm.shape, lambda i: (0, 0)))
        out_shape.append(jax.ShapeDtypeStruct(hm.shape, F32))

    outs = pl.pallas_call(
        functools.partial(_post_kernel, n_tiles=n_tiles, final=final),
        grid=(W_CHUNKS + n_tiles + (0 if final else 1),),
        in_specs=[
            pl.BlockSpec((tm, d), row),
            pl.BlockSpec((tm, ATT_W), row),
            pl.BlockSpec((tm, REC_W), row),
            _whole(hm),
            _whole(an_m),
            _whole(rn_m),
            pl.BlockSpec((None, rows_o, d), lambda i: (layer, chunk(i), 0)),
            _of_layer(g, layer),
            pl.BlockSpec((None, d, cols_u), lambda i: (layer, 0, chunk(i))),
            pl.BlockSpec((None, rows_d, d), lambda i: (layer, chunk(i), 0)),
            _whole(fg),
        ],
        out_specs=out_specs,
        out_shape=out_shape,
        scratch_shapes=[
            pltpu.VMEM((d_mix, d), BF16),
            pltpu.VMEM((W_CHUNKS, d, cols_u), BF16),
            pltpu.VMEM((d_ff, d), BF16),
        ],
        compiler_params=pltpu.CompilerParams(
            dimension_semantics=("arbitrary",), vmem_limit_bytes=VMEM_LIMIT),
        name="post",
    )(h, an, rn, hm, an_m, rn_m, w_out, g, w_up, w_down, fg)
    return (outs[0], hm) if final else tuple(outs)


def _block_diag_gates(w_a, w_x):
    depth, nblk, blk, _ = w_a.shape
    per = nblk // 2
    eye = jnp.eye(per, dtype=w_a.dtype)

    def bd(w):
        w5 = w.reshape(depth, 2, per, blk, blk)
        full = jnp.einsum('lgpij,pq->lgpiqj', w5, eye)
        return full.reshape(depth, 2, per * blk, per * blk)

    return jnp.concatenate([bd(w_a), bd(w_x)], axis=-1).astype(BF16)


def kernel(x, meta, attn_norm_g, w_in, b_f, conv_w, conv_b, w_gate_a, b_gate_a,
           w_gate_x, b_gate_x, lru_L, attn_out_g, rec_out_g, w_out, mlp_norm_g,
           w_up, w_down, final_g):
    nb, seq, d = x.shape
    depth = w_in.shape[0]
    c_f, c_x = 3 * ATT_W, 3 * ATT_W + N_HEADS

    wqkv = w_in[:, :, :c_f]
    wf = jnp.pad(w_in[:, :, c_f:c_x], ((0, 0), (0, 0), (0, LANES - N_HEADS)))
    wxy = w_in[:, :, c_x:]
    bf = jnp.pad(b_f, ((0, 0), (0, LANES - N_HEADS))).reshape(depth, 1, LANES)
    g_in = attn_norm_g.reshape(depth, 1, d)
    cb = conv_b.reshape(depth, 1, REC_W)
    wg = _block_diag_gates(w_gate_a, w_gate_x)
    bg = jnp.stack([b_gate_a, b_gate_x], axis=1)
    lru = lru_L.reshape(depth, 1, REC_W)
    g_att = attn_out_g.reshape(depth, 1, ATT_W)
    g_rec = rec_out_g.reshape(depth, 1, REC_W)
    g_mlp = mlp_norm_g.reshape(depth, 1, d)
    fg = final_g.reshape(1, d)
    h0_meta = jnp.zeros((1, REC_W), F32)

    h = x.reshape(nb * seq, d)
    hm = jnp.pad(meta, ((0, META_PAD - N_META), (0, 0)))

    for l in range(depth):
        last = l == depth - 1
        k, qvt, logf, xy, k_m, qvt_m, logf_m, xy_m = _inproj(
            h, hm, g_in, wqkv, wf, wxy, bf, conv_w, cb, l, seq=seq, tm=ROW_TILE, tile=ATT_TILE)

        an_m, bias_m = _meta_attn(k_m, qvt_m, logf_m, g_att, l)
        rn_m, h_state = _rec(xy_m, wg, bg, lru, g_rec, l, h0_meta,
                             nb=1, seq=META_PAD, tc=META_PAD, state_t=N_META - 1)

        an = _attn(k, qvt, logf, k_m, qvt_m, bias_m, g_att, l, seq=seq, tq=ATT_TILE)
        rn, _ = _rec(xy, wg, bg, lru, g_rec, l, h_state,
                     nb=nb, seq=seq, tc=REC_CHUNK, state_t=REC_CHUNK - 1)
        h, hm = _post(h, an, rn, hm, an_m, rn_m, w_out, w_up, w_down, g_mlp, fg, l,
                      tm=ROW_TILE, final=last)

    return h.reshape(nb, seq, d)
```

```python
import functools
import math

import jax
import jax.numpy as jnp
from jax import lax
from jax.experimental import pallas as pl
from jax.experimental.pallas import tpu as pltpu

F32 = jnp.float32
BF16 = jnp.bfloat16

N_META = 16
META_PAD = 128
N_HEADS = 8
HEAD_DIM = 64
ATT_W = N_HEADS * HEAD_DIM
REC_W = 512
CONV_WIDTH = 4
RG_C = 8.0
NORM_EPS = 1e-6
LANES = 128
SUBLANES = 8
LOG2E = math.log2(math.e)
ATT_TILE = 512
ROW_TILE = 512
TIME_TILE = 64
W_CHUNKS = 8
IN_CHUNKS = 4
SUM_ROWS = 16
ACC_ROWS = HEAD_DIM + SUM_ROWS
VMEM_LIMIT = 56 * 1024 * 1024


def _rms(x, g):
    ms = jnp.mean(x * x, axis=-1, keepdims=True)
    return x * lax.rsqrt(ms + NORM_EPS) * g


def _log_sigmoid(x):
    return jnp.minimum(x, 0.0) - jnp.log1p(jnp.exp(-jnp.abs(x)))


def _gelu_tanh(x):
    c = math.sqrt(2.0 / math.pi)
    return x * (0.5 * (1.0 + jnp.tanh(c * (x + 0.044715 * (x * x * x)))))


def _dot(a, b):
    return jnp.dot(a, b, preferred_element_type=F32)


def _dot_nt(a, b):
    return lax.dot_general(a, b, (((1,), (1,)), ((), ())), preferred_element_type=F32)


def _whole(arr):
    nd = arr.ndim
    return pl.BlockSpec(arr.shape, lambda *_: (0,) * nd, pipeline_mode=pl.Buffered(1))


def _of_layer(arr, layer):
    nd = arr.ndim - 1
    return pl.BlockSpec((None,) + arr.shape[1:], lambda *_: (layer,) + (0,) * nd,
                        pipeline_mode=pl.Buffered(1))


def _inproj_kernel(h_ref, hm_ref, g_ref, wq_ref, wk_ref, wv_ref, wf_ref, wxy_ref, bf_ref,
                   k_ref, qvt_ref, logf_ref, xy_ref, km_ref, qvtm_ref, logfm_ref, xym_ref,
                   wk_s, wqvt_s, wf_s, wxy_s, *, n_tiles):
    i = pl.program_id(0)

    @pl.when(i < IN_CHUNKS)
    def _():
        cols = wq_ref.shape[1]
        rows = wk_ref.shape[0]
        c0 = pl.multiple_of(i * cols, cols)
        r0 = pl.multiple_of(i * rows, rows)
        wq = wq_ref[...] * (LOG2E / math.sqrt(HEAD_DIM))
        wqvt_s[pl.ds(c0, cols), :] = wq.T.astype(BF16)
        wqvt_s[pl.ds(ATT_W + c0, cols), :] = wv_ref[...].T.astype(BF16)
        wk_s[pl.ds(r0, rows), :] = wk_ref[...].astype(BF16)
        wf_s[pl.ds(r0, rows), :] = wf_ref[...].astype(BF16)
        wxy_s[pl.ds(r0, rows), :] = wxy_ref[...].astype(BF16)

    def project(h_tile, k_out, qvt_out, logf_out, xy_out):
        z = _rms(h_tile, g_ref[...]).astype(BF16)
        k_out[...] = _dot(z, wk_s[...]).astype(BF16)
        qvt = _dot_nt(wqvt_s[...], z).astype(BF16)
        tile = qvt_out.shape[2]
        for s in range(qvt_out.shape[0]):
            qvt_out[s] = qvt[:, s * tile:(s + 1) * tile]
        xy_out[...] = _dot(z, wxy_s[...])
        logf_out[...] = _log_sigmoid(_dot(z, wf_s[...]) + bf_ref[...])

    @pl.when(jnp.logical_and(i >= IN_CHUNKS, i < IN_CHUNKS + n_tiles))
    def _():
        project(h_ref[...], k_ref, qvt_ref, logf_ref, xy_ref)

    @pl.when(i == IN_CHUNKS + n_tiles)
    def _():
        project(hm_ref[...], km_ref, qvtm_ref, logfm_ref, xym_ref)


def _inproj(h, hm, g, w_in, wf, wxy, bf, layer, *, tm, tile):
    n, d = h.shape
    tmeta = hm.shape[0]
    n_tiles = n // tm
    per = tm // tile
    cols = ATT_W // IN_CHUNKS
    rows = d // IN_CHUNKS
    q_blk, k_blk, v_blk = 0, 1, 2 * (ATT_W // cols)

    def chunk(i):
        return jnp.minimum(i, IN_CHUNKS - 1)

    def row(i):
        return jnp.clip(i - IN_CHUNKS, 0, n_tiles - 1)

    return pl.pallas_call(
        functools.partial(_inproj_kernel, n_tiles=n_tiles),
        grid=(IN_CHUNKS + n_tiles + 1,),
        in_specs=[
            pl.BlockSpec((tm, d), lambda i: (row(i), 0)),
            _whole(hm),
            _of_layer(g, layer),
            pl.BlockSpec((None, d, cols), lambda i: (layer, 0, q_blk + chunk(i))),
            pl.BlockSpec((None, rows, ATT_W), lambda i: (layer, chunk(i), k_blk)),
            pl.BlockSpec((None, d, cols), lambda i: (layer, 0, v_blk + chunk(i))),
            pl.BlockSpec((None, rows, LANES), lambda i: (layer, chunk(i), 0)),
            pl.BlockSpec((None, rows, 2 * REC_W), lambda i: (layer, chunk(i), 0)),
            _of_layer(bf, layer),
        ],
        out_specs=[
            pl.BlockSpec((tm, ATT_W), lambda i: (row(i), 0)),
            pl.BlockSpec((per, 2 * ATT_W, tile), lambda i: (row(i), 0, 0)),
            pl.BlockSpec((tm, LANES), lambda i: (row(i), 0)),
            pl.BlockSpec((tm, 2 * REC_W), lambda i: (row(i), 0)),
            pl.BlockSpec((tmeta, ATT_W), lambda i: (0, 0)),
            pl.BlockSpec((1, 2 * ATT_W, tmeta), lambda i: (0, 0, 0)),
            pl.BlockSpec((tmeta, LANES), lambda i: (0, 0)),
            pl.BlockSpec((tmeta, 2 * REC_W), lambda i: (0, 0)),
        ],
        out_shape=[
            jax.ShapeDtypeStruct((n, ATT_W), BF16),
            jax.ShapeDtypeStruct((n // tile, 2 * ATT_W, tile), BF16),
            jax.ShapeDtypeStruct((n, LANES), F32),
            jax.ShapeDtypeStruct((n, 2 * REC_W), F32),
            jax.ShapeDtypeStruct((tmeta, ATT_W), BF16),
            jax.ShapeDtypeStruct((1, 2 * ATT_W, tmeta), BF16),
            jax.ShapeDtypeStruct((tmeta, LANES), F32),
            jax.ShapeDtypeStruct((tmeta, 2 * REC_W), F32),
        ],
        scratch_shapes=[
            pltpu.VMEM((d, ATT_W), BF16),
            pltpu.VMEM((2 * ATT_W, d), BF16),
            pltpu.VMEM((d, LANES), BF16),
            pltpu.VMEM((d, 2 * REC_W), BF16),
        ],
        compiler_params=pltpu.CompilerParams(
            dimension_semantics=("arbitrary",), vmem_limit_bytes=VMEM_LIMIT),
        name="inproj",
    )(h, hm, g, w_in, w_in, w_in, wf, wxy, bf)


def _neg_cumsum_cols(lf_ref, out_ref):
    r = lax.broadcasted_iota(jnp.int32, (LANES, LANES), 0)
    c = lax.broadcasted_iota(jnp.int32, (LANES, LANES), 1)
    lower = (r >= c).astype(BF16)
    carry = jnp.zeros((1, LANES), F32)
    for blk in range(lf_ref.shape[0] // LANES):
        x = lf_ref[blk * LANES:(blk + 1) * LANES, :]
        hi = x.astype(BF16)
        r1 = x - hi.astype(F32)
        mid = r1.astype(BF16)
        lo = (r1 - mid.astype(F32)).astype(BF16)
        cs = _dot(lower, hi) + _dot(lower, mid) + _dot(lower, lo) + carry
        out_ref[blk * LANES:(blk + 1) * LANES, :] = cs * (-LOG2E)
        carry = cs[LANES - 1:LANES, :]


def _scores_t(kh, qth, bias_col, mask):
    st = _dot(kh, qth) + bias_col
    if mask is not None:
        st = jnp.where(mask, st, -jnp.inf)
    return st


def _pv_and_sum(vth, p):
    ones = jnp.ones((SUM_ROWS, vth.shape[1]), BF16)
    return _dot(jnp.concatenate([vth, ones], axis=0), p)


def _softmax_first(st, vth):
    m = jnp.max(st, axis=0, keepdims=True)
    return m, _pv_and_sum(vth, jnp.exp2(st - m).astype(BF16))


def _softmax_next(st, vth, m, acc):
    m_new = jnp.maximum(m, jnp.max(st, axis=0, keepdims=True))
    alpha = jnp.exp2(m - m_new)
    return m_new, alpha * acc + _pv_and_sum(vth, jnp.exp2(st - m_new).astype(BF16))


def _normalized(acc):
    return acc[:HEAD_DIM, :] / acc[HEAD_DIM:HEAD_DIM + 1, :]


def _meta_attn_kernel(k_ref, qt_ref, vt_ref, lf_ref, g_ref, o_ref, bm_ref, b_s, ot_s):
    t = k_ref.shape[0]
    _neg_cumsum_cols(lf_ref, b_s)
    row = lax.broadcasted_iota(jnp.int32, (t, LANES), 0)
    bm_ref[...] = jnp.where(row < N_META, b_s[...] - b_s[N_META - 1:N_META, :], -jnp.inf)
    key = lax.broadcasted_iota(jnp.int32, (t, t), 0)
    qry = lax.broadcasted_iota(jnp.int32, (t, t), 1)
    causal = key <= qry
    for h in range(N_HEADS):
        sl = slice(h * HEAD_DIM, (h + 1) * HEAD_DIM)
        st = _scores_t(k_ref[:, sl], qt_ref[0, sl, :], b_s[:, h:h + 1], causal)
        _, acc = _softmax_first(st, vt_ref[0, sl, :])
        ot_s[sl, :] = _normalized(acc)
    o_ref[...] = _rms(ot_s[...].T, g_ref[...]).astype(BF16)


def _meta_attn(k, qvt, logf, g, layer):
    t = k.shape[0]
    return pl.pallas_call(
        _meta_attn_kernel,
        grid=(1,),
        in_specs=[
            pl.BlockSpec((t, ATT_W), lambda i: (0, 0)),
            pl.BlockSpec((1, ATT_W, t), lambda i: (0, 0, 0)),
            pl.BlockSpec((1, ATT_W, t), lambda i: (0, 1, 0)),
            pl.BlockSpec((t, LANES), lambda i: (0, 0)),
            _of_layer(g, layer),
        ],
        out_specs=[
            pl.BlockSpec((t, ATT_W), lambda i: (0, 0)),
            pl.BlockSpec((t, LANES), lambda i: (0, 0)),
        ],
        out_shape=[
            jax.ShapeDtypeStruct((t, ATT_W), BF16),
            jax.ShapeDtypeStruct((t, LANES), F32),
        ],
        scratch_shapes=[pltpu.VMEM((t, LANES), F32), pltpu.VMEM((ATT_W, t), F32)],
        name="meta_attn",
    )(k, qvt, qvt, logf, g)


def _attn_kernel(qt_ref, k_ref, vt_ref, lf_ref, km_ref, vtm_ref, bm_ref, g_ref,
                 o_ref, b_s, m_s, acc_s, ot_s, *, tq):
    j = pl.program_id(1)

    @pl.when(j == 0)
    def _():
        _neg_cumsum_cols(lf_ref, b_s)

    heads = [slice(h * HEAD_DIM, (h + 1) * HEAD_DIM) for h in range(N_HEADS)]
    accs = [slice(h * ACC_ROWS, (h + 1) * ACC_ROWS) for h in range(N_HEADS)]

    sts = [_scores_t(km_ref[0:N_META, sl], qt_ref[0, sl, :], bm_ref[0:N_META, h:h + 1], None)
           for h, sl in enumerate(heads)]
    for h, sl in enumerate(heads):
        m, acc = _softmax_first(sts[h], vtm_ref[0, sl, 0:N_META])
        m_s[h:h + 1, :] = m
        acc_s[accs[h], :] = acc

    def tile(i, mask, finish):
        off = pl.multiple_of(i * tq, tq)

        def scores(h):
            sl = heads[h]
            return _scores_t(k_ref[pl.ds(off, tq), sl], qt_ref[0, sl, :],
                             b_s[pl.ds(off, tq), h:h + 1], mask)

        sts = [scores(0)]
        for h, sl in enumerate(heads):
            if h + 1 < N_HEADS:
                sts.append(scores(h + 1))
            m, acc = _softmax_next(sts[h], vt_ref[i, sl, :], m_s[h:h + 1, :], acc_s[accs[h], :])
            if finish:
                ot_s[sl, :] = _normalized(acc)
            else:
                m_s[h:h + 1, :] = m
                acc_s[accs[h], :] = acc

    def body(i, carry):
        tile(i, None, False)
        return carry

    lax.fori_loop(0, j, body, 0)

    key = lax.broadcasted_iota(jnp.int32, (tq, tq), 0)
    qry = lax.broadcasted_iota(jnp.int32, (tq, tq), 1)
    tile(j, key <= qry, True)
    o_ref[...] = _rms(ot_s[...].T, g_ref[...]).astype(BF16)


def _attn(k, qvt, logf, k_meta, qvt_meta, bias_meta, g, layer, *, seq, tq):
    n = k.shape[0]
    nb = n // seq
    per_seq = seq // tq
    tmeta = k_meta.shape[0]
    return pl.pallas_call(
        functools.partial(_attn_kernel, tq=tq),
        grid=(nb, per_seq),
        in_specs=[
            pl.BlockSpec((1, ATT_W, tq), lambda b, j: (b * per_seq + j, 0, 0)),
            pl.BlockSpec((seq, ATT_W), lambda b, j: (b, 0)),
            pl.BlockSpec((per_seq, ATT_W, tq), lambda b, j: (b, 1, 0)),
            pl.BlockSpec((seq, LANES), lambda b, j: (b, 0)),
            pl.BlockSpec((tmeta, ATT_W), lambda b, j: (0, 0)),
            pl.BlockSpec((1, ATT_W, tmeta), lambda b, j: (0, 1, 0)),
            _whole(bias_meta),
            _of_layer(g, layer),
        ],
        out_specs=pl.BlockSpec((tq, ATT_W), lambda b, j: (b * per_seq + j, 0)),
        out_shape=jax.ShapeDtypeStruct((n, ATT_W), BF16),
        scratch_shapes=[
            pltpu.VMEM((seq, LANES), F32),
            pltpu.VMEM((N_HEADS, tq), F32),
            pltpu.VMEM((N_HEADS * ACC_ROWS, tq), F32),
            pltpu.VMEM((ATT_W, tq), F32),
        ],
        compiler_params=pltpu.CompilerParams(
            dimension_semantics=("arbitrary", "arbitrary"), vmem_limit_bytes=VMEM_LIMIT),
        name="attn",
    )(qvt, k, qvt, logf, k_meta, qvt_meta, bias_meta, g)


def _rec_init(h0_ref, tail0_ref, ext_s, h_s, *, nb, tc):
    for b in range(nb):
        ext_s[b, tc:tc + SUBLANES, :] = tail0_ref[0]
    h_s[...] = jnp.broadcast_to(h0_ref[...], h_s.shape)


class _RecRefs:
    def __init__(self, cw_ref, cb_ref, wg_ref, bg_ref, lru_ref, g_ref, ext_s, a_s, u_s, h_s,
                 *, nb, tc):
        self.cw, self.cb, self.wg, self.bg, self.lru, self.g = (
            cw_ref, cb_ref, wg_ref, bg_ref, lru_ref, g_ref)
        self.ext_s, self.a_s, self.u_s, self.h_s = ext_s, a_s, u_s, h_s
        self.nb, self.tc = nb, tc
        self.pitch = tc + SUBLANES
        self.n_slab = REC_W // LANES


def _rec_gates(rr, b, xr_b):
    tc, half = rr.tc, REC_W // 2
    rr.ext_s[b, 0:SUBLANES, :] = rr.ext_s[b, tc:tc + SUBLANES, :]
    rr.ext_s[b, SUBLANES:SUBLANES + tc, :] = xr_b
    ext = rr.ext_s[b]
    run = rr.cw[0:1, :] * ext
    for k in range(1, CONV_WIDTH):
        run = pltpu.roll(run, 1, 0) + rr.cw[k:k + 1, :] * ext
    xc = run[SUBLANES:, :] + rr.cb[...]
    xcb = xc.astype(BF16)
    g0 = _dot(xcb[:, :half], rr.wg[0])
    g1 = _dot(xcb[:, half:], rr.wg[1])
    ra = jnp.concatenate([g0[:, :half], g1[:, :half]], axis=-1) + rr.bg[0:1, :]
    rx = jnp.concatenate([g0[:, half:], g1[:, half:]], axis=-1) + rr.bg[1:2, :]
    r = jax.nn.sigmoid(ra)
    i = jax.nn.sigmoid(rx)
    log_a = RG_C * r * _log_sigmoid(rr.lru[...])
    a = jnp.exp(log_a)
    th = jnp.tanh(log_a)
    om = -2.0 * th / (1.0 - th)
    mult = jnp.where(om > 0.0, om * lax.rsqrt(om), 0.0)
    u = mult * i * xc
    lo = b * rr.pitch
    for s in range(rr.n_slab):
        rr.a_s[s, lo:lo + tc, :] = a[:, s * LANES:(s + 1) * LANES]
        rr.u_s[s, lo:lo + tc, :] = u[:, s * LANES:(s + 1) * LANES]


def _rec_scan(rr, *, unroll):
    def rows(t):
        return pl.ds(t, rr.nb, stride=rr.pitch) if rr.nb > 1 else pl.ds(t, 1)

    def scan_body(t, hs):
        out = []
        for s in range(rr.n_slab):
            hn = rr.a_s[s, rows(t), :] * hs[s] + rr.u_s[s, rows(t), :]
            rr.u_s[s, rows(t), :] = hn
            out.append(hn)
        return tuple(out)

    hs = tuple(rr.h_s[:, s * LANES:(s + 1) * LANES] for s in range(rr.n_slab))
    if unroll:
        for t in range(rr.tc):
            hs = scan_body(t, hs)
    else:
        hs = lax.fori_loop(0, rr.tc, scan_body, hs, unroll=8)
    rr.h_s[...] = jnp.concatenate(hs, axis=-1)


def _rec_output(rr, b, yr_b):
    lo = b * rr.pitch
    hb = jnp.concatenate(
        [rr.u_s[s, lo:lo + rr.tc, :] for s in range(rr.n_slab)], axis=-1)
    return _rms(hb * _gelu_tanh(yr_b), rr.g[...]).astype(BF16)


def _rec_kernel(xr_ref, yr_ref, cw_ref, cb_ref, wg_ref, bg_ref, lru_ref, g_ref,
                h0_ref, tail0_ref, out_ref, hn_ref, tailn_ref,
                ext_s, a_s, u_s, h_s, *, nb, tc, pitch, state_t):
    @pl.when(pl.program_id(0) == 0)
    def _():
        _rec_init(h0_ref, tail0_ref, ext_s, h_s, nb=nb, tc=tc)

    rr = _RecRefs(cw_ref, cb_ref, wg_ref, bg_ref, lru_ref, g_ref, ext_s, a_s, u_s, h_s,
                  nb=nb, tc=tc)
    for b in range(nb):
        _rec_gates(rr, b, xr_ref[b])
    _rec_scan(rr, unroll=False)
    for b in range(nb):
        out_ref[b] = _rec_output(rr, b, yr_ref[b])

    n_slab = REC_W // LANES
    rows = pl.ds(state_t, nb, stride=pitch) if nb > 1 else pl.ds(state_t, 1)
    hn_ref[...] = jnp.concatenate([u_s[s, rows, :] for s in range(n_slab)], axis=-1)
    for b in range(nb):
        tailn_ref[b] = ext_s[b, state_t + 1:state_t + 1 + SUBLANES, :]


def _rec(xy, cw, cb, wg, bg, lru, g, layer, h0, tail0, *, nb, seq, tc, state_t):
    xy3 = xy.reshape(nb, seq, 2 * REC_W)
    pitch = tc + SUBLANES
    n_slab = REC_W // LANES
    out, hn, tailn = pl.pallas_call(
        functools.partial(_rec_kernel, nb=nb, tc=tc, pitch=pitch, state_t=state_t),
        grid=(seq // tc,),
        in_specs=[
            pl.BlockSpec((nb, tc, REC_W), lambda t: (0, t, 0)),
            pl.BlockSpec((nb, tc, REC_W), lambda t: (0, t, 1)),
            _of_layer(cw, layer),
            _of_layer(cb, layer),
            _of_layer(wg, layer),
            _of_layer(bg, layer),
            _of_layer(lru, layer),
            _of_layer(g, layer),
            _whole(h0),
            _whole(tail0),
        ],
        out_specs=[
            pl.BlockSpec((nb, tc, REC_W), lambda t: (0, t, 0)),
            pl.BlockSpec((nb, REC_W), lambda t: (0, 0)),
            pl.BlockSpec((nb, SUBLANES, REC_W), lambda t: (0, 0, 0)),
        ],
        out_shape=[
            jax.ShapeDtypeStruct((nb, seq, REC_W), BF16),
            jax.ShapeDtypeStruct((nb, REC_W), F32),
            jax.ShapeDtypeStruct((nb, SUBLANES, REC_W), F32),
        ],
        scratch_shapes=[
            pltpu.VMEM((nb, tc + SUBLANES, REC_W), F32),
            pltpu.VMEM((n_slab, nb * pitch, LANES), F32),
            pltpu.VMEM((n_slab, nb * pitch, LANES), F32),
            pltpu.VMEM((nb, REC_W), F32),
        ],
        compiler_params=pltpu.CompilerParams(
            dimension_semantics=("arbitrary",), vmem_limit_bytes=VMEM_LIMIT),
        name="rec",
    )(xy3, xy3, cw, cb, wg, bg, lru, g, h0, tail0)
    return out.reshape(nb * seq, REC_W), hn, tailn


def _post_kernel(h_ref, an_ref, xy_ref, hm_ref, anm_ref, rnm_ref,
                 wo_ref, g_ref, wu_ref, wd_ref, fg_ref,
                 cw_ref, cb_ref, wg_ref, bg_ref, lru_ref, grec_ref, h0_ref, tail0_ref,
                 o_ref, *rest, n_tiles, final, nb, tc):
    om_ref = None if final else rest[0]
    wo_s, wu_s, wd_s, rn_s, ext_s, a_s, u_s, hst_s = rest[-8:]
    i = pl.program_id(0)
    rows_o = wo_ref.shape[0]
    rows_d = wd_ref.shape[0]
    first_tile = W_CHUNKS + 1

    @pl.when(i < W_CHUNKS)
    def _():
        wo_s[pl.ds(pl.multiple_of(i * rows_o, rows_o), rows_o), :] = wo_ref[...].astype(BF16)
        wu_s[i] = wu_ref[...].astype(BF16)
        wd_s[pl.ds(pl.multiple_of(i * rows_d, rows_d), rows_d), :] = wd_ref[...].astype(BF16)

    @pl.when(i == 0)
    def _():
        _rec_init(h0_ref, tail0_ref, ext_s, hst_s, nb=nb, tc=tc)

    rr = _RecRefs(cw_ref, cb_ref, wg_ref, bg_ref, lru_ref, grec_ref, ext_s, a_s, u_s, hst_s,
                  nb=nb, tc=tc)

    def rec_gates(b):
        _rec_gates(rr, b, xy_ref[b, :, 0:REC_W])

    def rec_output(b, slot):
        rn_s[slot, b * tc:(b + 1) * tc, :] = _rec_output(rr, b, xy_ref[b, :, REC_W:2 * REC_W])

    @pl.when(i == W_CHUNKS)
    def _():
        for b in range(nb):
            rec_gates(b)
        _rec_scan(rr, unroll=True)
        for b in range(nb):
            rec_output(b, 0)

    def mix_and_mlp(h_tile, an, rn, between=None):
        h1 = h_tile + _dot(an, wo_s[0:ATT_W, :]) + _dot(rn, wo_s[ATT_W:ATT_W + REC_W, :])
        z = _rms(h1, g_ref[...]).astype(BF16)
        down = None
        for c in range(W_CHUNKS):
            if between is not None:
                between(c)
            u = jnp.maximum(_dot(z, wu_s[c]), 0.0)
            part = _dot((u * u).astype(BF16), wd_s[c * rows_d:(c + 1) * rows_d, :])
            down = part if down is None else down + part
        return h1 + down

    @pl.when(jnp.logical_and(i >= first_tile, i < first_tile + n_tiles))
    def _():
        s = i - first_tile
        nxt = lax.rem(s + 1, 2)
        per = 2 * nb // W_CHUNKS

        def next_tile_rec(c):
            half = W_CHUNKS // 2
            if c < half:
                for b in range(c * per, (c + 1) * per):
                    rec_gates(b)
            else:
                if c == half:
                    _rec_scan(rr, unroll=True)
                for b in range((c - half) * per, (c - half + 1) * per):
                    rec_output(b, nxt)

        rn = rn_s[lax.rem(s, 2)]
        h_tile = jnp.concatenate([h_ref[b] for b in range(nb)], axis=0)
        an = jnp.concatenate([an_ref[b] for b in range(nb)], axis=0)
        acc = mix_and_mlp(h_tile, an, rn, next_tile_rec)
        if final:
            acc = _rms(acc, fg_ref[...])
        for b in range(nb):
            o_ref[b] = acc[b * tc:(b + 1) * tc, :]

    if not final:
        @pl.when(i == first_tile + n_tiles)
        def _():
            om_ref[...] = mix_and_mlp(hm_ref[...], anm_ref[...], rnm_ref[...])


def _post(h, an, xy, hm, an_m, rn_m, w_out, w_up, w_down, g, fg,
          cw, cb, wg, bg, lru, g_rec, h0, tail0, layer, *, nb, seq, tc, final):
    n, d = h.shape
    n_tiles = seq // tc
    d_mix, d_ff = w_out.shape[1], w_up.shape[2]
    rows_o, cols_u, rows_d = d_mix // W_CHUNKS, d_ff // W_CHUNKS, d_ff // W_CHUNKS
    pitch = tc + SUBLANES
    n_slab = REC_W // LANES
    h3 = h.reshape(nb, seq, d)
    an3 = an.reshape(nb, seq, ATT_W)
    xy3 = xy.reshape(nb, seq, 2 * REC_W)

    def tile(i):
        return (0, jnp.clip(i - (W_CHUNKS + 1), 0, n_tiles - 1), 0)

    def next_tile(i):
        return (0, jnp.clip(i - W_CHUNKS, 0, n_tiles - 1), 0)

    def chunk(i):
        return jnp.minimum(i, W_CHUNKS - 1)

    out_specs = [pl.BlockSpec((nb, tc, d), tile)]
    out_shape = [jax.ShapeDtypeStruct((nb, seq, d), F32)]
    if not final:
        out_specs.append(pl.BlockSpec(hm.shape, lambda i: (0, 0)))
        out_shape.append(jax.ShapeDtypeStruct(hm.shape, F32))

    outs = pl.pallas_call(
        functools.partial(_post_kernel, n_tiles=n_tiles, final=final, nb=nb, tc=tc),
        grid=(W_CHUNKS + 1 + n_tiles + (0 if final else 1),),
        in_specs=[
            pl.BlockSpec((nb, tc, d), tile),
            pl.BlockSpec((nb, tc, ATT_W), tile),
            pl.BlockSpec((nb, tc, 2 * REC_W), next_tile),
            _whole(hm),
            _whole(an_m),
            _whole(rn_m),
            pl.BlockSpec((None, rows_o, d), lambda i: (layer, chunk(i), 0)),
            _of_layer(g, layer),
            pl.BlockSpec((None, d, cols_u), lambda i: (layer, 0, chunk(i))),
            pl.BlockSpec((None, rows_d, d), lambda i: (layer, chunk(i), 0)),
            _whole(fg),
            _of_layer(cw, layer),
            _of_layer(cb, layer),
            _of_layer(wg, layer),
            _of_layer(bg, layer),
            _of_layer(lru, layer),
            _of_layer(g_rec, layer),
            _whole(h0),
            _whole(tail0),
        ],
        out_specs=out_specs,
        out_shape=out_shape,
        scratch_shapes=[
            pltpu.VMEM((d_mix, d), BF16),
            pltpu.VMEM((W_CHUNKS, d, cols_u), BF16),
            pltpu.VMEM((d_ff, d), BF16),
            pltpu.VMEM((2, nb * tc, REC_W), BF16),
            pltpu.VMEM((nb, tc + SUBLANES, REC_W), F32),
            pltpu.VMEM((n_slab, nb * pitch, LANES), F32),
            pltpu.VMEM((n_slab, nb * pitch, LANES), F32),
            pltpu.VMEM((nb, REC_W), F32),
        ],
        compiler_params=pltpu.CompilerParams(
            dimension_semantics=("arbitrary",), vmem_limit_bytes=VMEM_LIMIT),
        name="post",
    )(h3, an3, xy3, hm, an_m, rn_m, w_out, g, w_up, w_down, fg,
      cw, cb, wg, bg, lru, g_rec, h0, tail0)
    h_new = outs[0].reshape(n, d)
    return (h_new, hm) if final else (h_new, outs[1])


def _block_diag_gates(w_a, w_x):
    depth, nblk, blk, _ = w_a.shape
    per = nblk // 2
    eye = jnp.eye(per, dtype=w_a.dtype)

    def bd(w):
        w5 = w.reshape(depth, 2, per, blk, blk)
        full = jnp.einsum('lgpij,pq->lgpiqj', w5, eye)
        return full.reshape(depth, 2, per * blk, per * blk)

    return jnp.concatenate([bd(w_a), bd(w_x)], axis=-1).astype(BF16)


def kernel(x, meta, attn_norm_g, w_in, b_f, conv_w, conv_b, w_gate_a, b_gate_a,
           w_gate_x, b_gate_x, lru_L, attn_out_g, rec_out_g, w_out, mlp_norm_g,
           w_up, w_down, final_g):
    nb, seq, d = x.shape
    depth = w_in.shape[0]
    c_f, c_x = 3 * ATT_W, 3 * ATT_W + N_HEADS

    wf = jnp.pad(w_in[:, :, c_f:c_x], ((0, 0), (0, 0), (0, LANES - N_HEADS)))
    wxy = w_in[:, :, c_x:]
    bf = jnp.pad(b_f, ((0, 0), (0, LANES - N_HEADS))).reshape(depth, 1, LANES)
    g_in = attn_norm_g.reshape(depth, 1, d)
    cb = conv_b.reshape(depth, 1, REC_W)
    wg = _block_diag_gates(w_gate_a, w_gate_x)
    bg = jnp.stack([b_gate_a, b_gate_x], axis=1)
    lru = lru_L.reshape(depth, 1, REC_W)
    g_att = attn_out_g.reshape(depth, 1, ATT_W)
    g_rec = rec_out_g.reshape(depth, 1, REC_W)
    g_mlp = mlp_norm_g.reshape(depth, 1, d)
    fg = final_g.reshape(1, d)
    h0_meta = jnp.zeros((1, REC_W), F32)
    tail0_meta = jnp.zeros((1, SUBLANES, REC_W), F32)

    h = x.reshape(nb * seq, d)
    hm = jnp.pad(meta, ((0, META_PAD - N_META), (0, 0)))

    for l in range(depth):
        last = l == depth - 1
        k, qvt, logf, xy, k_m, qvt_m, logf_m, xy_m = _inproj(
            h, hm, g_in, w_in, wf, wxy, bf, l, tm=ROW_TILE, tile=ATT_TILE)

        an_m, bias_m = _meta_attn(k_m, qvt_m, logf_m, g_att, l)
        rn_m, h_state, x_tail = _rec(
            xy_m, conv_w, cb, wg, bg, lru, g_rec, l, h0_meta, tail0_meta,
            nb=1, seq=META_PAD, tc=META_PAD, state_t=N_META - 1)

        an = _attn(k, qvt, logf, k_m, qvt_m, bias_m, g_att, l, seq=seq, tq=ATT_TILE)
        h, hm = _post(h, an, xy, hm, an_m, rn_m, w_out, w_up, w_down, g_mlp, fg,
                      conv_w, cb, wg, bg, lru, g_rec, h_state, x_tail, l,
                      nb=nb, seq=seq, tc=TIME_TILE, final=last)

    return h.reshape(nb, seq, d)
```

```python
import functools
import math

import jax
import jax.numpy as jnp
from jax import lax
from jax.experimental import pallas as pl
from jax.experimental.pallas import tpu as pltpu

F32 = jnp.float32
BF16 = jnp.bfloat16

N_META = 16
META_PAD = 128
N_HEADS = 8
HEAD_DIM = 64
ATT_W = N_HEADS * HEAD_DIM
REC_W = 512
CONV_WIDTH = 4
RG_C = 8.0
NORM_EPS = 1e-6
LANES = 128
SUBLANES = 8
LOG2E = math.log2(math.e)
ATT_TILE = 512
ROW_TILE = 1024
TIME_TILE = 64
W_CHUNKS = 8
IN_CHUNKS = 4
SUM_ROWS = 16
ACC_ROWS = HEAD_DIM + SUM_ROWS
VMEM_LIMIT = 56 * 1024 * 1024


def _rms(x, g):
    ms = jnp.mean(x * x, axis=-1, keepdims=True)
    return x * lax.rsqrt(ms + NORM_EPS) * g


def _log_sigmoid(x):
    return jnp.minimum(x, 0.0) - jnp.log1p(jnp.exp(-jnp.abs(x)))


def _gelu_tanh(x):
    c = math.sqrt(2.0 / math.pi)
    return x * (0.5 * (1.0 + jnp.tanh(c * (x + 0.044715 * (x * x * x)))))


def _dot(a, b):
    return jnp.dot(a, b, preferred_element_type=F32)


def _dot_nt(a, b):
    return lax.dot_general(a, b, (((1,), (1,)), ((), ())), preferred_element_type=F32)


def _whole(arr):
    nd = arr.ndim
    return pl.BlockSpec(arr.shape, lambda *_: (0,) * nd, pipeline_mode=pl.Buffered(1))


def _of_layer(arr, layer):
    nd = arr.ndim - 1
    return pl.BlockSpec((None,) + arr.shape[1:], lambda *_: (layer,) + (0,) * nd,
                        pipeline_mode=pl.Buffered(1))


def _inproj_kernel(h_ref, hm_ref, g_ref, wq_ref, wk_ref, wv_ref, wf_ref, wxy_ref, bf_ref,
                   k_ref, qvt_ref, logf_ref, xy_ref, km_ref, qvtm_ref, logfm_ref, xym_ref,
                   wk_s, wqvt_s, wf_s, wxy_s, *, n_tiles):
    i = pl.program_id(0)

    @pl.when(i < IN_CHUNKS)
    def _():
        rows = wq_ref.shape[0]
        r0 = pl.multiple_of(i * rows, rows)
        wq = wq_ref[...] * (LOG2E / math.sqrt(HEAD_DIM))
        wqvt_s[pl.ds(r0, rows), :] = wq.astype(BF16)
        wqvt_s[pl.ds(ATT_W + r0, rows), :] = wv_ref[...].astype(BF16)
        wxy_s[i] = wxy_ref[...].T.astype(BF16)

    @pl.when(i < wk_s.shape[0])
    def _():
        wk_s[i] = wk_ref[...].T.astype(BF16)

    @pl.when(i == 0)
    def _():
        wf_s[...] = wf_ref[...].T.astype(BF16)

    def project(h_tile, k_out, qvt_out, logf_out, xy_out):
        z = _rms(h_tile, g_ref[...]).astype(BF16)
        k_out[...] = jnp.concatenate(
            [_dot(z, wk_s[c]) for c in range(wk_s.shape[0])], axis=-1).astype(BF16)
        qvt = _dot_nt(wqvt_s[...], z).astype(BF16)
        tile = qvt_out.shape[2]
        for s in range(qvt_out.shape[0]):
            qvt_out[s] = qvt[:, s * tile:(s + 1) * tile]
        xy_out[...] = jnp.concatenate(
            [_dot(z, wxy_s[c]) for c in range(wxy_s.shape[0])], axis=-1)
        logf_out[...] = _log_sigmoid(_dot(z, wf_s[...]) + bf_ref[...])

    @pl.when(jnp.logical_and(i >= IN_CHUNKS, i < IN_CHUNKS + n_tiles))
    def _():
        project(h_ref[...], k_ref, qvt_ref, logf_ref, xy_ref)

    @pl.when(i == IN_CHUNKS + n_tiles)
    def _():
        project(hm_ref[...], km_ref, qvtm_ref, logfm_ref, xym_ref)


def _inproj(h, hm, g, w_t, wf_t, wxy_t, bf, layer, *, tm, tile):
    n, d = h.shape
    tmeta = hm.shape[0]
    n_tiles = n // tm
    per = tm // tile
    rows = ATT_W // IN_CHUNKS
    rows_k = 2 * rows
    n_k = ATT_W // rows_k
    rows_xy = 2 * REC_W // IN_CHUNKS
    q_blk, k_blk, v_blk = 0, ATT_W // rows_k, 2 * (ATT_W // rows)

    def chunk(i):
        return jnp.minimum(i, IN_CHUNKS - 1)

    def row(i):
        return jnp.clip(i - IN_CHUNKS, 0, n_tiles - 1)

    return pl.pallas_call(
        functools.partial(_inproj_kernel, n_tiles=n_tiles),
        grid=(IN_CHUNKS + n_tiles + 1,),
        in_specs=[
            pl.BlockSpec((tm, d), lambda i: (row(i), 0)),
            _whole(hm),
            _of_layer(g, layer),
            pl.BlockSpec((None, rows, d), lambda i: (layer, q_blk + chunk(i), 0)),
            pl.BlockSpec((None, rows_k, d), lambda i: (layer, k_blk + jnp.minimum(i, n_k - 1), 0)),
            pl.BlockSpec((None, rows, d), lambda i: (layer, v_blk + chunk(i), 0)),
            _of_layer(wf_t, layer),
            pl.BlockSpec((None, rows_xy, d), lambda i: (layer, chunk(i), 0)),
            _of_layer(bf, layer),
        ],
        out_specs=[
            pl.BlockSpec((tm, ATT_W), lambda i: (row(i), 0)),
            pl.BlockSpec((per, 2 * ATT_W, tile), lambda i: (row(i), 0, 0)),
            pl.BlockSpec((tm, LANES), lambda i: (row(i), 0)),
            pl.BlockSpec((tm, 2 * REC_W), lambda i: (row(i), 0)),
            pl.BlockSpec((tmeta, ATT_W), lambda i: (0, 0)),
            pl.BlockSpec((1, 2 * ATT_W, tmeta), lambda i: (0, 0, 0)),
            pl.BlockSpec((tmeta, LANES), lambda i: (0, 0)),
            pl.BlockSpec((tmeta, 2 * REC_W), lambda i: (0, 0)),
        ],
        out_shape=[
            jax.ShapeDtypeStruct((n, ATT_W), BF16),
            jax.ShapeDtypeStruct((n // tile, 2 * ATT_W, tile), BF16),
            jax.ShapeDtypeStruct((n, LANES), F32),
            jax.ShapeDtypeStruct((n, 2 * REC_W), F32),
            jax.ShapeDtypeStruct((tmeta, ATT_W), BF16),
            jax.ShapeDtypeStruct((1, 2 * ATT_W, tmeta), BF16),
            jax.ShapeDtypeStruct((tmeta, LANES), F32),
            jax.ShapeDtypeStruct((tmeta, 2 * REC_W), F32),
        ],
        scratch_shapes=[
            pltpu.VMEM((n_k, d, rows_k), BF16),
            pltpu.VMEM((2 * ATT_W, d), BF16),
            pltpu.VMEM((d, LANES), BF16),
            pltpu.VMEM((IN_CHUNKS, d, rows_xy), BF16),
        ],
        compiler_params=pltpu.CompilerParams(
            dimension_semantics=("arbitrary",), vmem_limit_bytes=VMEM_LIMIT),
        name="inproj",
    )(h, hm, g, w_t, w_t, w_t, wf_t, wxy_t, bf)


def _neg_cumsum_cols(lf_ref, out_ref):
    r = lax.broadcasted_iota(jnp.int32, (LANES, LANES), 0)
    c = lax.broadcasted_iota(jnp.int32, (LANES, LANES), 1)
    lower = (r >= c).astype(BF16)
    carry = jnp.zeros((1, LANES), F32)
    for blk in range(lf_ref.shape[0] // LANES):
        x = lf_ref[blk * LANES:(blk + 1) * LANES, :]
        hi = x.astype(BF16)
        r1 = x - hi.astype(F32)
        mid = r1.astype(BF16)
        lo = (r1 - mid.astype(F32)).astype(BF16)
        cs = _dot(lower, hi) + _dot(lower, mid) + _dot(lower, lo) + carry
        out_ref[blk * LANES:(blk + 1) * LANES, :] = cs * (-LOG2E)
        carry = cs[LANES - 1:LANES, :]


def _scores_t(kh, qth, bias_col, mask):
    st = _dot(kh, qth) + bias_col
    if mask is not None:
        st = jnp.where(mask, st, -jnp.inf)
    return st


def _pv_and_sum(vth, p):
    ones = jnp.ones((SUM_ROWS, vth.shape[1]), BF16)
    return _dot(jnp.concatenate([vth, ones], axis=0), p)


def _softmax_first(st, vth):
    m = jnp.max(st, axis=0, keepdims=True)
    return m, _pv_and_sum(vth, jnp.exp2(st - m).astype(BF16))


def _softmax_next(st, vth, m, acc):
    m_new = jnp.maximum(m, jnp.max(st, axis=0, keepdims=True))
    alpha = jnp.exp2(m - m_new)
    return m_new, alpha * acc + _pv_and_sum(vth, jnp.exp2(st - m_new).astype(BF16))


def _normalized(acc):
    return acc[:HEAD_DIM, :] / acc[HEAD_DIM:HEAD_DIM + 1, :]


def _meta_attn_kernel(k_ref, qt_ref, vt_ref, lf_ref, g_ref, o_ref, bm_ref, b_s, ot_s):
    t = k_ref.shape[0]
    _neg_cumsum_cols(lf_ref, b_s)
    row = lax.broadcasted_iota(jnp.int32, (t, LANES), 0)
    bm_ref[...] = jnp.where(row < N_META, b_s[...] - b_s[N_META - 1:N_META, :], -jnp.inf)
    key = lax.broadcasted_iota(jnp.int32, (t, t), 0)
    qry = lax.broadcasted_iota(jnp.int32, (t, t), 1)
    causal = key <= qry
    for h in range(N_HEADS):
        sl = slice(h * HEAD_DIM, (h + 1) * HEAD_DIM)
        st = _scores_t(k_ref[:, sl], qt_ref[0, sl, :], b_s[:, h:h + 1], causal)
        _, acc = _softmax_first(st, vt_ref[0, sl, :])
        ot_s[sl, :] = _normalized(acc)
    o_ref[...] = _rms(ot_s[...].T, g_ref[...]).astype(BF16)


def _meta_attn(k, qvt, logf, g, layer):
    t = k.shape[0]
    return pl.pallas_call(
        _meta_attn_kernel,
        grid=(1,),
        in_specs=[
            pl.BlockSpec((t, ATT_W), lambda i: (0, 0)),
            pl.BlockSpec((1, ATT_W, t), lambda i: (0, 0, 0)),
            pl.BlockSpec((1, ATT_W, t), lambda i: (0, 1, 0)),
            pl.BlockSpec((t, LANES), lambda i: (0, 0)),
            _of_layer(g, layer),
        ],
        out_specs=[
            pl.BlockSpec((t, ATT_W), lambda i: (0, 0)),
            pl.BlockSpec((t, LANES), lambda i: (0, 0)),
        ],
        out_shape=[
            jax.ShapeDtypeStruct((t, ATT_W), BF16),
            jax.ShapeDtypeStruct((t, LANES), F32),
        ],
        scratch_shapes=[pltpu.VMEM((t, LANES), F32), pltpu.VMEM((ATT_W, t), F32)],
        name="meta_attn",
    )(k, qvt, qvt, logf, g)


def _attn_kernel(qt_ref, k_ref, vt_ref, lf_ref, km_ref, vtm_ref, bm_ref, g_ref,
                 o_ref, b_s, m_s, acc_s, ot_s, *, tq):
    j = pl.program_id(1)

    @pl.when(j == 0)
    def _():
        _neg_cumsum_cols(lf_ref, b_s)

    heads = [slice(h * HEAD_DIM, (h + 1) * HEAD_DIM) for h in range(N_HEADS)]
    accs = [slice(h * ACC_ROWS, (h + 1) * ACC_ROWS) for h in range(N_HEADS)]

    sts = [_scores_t(km_ref[0:N_META, sl], qt_ref[0, sl, :], bm_ref[0:N_META, h:h + 1], None)
           for h, sl in enumerate(heads)]
    for h, sl in enumerate(heads):
        m, acc = _softmax_first(sts[h], vtm_ref[0, sl, 0:N_META])
        m_s[h:h + 1, :] = m
        acc_s[accs[h], :] = acc

    def tile(i, mask, finish):
        off = pl.multiple_of(i * tq, tq)

        def scores(h):
            sl = heads[h]
            return _scores_t(k_ref[pl.ds(off, tq), sl], qt_ref[0, sl, :],
                             b_s[pl.ds(off, tq), h:h + 1], mask)

        sts = [scores(0)]
        for h, sl in enumerate(heads):
            if h + 1 < N_HEADS:
                sts.append(scores(h + 1))
            m, acc = _softmax_next(sts[h], vt_ref[i, sl, :], m_s[h:h + 1, :], acc_s[accs[h], :])
            if finish:
                ot_s[sl, :] = _normalized(acc)
            else:
                m_s[h:h + 1, :] = m
                acc_s[accs[h], :] = acc

    def body(i, carry):
        tile(i, None, False)
        return carry

    lax.fori_loop(0, j, body, 0)

    key = lax.broadcasted_iota(jnp.int32, (tq, tq), 0)
    qry = lax.broadcasted_iota(jnp.int32, (tq, tq), 1)
    tile(j, key <= qry, True)
    o_ref[...] = _rms(ot_s[...].T, g_ref[...]).astype(BF16)


def _attn(k, qvt, logf, k_meta, qvt_meta, bias_meta, g, layer, *, seq, tq):
    n = k.shape[0]
    nb = n // seq
    per_seq = seq // tq
    tmeta = k_meta.shape[0]
    return pl.pallas_call(
        functools.partial(_attn_kernel, tq=tq),
        grid=(nb, per_seq),
        in_specs=[
            pl.BlockSpec((1, ATT_W, tq), lambda b, j: (b * per_seq + j, 0, 0)),
            pl.BlockSpec((seq, ATT_W), lambda b, j: (b, 0)),
            pl.BlockSpec((per_seq, ATT_W, tq), lambda b, j: (b, 1, 0)),
            pl.BlockSpec((seq, LANES), lambda b, j: (b, 0)),
            pl.BlockSpec((tmeta, ATT_W), lambda b, j: (0, 0)),
            pl.BlockSpec((1, ATT_W, tmeta), lambda b, j: (0, 1, 0)),
            _whole(bias_meta),
            _of_layer(g, layer),
        ],
        out_specs=pl.BlockSpec((tq, ATT_W), lambda b, j: (b * per_seq + j, 0)),
        out_shape=jax.ShapeDtypeStruct((n, ATT_W), BF16),
        scratch_shapes=[
            pltpu.VMEM((seq, LANES), F32),
            pltpu.VMEM((N_HEADS, tq), F32),
            pltpu.VMEM((N_HEADS * ACC_ROWS, tq), F32),
            pltpu.VMEM((ATT_W, tq), F32),
        ],
        compiler_params=pltpu.CompilerParams(
            dimension_semantics=("arbitrary", "arbitrary"), vmem_limit_bytes=VMEM_LIMIT),
        name="attn",
    )(qvt, k, qvt, logf, k_meta, qvt_meta, bias_meta, g)


def _rec_init(h0_ref, tail0_ref, ext_s, h_s, *, nb, tc):
    for b in range(nb):
        ext_s[b, tc:tc + SUBLANES, :] = tail0_ref[0]
    h_s[...] = jnp.broadcast_to(h0_ref[...], h_s.shape)


class _RecRefs:
    def __init__(self, cw_ref, cb_ref, wg_ref, bg_ref, lru_ref, g_ref, ext_s, a_s, u_s, h_s,
                 *, nb, tc):
        self.cw, self.cb, self.wg, self.bg, self.lru, self.g = (
            cw_ref, cb_ref, wg_ref, bg_ref, lru_ref, g_ref)
        self.ext_s, self.a_s, self.u_s, self.h_s = ext_s, a_s, u_s, h_s
        self.nb, self.tc = nb, tc
        self.pitch = tc + SUBLANES
        self.n_slab = REC_W // LANES


def _rec_gates(rr, b, xr_b):
    tc, half = rr.tc, REC_W // 2
    rr.ext_s[b, 0:SUBLANES, :] = rr.ext_s[b, tc:tc + SUBLANES, :]
    rr.ext_s[b, SUBLANES:SUBLANES + tc, :] = xr_b
    ext = rr.ext_s[b]
    run = rr.cw[0:1, :] * ext
    for k in range(1, CONV_WIDTH):
        run = pltpu.roll(run, 1, 0) + rr.cw[k:k + 1, :] * ext
    xc = run[SUBLANES:, :] + rr.cb[...]
    xcb = xc.astype(BF16)
    g0 = _dot(xcb[:, :half], rr.wg[0])
    g1 = _dot(xcb[:, half:], rr.wg[1])
    ra = jnp.concatenate([g0[:, :half], g1[:, :half]], axis=-1) + rr.bg[0:1, :]
    rx = jnp.concatenate([g0[:, half:], g1[:, half:]], axis=-1) + rr.bg[1:2, :]
    r = jax.nn.sigmoid(ra)
    i = jax.nn.sigmoid(rx)
    log_a = RG_C * r * _log_sigmoid(rr.lru[...])
    a = jnp.exp(log_a)
    th = jnp.tanh(log_a)
    om = -2.0 * th / (1.0 - th)
    mult = jnp.where(om > 0.0, om * lax.rsqrt(om), 0.0)
    u = mult * i * xc
    lo = b * rr.pitch
    for s in range(rr.n_slab):
        rr.a_s[s, lo:lo + tc, :] = a[:, s * LANES:(s + 1) * LANES]
        rr.u_s[s, lo:lo + tc, :] = u[:, s * LANES:(s + 1) * LANES]


def _rec_scan(rr, *, unroll):
    def rows(t):
        return pl.ds(t, rr.nb, stride=rr.pitch) if rr.nb > 1 else pl.ds(t, 1)

    def scan_body(t, hs):
        out = []
        for s in range(rr.n_slab):
            hn = rr.a_s[s, rows(t), :] * hs[s] + rr.u_s[s, rows(t), :]
            rr.u_s[s, rows(t), :] = hn
            out.append(hn)
        return tuple(out)

    hs = tuple(rr.h_s[:, s * LANES:(s + 1) * LANES] for s in range(rr.n_slab))
    if unroll:
        for t in range(rr.tc):
            hs = scan_body(t, hs)
    else:
        hs = lax.fori_loop(0, rr.tc, scan_body, hs, unroll=8)
    rr.h_s[...] = jnp.concatenate(hs, axis=-1)


def _rec_output(rr, b, yr_b):
    lo = b * rr.pitch
    hb = jnp.concatenate(
        [rr.u_s[s, lo:lo + rr.tc, :] for s in range(rr.n_slab)], axis=-1)
    return _rms(hb * _gelu_tanh(yr_b), rr.g[...]).astype(BF16)


def _rec_kernel(xr_ref, yr_ref, cw_ref, cb_ref, wg_ref, bg_ref, lru_ref, g_ref,
                h0_ref, tail0_ref, out_ref, hn_ref, tailn_ref,
                ext_s, a_s, u_s, h_s, *, nb, tc, pitch, state_t):
    @pl.when(pl.program_id(0) == 0)
    def _():
        _rec_init(h0_ref, tail0_ref, ext_s, h_s, nb=nb, tc=tc)

    rr = _RecRefs(cw_ref, cb_ref, wg_ref, bg_ref, lru_ref, g_ref, ext_s, a_s, u_s, h_s,
                  nb=nb, tc=tc)
    for b in range(nb):
        _rec_gates(rr, b, xr_ref[b])
    _rec_scan(rr, unroll=False)
    for b in range(nb):
        out_ref[b] = _rec_output(rr, b, yr_ref[b])

    n_slab = REC_W // LANES
    rows = pl.ds(state_t, nb, stride=pitch) if nb > 1 else pl.ds(state_t, 1)
    hn_ref[...] = jnp.concatenate([u_s[s, rows, :] for s in range(n_slab)], axis=-1)
    for b in range(nb):
        tailn_ref[b] = ext_s[b, state_t + 1:state_t + 1 + SUBLANES, :]


def _rec(xy, cw, cb, wg, bg, lru, g, layer, h0, tail0, *, nb, seq, tc, state_t):
    xy3 = xy.reshape(nb, seq, 2 * REC_W)
    pitch = tc + SUBLANES
    n_slab = REC_W // LANES
    out, hn, tailn = pl.pallas_call(
        functools.partial(_rec_kernel, nb=nb, tc=tc, pitch=pitch, state_t=state_t),
        grid=(seq // tc,),
        in_specs=[
            pl.BlockSpec((nb, tc, REC_W), lambda t: (0, t, 0)),
            pl.BlockSpec((nb, tc, REC_W), lambda t: (0, t, 1)),
            _of_layer(cw, layer),
            _of_layer(cb, layer),
            _of_layer(wg, layer),
            _of_layer(bg, layer),
            _of_layer(lru, layer),
            _of_layer(g, layer),
            _whole(h0),
            _whole(tail0),
        ],
        out_specs=[
            pl.BlockSpec((nb, tc, REC_W), lambda t: (0, t, 0)),
            pl.BlockSpec((nb, REC_W), lambda t: (0, 0)),
            pl.BlockSpec((nb, SUBLANES, REC_W), lambda t: (0, 0, 0)),
        ],
        out_shape=[
            jax.ShapeDtypeStruct((nb, seq, REC_W), BF16),
            jax.ShapeDtypeStruct((nb, REC_W), F32),
            jax.ShapeDtypeStruct((nb, SUBLANES, REC_W), F32),
        ],
        scratch_shapes=[
            pltpu.VMEM((nb, tc + SUBLANES, REC_W), F32),
            pltpu.VMEM((n_slab, nb * pitch, LANES), F32),
            pltpu.VMEM((n_slab, nb * pitch, LANES), F32),
            pltpu.VMEM((nb, REC_W), F32),
        ],
        compiler_params=pltpu.CompilerParams(
            dimension_semantics=("arbitrary",), vmem_limit_bytes=VMEM_LIMIT),
        name="rec",
    )(xy3, xy3, cw, cb, wg, bg, lru, g, h0, tail0)
    return out.reshape(nb * seq, REC_W), hn, tailn


def _post_kernel(h_ref, an_ref, xy_ref, hm_ref, anm_ref, rnm_ref,
                 wo_ref, g_ref, wu_ref, wd_ref, fg_ref,
                 cw_ref, cb_ref, wg_ref, bg_ref, lru_ref, grec_ref, h0_ref, tail0_ref,
                 o_ref, *rest, n_tiles, final, nb, tc):
    om_ref = None if final else rest[0]
    wo_s, wu_s, wd_s, rn_s, ext_s, a_s, u_s, hst_s = rest[-8:]
    i = pl.program_id(0)
    rows_o = wo_ref.shape[0]
    rows_d = wd_ref.shape[0]
    first_tile = W_CHUNKS + 1

    @pl.when(i < W_CHUNKS)
    def _():
        wo_s[pl.ds(pl.multiple_of(i * rows_o, rows_o), rows_o), :] = wo_ref[...].astype(BF16)
        wu_s[i] = wu_ref[...].astype(BF16)
        wd_s[pl.ds(pl.multiple_of(i * rows_d, rows_d), rows_d), :] = wd_ref[...].astype(BF16)

    @pl.when(i == 0)
    def _():
        _rec_init(h0_ref, tail0_ref, ext_s, hst_s, nb=nb, tc=tc)

    rr = _RecRefs(cw_ref, cb_ref, wg_ref, bg_ref, lru_ref, grec_ref, ext_s, a_s, u_s, hst_s,
                  nb=nb, tc=tc)

    def rec_gates(b):
        _rec_gates(rr, b, xy_ref[b, :, 0:REC_W])

    def rec_output(b, slot):
        rn_s[slot, b * tc:(b + 1) * tc, :] = _rec_output(rr, b, xy_ref[b, :, REC_W:2 * REC_W])

    @pl.when(i == W_CHUNKS)
    def _():
        for b in range(nb):
            rec_gates(b)
        _rec_scan(rr, unroll=True)
        for b in range(nb):
            rec_output(b, 0)

    def mix_and_mlp(h_tile, an, rn, between=None):
        h1 = h_tile + _dot(an, wo_s[0:ATT_W, :]) + _dot(rn, wo_s[ATT_W:ATT_W + REC_W, :])
        z = _rms(h1, g_ref[...]).astype(BF16)
        down = None
        for c in range(W_CHUNKS):
            if between is not None:
                between(c)
            u = jnp.maximum(_dot(z, wu_s[c]), 0.0)
            part = _dot((u * u).astype(BF16), wd_s[c * rows_d:(c + 1) * rows_d, :])
            down = part if down is None else down + part
        return h1 + down

    @pl.when(jnp.logical_and(i >= first_tile, i < first_tile + n_tiles))
    def _():
        s = i - first_tile
        nxt = lax.rem(s + 1, 2)
        per = 2 * nb // W_CHUNKS

        def next_tile_rec(c):
            half = W_CHUNKS // 2
            if c < half:
                for b in range(c * per, (c + 1) * per):
                    rec_gates(b)
            else:
                if c == half:
                    _rec_scan(rr, unroll=True)
                for b in range((c - half) * per, (c - half + 1) * per):
                    rec_output(b, nxt)

        rn = rn_s[lax.rem(s, 2)]
        h_tile = jnp.concatenate([h_ref[b] for b in range(nb)], axis=0)
        an = jnp.concatenate([an_ref[b] for b in range(nb)], axis=0)
        acc = mix_and_mlp(h_tile, an, rn, next_tile_rec)
        if final:
            acc = _rms(acc, fg_ref[...])
        for b in range(nb):
            o_ref[b] = acc[b * tc:(b + 1) * tc, :]

    if not final:
        @pl.when(i == first_tile + n_tiles)
        def _():
            om_ref[...] = mix_and_mlp(hm_ref[...], anm_ref[...], rnm_ref[...])


def _post(h, an, xy, hm, an_m, rn_m, w_out, w_up, w_down, g, fg,
          cw, cb, wg, bg, lru, g_rec, h0, tail0, layer, *, nb, seq, tc, final):
    n, d = h.shape
    n_tiles = seq // tc
    d_mix, d_ff = w_out.shape[1], w_up.shape[2]
    rows_o, cols_u, rows_d = d_mix // W_CHUNKS, d_ff // W_CHUNKS, d_ff // W_CHUNKS
    pitch = tc + SUBLANES
    n_slab = REC_W // LANES
    h3 = h.reshape(nb, seq, d)
    an3 = an.reshape(nb, seq, ATT_W)
    xy3 = xy.reshape(nb, seq, 2 * REC_W)

    def tile(i):
        return (0, jnp.clip(i - (W_CHUNKS + 1), 0, n_tiles - 1), 0)

    def next_tile(i):
        return (0, jnp.clip(i - W_CHUNKS, 0, n_tiles - 1), 0)

    def chunk(i):
        return jnp.minimum(i, W_CHUNKS - 1)

    out_specs = [pl.BlockSpec((nb, tc, d), tile)]
    out_shape = [jax.ShapeDtypeStruct((nb, seq, d), F32)]
    if not final:
        out_specs.append(pl.BlockSpec(hm.shape, lambda i: (0, 0)))
        out_shape.append(jax.ShapeDtypeStruct(hm.shape, F32))

    outs = pl.pallas_call(
        functools.partial(_post_kernel, n_tiles=n_tiles, final=final, nb=nb, tc=tc),
        grid=(W_CHUNKS + 1 + n_tiles + (0 if final else 1),),
        in_specs=[
            pl.BlockSpec((nb, tc, d), tile),
            pl.BlockSpec((nb, tc, ATT_W), tile),
            pl.BlockSpec((nb, tc, 2 * REC_W), next_tile),
            _whole(hm),
            _whole(an_m),
            _whole(rn_m),
            pl.BlockSpec((None, rows_o, d), lambda i: (layer, chunk(i), 0)),
            _of_layer(g, layer),
            pl.BlockSpec((None, d, cols_u), lambda i: (layer, 0, chunk(i))),
            pl.BlockSpec((None, rows_d, d), lambda i: (layer, chunk(i), 0)),
            _whole(fg),
            _of_layer(cw, layer),
            _of_layer(cb, layer),
            _of_layer(wg, layer),
            _of_layer(bg, layer),
            _of_layer(lru, layer),
            _of_layer(g_rec, layer),
            _whole(h0),
            _whole(tail0),
        ],
        out_specs=out_specs,
        out_shape=out_shape,
        scratch_shapes=[
            pltpu.VMEM((d_mix, d), BF16),
            pltpu.VMEM((W_CHUNKS, d, cols_u), BF16),
            pltpu.VMEM((d_ff, d), BF16),
            pltpu.VMEM((2, nb * tc, REC_W), BF16),
            pltpu.VMEM((nb, tc + SUBLANES, REC_W), F32),
            pltpu.VMEM((n_slab, nb * pitch, LANES), F32),
            pltpu.VMEM((n_slab, nb * pitch, LANES), F32),
            pltpu.VMEM((nb, REC_W), F32),
        ],
        compiler_params=pltpu.CompilerParams(
            dimension_semantics=("arbitrary",), vmem_limit_bytes=VMEM_LIMIT),
        name="post",
    )(h3, an3, xy3, hm, an_m, rn_m, w_out, g, w_up, w_down, fg,
      cw, cb, wg, bg, lru, g_rec, h0, tail0)
    h_new = outs[0].reshape(n, d)
    return (h_new, hm) if final else (h_new, outs[1])


def _block_diag_gates(w_a, w_x):
    depth, nblk, blk, _ = w_a.shape
    per = nblk // 2
    eye = jnp.eye(per, dtype=w_a.dtype)

    def bd(w):
        w5 = w.reshape(depth, 2, per, blk, blk)
        full = jnp.einsum('lgpij,pq->lgpiqj', w5, eye)
        return full.reshape(depth, 2, per * blk, per * blk)

    return jnp.concatenate([bd(w_a), bd(w_x)], axis=-1).astype(BF16)


def kernel(x, meta, attn_norm_g, w_in, b_f, conv_w, conv_b, w_gate_a, b_gate_a,
           w_gate_x, b_gate_x, lru_L, attn_out_g, rec_out_g, w_out, mlp_norm_g,
           w_up, w_down, final_g):
    nb, seq, d = x.shape
    depth = w_in.shape[0]
    c_f, c_x = 3 * ATT_W, 3 * ATT_W + N_HEADS

    w_t = jnp.swapaxes(w_in, 1, 2)
    wf_t = jnp.pad(w_t[:, c_f:c_x, :], ((0, 0), (0, LANES - N_HEADS), (0, 0)))
    wxy_t = w_t[:, c_x:, :]
    bf = jnp.pad(b_f, ((0, 0), (0, LANES - N_HEADS))).reshape(depth, 1, LANES)
    g_in = attn_norm_g.reshape(depth, 1, d)
    cb = conv_b.reshape(depth, 1, REC_W)
    wg = _block_diag_gates(w_gate_a, w_gate_x)
    bg = jnp.stack([b_gate_a, b_gate_x], axis=1)
    lru = lru_L.reshape(depth, 1, REC_W)
    g_att = attn_out_g.reshape(depth, 1, ATT_W)
    g_rec = rec_out_g.reshape(depth, 1, REC_W)
    g_mlp = mlp_norm_g.reshape(depth, 1, d)
    fg = final_g.reshape(1, d)
    h0_meta = jnp.zeros((1, REC_W), F32)
    tail0_meta = jnp.zeros((1, SUBLANES, REC_W), F32)

    h = x.reshape(nb * seq, d)
    hm = jnp.pad(meta, ((0, META_PAD - N_META), (0, 0)))

    for l in range(depth):
        last = l == depth - 1
        k, qvt, logf, xy, k_m, qvt_m, logf_m, xy_m = _inproj(
            h, hm, g_in, w_t, wf_t, wxy_t, bf, l, tm=ROW_TILE, tile=ATT_TILE)

        an_m, bias_m = _meta_attn(k_m, qvt_m, logf_m, g_att, l)
        rn_m, h_state, x_tail = _rec(
            xy_m, conv_w, cb, wg, bg, lru, g_rec, l, h0_meta, tail0_meta,
            nb=1, seq=META_PAD, tc=META_PAD, state_t=N_META - 1)

        an = _attn(k, qvt, logf, k_m, qvt_m, bias_m, g_att, l, seq=seq, tq=ATT_TILE)
        h, hm = _post(h, an, xy, hm, an_m, rn_m, w_out, w_up, w_down, g_mlp, fg,
                      conv_w, cb, wg, bg, lru, g_rec, h_state, x_tail, l,
                      nb=nb, seq=seq, tc=TIME_TILE, final=last)

    return h.reshape(nb, seq, d)
```

```python
import functools
import math

import jax
import jax.numpy as jnp
from jax import lax
from jax.experimental import pallas as pl
from jax.experimental.pallas import tpu as pltpu

F32 = jnp.float32
BF16 = jnp.bfloat16

N_META = 16
META_PAD = 128
N_HEADS = 8
HEAD_DIM = 64
ATT_W = N_HEADS * HEAD_DIM
REC_W = 512
CONV_WIDTH = 4
RG_C = 8.0
NORM_EPS = 1e-6
LANES = 128
SUBLANES = 8
LOG2E = math.log2(math.e)
ATT_TILE = 512
ROW_TILE = 1024
TIME_TILE = 64
W_CHUNKS = 8
IN_CHUNKS = 4
F_ROWS = 16
SUM_ROWS = 16
ACC_ROWS = HEAD_DIM + SUM_ROWS
VMEM_LIMIT = 56 * 1024 * 1024


def _rms(x, g):
    ms = jnp.mean(x * x, axis=-1, keepdims=True)
    return x * lax.rsqrt(ms + NORM_EPS) * g


def _log_sigmoid(x):
    return jnp.minimum(x, 0.0) - jnp.log1p(jnp.exp(-jnp.abs(x)))


def _gelu_tanh(x):
    c = math.sqrt(2.0 / math.pi)
    return x * (0.5 * (1.0 + jnp.tanh(c * (x + 0.044715 * (x * x * x)))))


def _dot(a, b):
    return jnp.dot(a, b, preferred_element_type=F32)


def _dot_nt(a, b):
    return lax.dot_general(a, b, (((1,), (1,)), ((), ())), preferred_element_type=F32)


def _whole(arr):
    nd = arr.ndim
    return pl.BlockSpec(arr.shape, lambda *_: (0,) * nd, pipeline_mode=pl.Buffered(1))


def _of_layer(arr, layer):
    nd = arr.ndim - 1
    return pl.BlockSpec((None,) + arr.shape[1:], lambda *_: (layer,) + (0,) * nd,
                        pipeline_mode=pl.Buffered(1))


def _inproj_kernel(h_ref, hm_ref, g_ref, wq_ref, wk_ref, wv_ref, wf_ref, wxy_ref, bf_ref,
                   k_ref, qvt_ref, logf_ref, xy_ref, km_ref, qvtm_ref, logfm_ref, xym_ref,
                   wk_s, wqvt_s, wxy_s, *, n_tiles):
    i = pl.program_id(0)

    @pl.when(i < IN_CHUNKS)
    def _():
        rows = wq_ref.shape[0]
        r0 = pl.multiple_of(i * rows, rows)
        wq = wq_ref[...] * (LOG2E / math.sqrt(HEAD_DIM))
        wqvt_s[pl.ds(r0, rows), :] = wq.astype(BF16)
        wqvt_s[pl.ds(ATT_W + r0, rows), :] = wv_ref[...].astype(BF16)
        wxy_s[i] = wxy_ref[...].T.astype(BF16)

    @pl.when(i < wk_s.shape[0])
    def _():
        wk_s[i] = wk_ref[...].T.astype(BF16)

    @pl.when(i == 0)
    def _():
        wqvt_s[2 * ATT_W:2 * ATT_W + F_ROWS, :] = wf_ref[...].astype(BF16)

    def project(h_tile, k_out, qvt_out, logf_out, xy_out):
        z = _rms(h_tile, g_ref[...]).astype(BF16)
        k_out[...] = jnp.concatenate(
            [_dot(z, wk_s[c]) for c in range(wk_s.shape[0])], axis=-1).astype(BF16)
        qvft = _dot_nt(wqvt_s[...], z)
        qvt = qvft[:2 * ATT_W, :].astype(BF16)
        tile = qvt_out.shape[2]
        for s in range(qvt_out.shape[0]):
            qvt_out[s] = qvt[:, s * tile:(s + 1) * tile]
        xy_out[...] = jnp.concatenate(
            [_dot(z, wxy_s[c]) for c in range(wxy_s.shape[0])], axis=-1)
        logf_out[0] = _log_sigmoid(qvft[2 * ATT_W:, :] + bf_ref[...])

    @pl.when(jnp.logical_and(i >= IN_CHUNKS, i < IN_CHUNKS + n_tiles))
    def _():
        project(h_ref[...], k_ref, qvt_ref, logf_ref, xy_ref)

    @pl.when(i == IN_CHUNKS + n_tiles)
    def _():
        project(hm_ref[...], km_ref, qvtm_ref, logfm_ref, xym_ref)


def _inproj(h, hm, g, w_t, wf_t, wxy_t, bf, layer, *, seq, tm, tile):
    n, d = h.shape
    tmeta = hm.shape[0]
    n_tiles = n // tm
    per_seq = seq // tm
    per = tm // tile
    rows = ATT_W // IN_CHUNKS
    rows_k = 2 * rows
    n_k = ATT_W // rows_k
    rows_xy = 2 * REC_W // IN_CHUNKS
    q_blk, k_blk, v_blk = 0, ATT_W // rows_k, 2 * (ATT_W // rows)

    def chunk(i):
        return jnp.minimum(i, IN_CHUNKS - 1)

    def row(i):
        return jnp.clip(i - IN_CHUNKS, 0, n_tiles - 1)

    return pl.pallas_call(
        functools.partial(_inproj_kernel, n_tiles=n_tiles),
        grid=(IN_CHUNKS + n_tiles + 1,),
        in_specs=[
            pl.BlockSpec((tm, d), lambda i: (row(i), 0)),
            _whole(hm),
            _of_layer(g, layer),
            pl.BlockSpec((None, rows, d), lambda i: (layer, q_blk + chunk(i), 0)),
            pl.BlockSpec((None, rows_k, d), lambda i: (layer, k_blk + jnp.minimum(i, n_k - 1), 0)),
            pl.BlockSpec((None, rows, d), lambda i: (layer, v_blk + chunk(i), 0)),
            _of_layer(wf_t, layer),
            pl.BlockSpec((None, rows_xy, d), lambda i: (layer, chunk(i), 0)),
            _of_layer(bf, layer),
        ],
        out_specs=[
            pl.BlockSpec((tm, ATT_W), lambda i: (row(i), 0)),
            pl.BlockSpec((per, 2 * ATT_W, tile), lambda i: (row(i), 0, 0)),
            pl.BlockSpec((1, F_ROWS, tm), lambda i: (row(i) // per_seq, 0, row(i) % per_seq)),
            pl.BlockSpec((tm, 2 * REC_W), lambda i: (row(i), 0)),
            pl.BlockSpec((tmeta, ATT_W), lambda i: (0, 0)),
            pl.BlockSpec((1, 2 * ATT_W, tmeta), lambda i: (0, 0, 0)),
            pl.BlockSpec((1, F_ROWS, tmeta), lambda i: (0, 0, 0)),
            pl.BlockSpec((tmeta, 2 * REC_W), lambda i: (0, 0)),
        ],
        out_shape=[
            jax.ShapeDtypeStruct((n, ATT_W), BF16),
            jax.ShapeDtypeStruct((n // tile, 2 * ATT_W, tile), BF16),
            jax.ShapeDtypeStruct((n // seq, F_ROWS, seq), F32),
            jax.ShapeDtypeStruct((n, 2 * REC_W), F32),
            jax.ShapeDtypeStruct((tmeta, ATT_W), BF16),
            jax.ShapeDtypeStruct((1, 2 * ATT_W, tmeta), BF16),
            jax.ShapeDtypeStruct((1, F_ROWS, tmeta), F32),
            jax.ShapeDtypeStruct((tmeta, 2 * REC_W), F32),
        ],
        scratch_shapes=[
            pltpu.VMEM((n_k, d, rows_k), BF16),
            pltpu.VMEM((2 * ATT_W + F_ROWS, d), BF16),
            pltpu.VMEM((IN_CHUNKS, d, rows_xy), BF16),
        ],
        compiler_params=pltpu.CompilerParams(
            dimension_semantics=("arbitrary",), vmem_limit_bytes=VMEM_LIMIT),
        name="inproj",
    )(h, hm, g, w_t, w_t, w_t, wf_t, wxy_t, bf)


def _neg_cumsum_cols(lf_ref, out_ref):
    r = lax.broadcasted_iota(jnp.int32, (LANES, LANES), 0)
    c = lax.broadcasted_iota(jnp.int32, (LANES, LANES), 1)
    lower = (r >= c).astype(BF16)
    pad = jnp.zeros((LANES - F_ROWS, LANES), F32)
    carry = jnp.zeros((1, LANES), F32)
    for blk in range(lf_ref.shape[2] // LANES):
        x = jnp.concatenate([lf_ref[0, :, blk * LANES:(blk + 1) * LANES], pad], axis=0)
        hi = x.astype(BF16)
        r1 = x - hi.astype(F32)
        mid = r1.astype(BF16)
        lo = (r1 - mid.astype(F32)).astype(BF16)
        cs = _dot_nt(lower, hi) + _dot_nt(lower, mid) + _dot_nt(lower, lo) + carry
        out_ref[blk * LANES:(blk + 1) * LANES, :] = cs * (-LOG2E)
        carry = cs[LANES - 1:LANES, :]


def _scores_t(kh, qth, bias_col, mask):
    st = _dot(kh, qth) + bias_col
    if mask is not None:
        st = jnp.where(mask, st, -jnp.inf)
    return st


def _pv_and_sum(vth, p):
    ones = jnp.ones((SUM_ROWS, vth.shape[1]), BF16)
    return _dot(jnp.concatenate([vth, ones], axis=0), p)


def _softmax_first(st, vth):
    m = jnp.max(st, axis=0, keepdims=True)
    return m, _pv_and_sum(vth, jnp.exp2(st - m).astype(BF16))


def _softmax_next(st, vth, m, acc):
    m_new = jnp.maximum(m, jnp.max(st, axis=0, keepdims=True))
    alpha = jnp.exp2(m - m_new)
    return m_new, alpha * acc + _pv_and_sum(vth, jnp.exp2(st - m_new).astype(BF16))


def _normalized(acc):
    return acc[:HEAD_DIM, :] / acc[HEAD_DIM:HEAD_DIM + 1, :]


def _meta_attn_kernel(k_ref, qt_ref, vt_ref, lf_ref, g_ref, o_ref, bm_ref, b_s, ot_s):
    t = k_ref.shape[0]
    _neg_cumsum_cols(lf_ref, b_s)
    row = lax.broadcasted_iota(jnp.int32, (t, LANES), 0)
    bm_ref[...] = jnp.where(row < N_META, b_s[...] - b_s[N_META - 1:N_META, :], -jnp.inf)
    key = lax.broadcasted_iota(jnp.int32, (t, t), 0)
    qry = lax.broadcasted_iota(jnp.int32, (t, t), 1)
    causal = key <= qry
    for h in range(N_HEADS):
        sl = slice(h * HEAD_DIM, (h + 1) * HEAD_DIM)
        st = _scores_t(k_ref[:, sl], qt_ref[0, sl, :], b_s[:, h:h + 1], causal)
        _, acc = _softmax_first(st, vt_ref[0, sl, :])
        ot_s[sl, :] = _normalized(acc)
    o_ref[...] = _rms(ot_s[...].T, g_ref[...]).astype(BF16)


def _meta_attn(k, qvt, logf, g, layer):
    t = k.shape[0]
    return pl.pallas_call(
        _meta_attn_kernel,
        grid=(1,),
        in_specs=[
            pl.BlockSpec((t, ATT_W), lambda i: (0, 0)),
            pl.BlockSpec((1, ATT_W, t), lambda i: (0, 0, 0)),
            pl.BlockSpec((1, ATT_W, t), lambda i: (0, 1, 0)),
            pl.BlockSpec((1, F_ROWS, t), lambda i: (0, 0, 0)),
            _of_layer(g, layer),
        ],
        out_specs=[
            pl.BlockSpec((t, ATT_W), lambda i: (0, 0)),
            pl.BlockSpec((t, LANES), lambda i: (0, 0)),
        ],
        out_shape=[
            jax.ShapeDtypeStruct((t, ATT_W), BF16),
            jax.ShapeDtypeStruct((t, LANES), F32),
        ],
        scratch_shapes=[pltpu.VMEM((t, LANES), F32), pltpu.VMEM((ATT_W, t), F32)],
        name="meta_attn",
    )(k, qvt, qvt, logf, g)


def _attn_kernel(qt_ref, k_ref, vt_ref, lf_ref, km_ref, vtm_ref, bm_ref, g_ref,
                 o_ref, b_s, m_s, acc_s, ot_s, *, tq):
    j = pl.program_id(1)

    @pl.when(j == 0)
    def _():
        _neg_cumsum_cols(lf_ref, b_s)

    heads = [slice(h * HEAD_DIM, (h + 1) * HEAD_DIM) for h in range(N_HEADS)]
    accs = [slice(h * ACC_ROWS, (h + 1) * ACC_ROWS) for h in range(N_HEADS)]

    sts = [_scores_t(km_ref[0:N_META, sl], qt_ref[0, sl, :], bm_ref[0:N_META, h:h + 1], None)
           for h, sl in enumerate(heads)]
    for h, sl in enumerate(heads):
        m, acc = _softmax_first(sts[h], vtm_ref[0, sl, 0:N_META])
        m_s[h:h + 1, :] = m
        acc_s[accs[h], :] = acc

    def tile(i, mask, finish):
        off = pl.multiple_of(i * tq, tq)

        def scores(h):
            sl = heads[h]
            return _scores_t(k_ref[pl.ds(off, tq), sl], qt_ref[0, sl, :],
                             b_s[pl.ds(off, tq), h:h + 1], mask)

        sts = [scores(0)]
        for h, sl in enumerate(heads):
            if h + 1 < N_HEADS:
                sts.append(scores(h + 1))
            m, acc = _softmax_next(sts[h], vt_ref[i, sl, :], m_s[h:h + 1, :], acc_s[accs[h], :])
            if finish:
                ot_s[sl, :] = _normalized(acc)
            else:
                m_s[h:h + 1, :] = m
                acc_s[accs[h], :] = acc

    def body(i, carry):
        tile(i, None, False)
        return carry

    lax.fori_loop(0, j, body, 0)

    key = lax.broadcasted_iota(jnp.int32, (tq, tq), 0)
    qry = lax.broadcasted_iota(jnp.int32, (tq, tq), 1)
    tile(j, key <= qry, True)
    o_ref[...] = _rms(ot_s[...].T, g_ref[...]).astype(BF16)


def _attn(k, qvt, logf, k_meta, qvt_meta, bias_meta, g, layer, *, seq, tq):
    n = k.shape[0]
    nb = n // seq
    per_seq = seq // tq
    tmeta = k_meta.shape[0]
    return pl.pallas_call(
        functools.partial(_attn_kernel, tq=tq),
        grid=(nb, per_seq),
        in_specs=[
            pl.BlockSpec((1, ATT_W, tq), lambda b, j: (b * per_seq + j, 0, 0)),
            pl.BlockSpec((seq, ATT_W), lambda b, j: (b, 0)),
            pl.BlockSpec((per_seq, ATT_W, tq), lambda b, j: (b, 1, 0)),
            pl.BlockSpec((1, F_ROWS, seq), lambda b, j: (b, 0, 0)),
            pl.BlockSpec((tmeta, ATT_W), lambda b, j: (0, 0)),
            pl.BlockSpec((1, ATT_W, tmeta), lambda b, j: (0, 1, 0)),
            _whole(bias_meta),
            _of_layer(g, layer),
        ],
        out_specs=pl.BlockSpec((tq, ATT_W), lambda b, j: (b * per_seq + j, 0)),
        out_shape=jax.ShapeDtypeStruct((n, ATT_W), BF16),
        scratch_shapes=[
            pltpu.VMEM((seq, LANES), F32),
            pltpu.VMEM((N_HEADS, tq), F32),
            pltpu.VMEM((N_HEADS * ACC_ROWS, tq), F32),
            pltpu.VMEM((ATT_W, tq), F32),
        ],
        compiler_params=pltpu.CompilerParams(
            dimension_semantics=("arbitrary", "arbitrary"), vmem_limit_bytes=VMEM_LIMIT),
        name="attn",
    )(qvt, k, qvt, logf, k_meta, qvt_meta, bias_meta, g)


def _rec_init(h0_ref, tail0_ref, ext_s, h_s, *, nb, tc):
    for b in range(nb):
        ext_s[b, tc:tc + SUBLANES, :] = tail0_ref[0]
    h_s[...] = jnp.broadcast_to(h0_ref[...], h_s.shape)


class _RecRefs:
    def __init__(self, cw_ref, cb_ref, wg_ref, bg_ref, lru_ref, g_ref, ext_s, a_s, u_s, h_s,
                 *, nb, tc):
        self.cw, self.cb, self.wg, self.bg, self.lru, self.g = (
            cw_ref, cb_ref, wg_ref, bg_ref, lru_ref, g_ref)
        self.ext_s, self.a_s, self.u_s, self.h_s = ext_s, a_s, u_s, h_s
        self.nb, self.tc = nb, tc
        self.pitch = tc + SUBLANES
        self.n_slab = REC_W // LANES


def _rec_conv_gates(rr, b, xr_b):
    tc, half = rr.tc, REC_W // 2
    rr.ext_s[b, 0:SUBLANES, :] = rr.ext_s[b, tc:tc + SUBLANES, :]
    rr.ext_s[b, SUBLANES:SUBLANES + tc, :] = xr_b
    ext = rr.ext_s[b]
    run = rr.cw[0:1, :] * ext
    for k in range(1, CONV_WIDTH):
        run = pltpu.roll(run, 1, 0) + rr.cw[k:k + 1, :] * ext
    xc = run[SUBLANES:, :] + rr.cb[...]
    xcb = xc.astype(BF16)
    g0 = _dot(xcb[:, :half], rr.wg[0])
    g1 = _dot(xcb[:, half:], rr.wg[1])
    ra = jnp.concatenate([g0[:, :half], g1[:, :half]], axis=-1) + rr.bg[0:1, :]
    rx = jnp.concatenate([g0[:, half:], g1[:, half:]], axis=-1) + rr.bg[1:2, :]
    return xc, ra, rx


def _rec_decay(rr, b, xc, ra, rx):
    tc = rr.tc
    r = jax.nn.sigmoid(ra)
    i = jax.nn.sigmoid(rx)
    log_a = RG_C * r * _log_sigmoid(rr.lru[...])
    a = jnp.exp(log_a)
    th = jnp.tanh(log_a)
    om = -2.0 * th / (1.0 - th)
    mult = jnp.where(om > 0.0, om * lax.rsqrt(om), 0.0)
    u = mult * i * xc
    lo = b * rr.pitch
    for s in range(rr.n_slab):
        rr.a_s[s, lo:lo + tc, :] = a[:, s * LANES:(s + 1) * LANES]
        rr.u_s[s, lo:lo + tc, :] = u[:, s * LANES:(s + 1) * LANES]


def _rec_scan(rr, *, unroll):
    def rows(t):
        return pl.ds(t, rr.nb, stride=rr.pitch) if rr.nb > 1 else pl.ds(t, 1)

    def scan_body(t, hs):
        out = []
        for s in range(rr.n_slab):
            hn = rr.a_s[s, rows(t), :] * hs[s] + rr.u_s[s, rows(t), :]
            rr.u_s[s, rows(t), :] = hn
            out.append(hn)
        return tuple(out)

    hs = tuple(rr.h_s[:, s * LANES:(s + 1) * LANES] for s in range(rr.n_slab))
    if unroll:
        for t in range(rr.tc):
            hs = scan_body(t, hs)
    else:
        hs = lax.fori_loop(0, rr.tc, scan_body, hs, unroll=8)
    rr.h_s[...] = jnp.concatenate(hs, axis=-1)


def _rec_output(rr, b, yr_b):
    lo = b * rr.pitch
    hb = jnp.concatenate(
        [rr.u_s[s, lo:lo + rr.tc, :] for s in range(rr.n_slab)], axis=-1)
    return _rms(hb * _gelu_tanh(yr_b), rr.g[...])


def _rec_kernel(xr_ref, yr_ref, cw_ref, cb_ref, wg_ref, bg_ref, lru_ref, g_ref,
                h0_ref, tail0_ref, out_ref, hn_ref, tailn_ref,
                ext_s, a_s, u_s, h_s, *, nb, tc, pitch, state_t):
    @pl.when(pl.program_id(0) == 0)
    def _():
        _rec_init(h0_ref, tail0_ref, ext_s, h_s, nb=nb, tc=tc)

    rr = _RecRefs(cw_ref, cb_ref, wg_ref, bg_ref, lru_ref, g_ref, ext_s, a_s, u_s, h_s,
                  nb=nb, tc=tc)
    for b in range(nb):
        _rec_decay(rr, b, *_rec_conv_gates(rr, b, xr_ref[b]))
    _rec_scan(rr, unroll=False)
    for b in range(nb):
        out_ref[b] = _rec_output(rr, b, yr_ref[b]).astype(BF16)

    n_slab = REC_W // LANES
    rows = pl.ds(state_t, nb, stride=pitch) if nb > 1 else pl.ds(state_t, 1)
    hn_ref[...] = jnp.concatenate([u_s[s, rows, :] for s in range(n_slab)], axis=-1)
    for b in range(nb):
        tailn_ref[b] = ext_s[b, state_t + 1:state_t + 1 + SUBLANES, :]


def _rec(xy, cw, cb, wg, bg, lru, g, layer, h0, tail0, *, nb, seq, tc, state_t):
    xy3 = xy.reshape(nb, seq, 2 * REC_W)
    pitch = tc + SUBLANES
    n_slab = REC_W // LANES
    out, hn, tailn = pl.pallas_call(
        functools.partial(_rec_kernel, nb=nb, tc=tc, pitch=pitch, state_t=state_t),
        grid=(seq // tc,),
        in_specs=[
            pl.BlockSpec((nb, tc, REC_W), lambda t: (0, t, 0)),
            pl.BlockSpec((nb, tc, REC_W), lambda t: (0, t, 1)),
            _of_layer(cw, layer),
            _of_layer(cb, layer),
            _of_layer(wg, layer),
            _of_layer(bg, layer),
            _of_layer(lru, layer),
            _of_layer(g, layer),
            _whole(h0),
            _whole(tail0),
        ],
        out_specs=[
            pl.BlockSpec((nb, tc, REC_W), lambda t: (0, t, 0)),
            pl.BlockSpec((nb, REC_W), lambda t: (0, 0)),
            pl.BlockSpec((nb, SUBLANES, REC_W), lambda t: (0, 0, 0)),
        ],
        out_shape=[
            jax.ShapeDtypeStruct((nb, seq, REC_W), BF16),
            jax.ShapeDtypeStruct((nb, REC_W), F32),
            jax.ShapeDtypeStruct((nb, SUBLANES, REC_W), F32),
        ],
        scratch_shapes=[
            pltpu.VMEM((nb, tc + SUBLANES, REC_W), F32),
            pltpu.VMEM((n_slab, nb * pitch, LANES), F32),
            pltpu.VMEM((n_slab, nb * pitch, LANES), F32),
            pltpu.VMEM((nb, REC_W), F32),
        ],
        compiler_params=pltpu.CompilerParams(
            dimension_semantics=("arbitrary",), vmem_limit_bytes=VMEM_LIMIT),
        name="rec",
    )(xy3, xy3, cw, cb, wg, bg, lru, g, h0, tail0)
    return out.reshape(nb * seq, REC_W), hn, tailn


def _post_kernel(h_ref, an_ref, xy_ref, hm_ref, anm_ref, rnm_ref,
                 wo_ref, g_ref, wu_ref, wd_ref, fg_ref,
                 cw_ref, cb_ref, wg_ref, bg_ref, lru_ref, grec_ref, h0_ref, tail0_ref,
                 o_ref, *rest, n_tiles, final, nb, tc):
    om_ref = None if final else rest[0]
    wo_s, wu_s, wd_s, rn_s, ext_s, a_s, u_s, hst_s = rest[-8:]
    i = pl.program_id(0)
    rows_o = wo_ref.shape[0]
    rows_d = wd_ref.shape[0]
    first_tile = W_CHUNKS + 1

    @pl.when(i < W_CHUNKS)
    def _():
        wo_s[pl.ds(pl.multiple_of(i * rows_o, rows_o), rows_o), :] = wo_ref[...].astype(BF16)
        wu_s[i] = wu_ref[...].astype(BF16)
        wd_s[pl.ds(pl.multiple_of(i * rows_d, rows_d), rows_d), :] = wd_ref[...].astype(BF16)

    @pl.when(i == 0)
    def _():
        _rec_init(h0_ref, tail0_ref, ext_s, hst_s, nb=nb, tc=tc)

    rr = _RecRefs(cw_ref, cb_ref, wg_ref, bg_ref, lru_ref, grec_ref, ext_s, a_s, u_s, hst_s,
                  nb=nb, tc=tc)

    def rec_gates(b):
        _rec_decay(rr, b, *_rec_conv_gates(rr, b, xy_ref[b, :, 0:REC_W]))

    def rec_output(b, slot):
        out = _rec_output(rr, b, xy_ref[b, :, REC_W:2 * REC_W])
        rn_s[slot, b * tc:(b + 1) * tc, :] = out.astype(BF16)

    @pl.when(i == W_CHUNKS)
    def _():
        for b in range(nb):
            rec_gates(b)
        _rec_scan(rr, unroll=True)
        for b in range(nb):
            rec_output(b, 0)

    def mix_and_mlp(h_tile, an, rn, between=None):
        h1 = h_tile + _dot(an, wo_s[0:ATT_W, :]) + _dot(rn, wo_s[ATT_W:ATT_W + REC_W, :])
        z = _rms(h1, g_ref[...]).astype(BF16)
        down = None
        for c in range(W_CHUNKS):
            if between is not None:
                between(c)
            u = jnp.maximum(_dot(z, wu_s[c]), 0.0)
            part = _dot((u * u).astype(BF16), wd_s[c * rows_d:(c + 1) * rows_d, :])
            down = part if down is None else down + part
        return h1 + down

    @pl.when(jnp.logical_and(i >= first_tile, i < first_tile + n_tiles))
    def _():
        s = i - first_tile
        nxt = lax.rem(s + 1, 2)
        half = W_CHUNKS // 2

        def next_tile_rec(c):
            if c == half:
                _rec_scan(rr, unroll=True)
            for b in range((c % half) * nb // half, (c % half + 1) * nb // half):
                if c < half:
                    rec_gates(b)
                else:
                    rec_output(b, nxt)

        rn = rn_s[lax.rem(s, 2)]
        h_tile = jnp.concatenate([h_ref[b] for b in range(nb)], axis=0)
        an = jnp.concatenate([an_ref[b] for b in range(nb)], axis=0)
        acc = mix_and_mlp(h_tile, an, rn, next_tile_rec)
        if final:
            acc = _rms(acc, fg_ref[...])
        for b in range(nb):
            o_ref[b] = acc[b * tc:(b + 1) * tc, :]

    if not final:
        @pl.when(i == first_tile + n_tiles)
        def _():
            om_ref[...] = mix_and_mlp(hm_ref[...], anm_ref[...], rnm_ref[...])


def _post(h, an, xy, hm, an_m, rn_m, w_out, w_up, w_down, g, fg,
          cw, cb, wg, bg, lru, g_rec, h0, tail0, layer, *, nb, seq, tc, final):
    n, d = h.shape
    n_tiles = seq // tc
    d_mix, d_ff = w_out.shape[1], w_up.shape[2]
    rows_o, cols_u, rows_d = d_mix // W_CHUNKS, d_ff // W_CHUNKS, d_ff // W_CHUNKS
    pitch = tc + SUBLANES
    n_slab = REC_W // LANES
    h3 = h.reshape(nb, seq, d)
    an3 = an.reshape(nb, seq, ATT_W)
    xy3 = xy.reshape(nb, seq, 2 * REC_W)

    def tile(i):
        return (0, jnp.clip(i - (W_CHUNKS + 1), 0, n_tiles - 1), 0)

    def next_tile(i):
        return (0, jnp.clip(i - W_CHUNKS, 0, n_tiles - 1), 0)

    def chunk(i):
        return jnp.minimum(i, W_CHUNKS - 1)

    out_specs = [pl.BlockSpec((nb, tc, d), tile)]
    out_shape = [jax.ShapeDtypeStruct((nb, seq, d), F32)]
    if not final:
        out_specs.append(pl.BlockSpec(hm.shape, lambda i: (0, 0)))
        out_shape.append(jax.ShapeDtypeStruct(hm.shape, F32))

    outs = pl.pallas_call(
        functools.partial(_post_kernel, n_tiles=n_tiles, final=final, nb=nb, tc=tc),
        grid=(W_CHUNKS + 1 + n_tiles + (0 if final else 1),),
        in_specs=[
            pl.BlockSpec((nb, tc, d), tile),
            pl.BlockSpec((nb, tc, ATT_W), tile),
            pl.BlockSpec((nb, tc, 2 * REC_W), next_tile),
            _whole(hm),
            _whole(an_m),
            _whole(rn_m),
            pl.BlockSpec((None, rows_o, d), lambda i: (layer, chunk(i), 0)),
            _of_layer(g, layer),
            pl.BlockSpec((None, d, cols_u), lambda i: (layer, 0, chunk(i))),
            pl.BlockSpec((None, rows_d, d), lambda i: (layer, chunk(i), 0)),
            _whole(fg),
            _of_layer(cw, layer),
            _of_layer(cb, layer),
            _of_layer(wg, layer),
            _of_layer(bg, layer),
            _of_layer(lru, layer),
            _of_layer(g_rec, layer),
            _whole(h0),
            _whole(tail0),
        ],
        out_specs=out_specs,
        out_shape=out_shape,
        scratch_shapes=[
            pltpu.VMEM((d_mix, d), BF16),
            pltpu.VMEM((W_CHUNKS, d, cols_u), BF16),
            pltpu.VMEM((d_ff, d), BF16),
            pltpu.VMEM((2, nb * tc, REC_W), BF16),
            pltpu.VMEM((nb, tc + SUBLANES, REC_W), F32),
            pltpu.VMEM((n_slab, nb * pitch, LANES), F32),
            pltpu.VMEM((n_slab, nb * pitch, LANES), F32),
            pltpu.VMEM((nb, REC_W), F32),
        ],
        compiler_params=pltpu.CompilerParams(
            dimension_semantics=("arbitrary",), vmem_limit_bytes=VMEM_LIMIT),
        name="post",
    )(h3, an3, xy3, hm, an_m, rn_m, w_out, g, w_up, w_down, fg,
      cw, cb, wg, bg, lru, g_rec, h0, tail0)
    h_new = outs[0].reshape(n, d)
    return (h_new, hm) if final else (h_new, outs[1])


def _block_diag_gates(w_a, w_x):
    depth, nblk, blk, _ = w_a.shape
    per = nblk // 2
    eye = jnp.eye(per, dtype=w_a.dtype)

    def bd(w):
        w5 = w.reshape(depth, 2, per, blk, blk)
        full = jnp.einsum('lgpij,pq->lgpiqj', w5, eye)
        return full.reshape(depth, 2, per * blk, per * blk)

    return jnp.concatenate([bd(w_a), bd(w_x)], axis=-1).astype(BF16)


def kernel(x, meta, attn_norm_g, w_in, b_f, conv_w, conv_b, w_gate_a, b_gate_a,
           w_gate_x, b_gate_x, lru_L, attn_out_g, rec_out_g, w_out, mlp_norm_g,
           w_up, w_down, final_g):
    nb, seq, d = x.shape
    depth = w_in.shape[0]
    c_f, c_x = 3 * ATT_W, 3 * ATT_W + N_HEADS

    w_t = jnp.swapaxes(w_in, 1, 2)
    wf_t = jnp.pad(w_t[:, c_f:c_x, :], ((0, 0), (0, F_ROWS - N_HEADS), (0, 0)))
    wxy_t = w_t[:, c_x:, :]
    bf = jnp.pad(b_f, ((0, 0), (0, F_ROWS - N_HEADS))).reshape(depth, F_ROWS, 1)
    g_in = attn_norm_g.reshape(depth, 1, d)
    cb = conv_b.reshape(depth, 1, REC_W)
    wg = _block_diag_gates(w_gate_a, w_gate_x)
    bg = jnp.stack([b_gate_a, b_gate_x], axis=1)
    lru = lru_L.reshape(depth, 1, REC_W)
    g_att = attn_out_g.reshape(depth, 1, ATT_W)
    g_rec = rec_out_g.reshape(depth, 1, REC_W)
    g_mlp = mlp_norm_g.reshape(depth, 1, d)
    fg = final_g.reshape(1, d)
    h0_meta = jnp.zeros((1, REC_W), F32)
    tail0_meta = jnp.zeros((1, SUBLANES, REC_W), F32)

    h = x.reshape(nb * seq, d)
    hm = jnp.pad(meta, ((0, META_PAD - N_META), (0, 0)))

    for l in range(depth):
        last = l == depth - 1
        k, qvt, logf, xy, k_m, qvt_m, logf_m, xy_m = _inproj(
            h, hm, g_in, w_t, wf_t, wxy_t, bf, l, seq=seq, tm=ROW_TILE, tile=ATT_TILE)

        an_m, bias_m = _meta_attn(k_m, qvt_m, logf_m, g_att, l)
        rn_m, h_state, x_tail = _rec(
            xy_m, conv_w, cb, wg, bg, lru, g_rec, l, h0_meta, tail0_meta,
            nb=1, seq=META_PAD, tc=META_PAD, state_t=N_META - 1)

        an = _attn(k, qvt, logf, k_m, qvt_m, bias_m, g_att, l, seq=seq, tq=ATT_TILE)
        h, hm = _post(h, an, xy, hm, an_m, rn_m, w_out, w_up, w_down, g_mlp, fg,
                      conv_w, cb, wg, bg, lru, g_rec, h_state, x_tail, l,
                      nb=nb, seq=seq, tc=TIME_TILE, final=last)

    return h.reshape(nb, seq, d)
```

```python
import functools
import math

import jax
import jax.numpy as jnp
from jax import lax
from jax.experimental import pallas as pl
from jax.experimental.pallas import tpu as pltpu

F32 = jnp.float32
BF16 = jnp.bfloat16

N_META = 16
META_PAD = 128
N_HEADS = 8
HEAD_DIM = 64
ATT_W = N_HEADS * HEAD_DIM
REC_W = 512
CONV_WIDTH = 4
RG_C = 8.0
NORM_EPS = 1e-6
LANES = 128
SUBLANES = 8
LOG2E = math.log2(math.e)
ATT_TILE = 512
ROW_TILE = 1024
TIME_TILE = 64
W_CHUNKS = 8
IN_CHUNKS = 4
F_ROWS = 16
SUM_ROWS = 16
ACC_ROWS = HEAD_DIM + SUM_ROWS
VMEM_LIMIT = 56 * 1024 * 1024


def _rms(x, g):
    ms = jnp.mean(x * x, axis=-1, keepdims=True)
    return x * lax.rsqrt(ms + NORM_EPS) * g


def _log_sigmoid(x):
    return jnp.minimum(x, 0.0) - jnp.log1p(jnp.exp(-jnp.abs(x)))


def _gelu_tanh(x):
    c = math.sqrt(2.0 / math.pi)
    return x * (0.5 * (1.0 + jnp.tanh(c * (x + 0.044715 * (x * x * x)))))


def _dot(a, b):
    return jnp.dot(a, b, preferred_element_type=F32)


def _dot_nt(a, b):
    return lax.dot_general(a, b, (((1,), (1,)), ((), ())), preferred_element_type=F32)


def _whole(arr):
    nd = arr.ndim
    return pl.BlockSpec(arr.shape, lambda *_: (0,) * nd, pipeline_mode=pl.Buffered(1))


def _of_layer(arr, layer):
    nd = arr.ndim - 1
    return pl.BlockSpec((None,) + arr.shape[1:], lambda *_: (layer,) + (0,) * nd,
                        pipeline_mode=pl.Buffered(1))


def _inproj_kernel(h_ref, hm_ref, g_ref, wq_ref, wk_ref, wv_ref, wf_ref, wxy_ref, bf_ref,
                   k_ref, qvt_ref, logf_ref, xy_ref, km_ref, qvtm_ref, logfm_ref, xym_ref,
                   wk_s, wqvt_s, wxy_s, *, n_tiles):
    i = pl.program_id(0)

    @pl.when(i < IN_CHUNKS)
    def _():
        rows = wq_ref.shape[0]
        r0 = pl.multiple_of(i * rows, rows)
        wq = wq_ref[...] * (LOG2E / math.sqrt(HEAD_DIM))
        wqvt_s[pl.ds(r0, rows), :] = wq.astype(BF16)
        wqvt_s[pl.ds(ATT_W + r0, rows), :] = wv_ref[...].astype(BF16)
        wxy_s[i] = wxy_ref[...].T.astype(BF16)

    @pl.when(i < wk_s.shape[0])
    def _():
        wk_s[i] = wk_ref[...].T.astype(BF16)

    @pl.when(i == 0)
    def _():
        wqvt_s[2 * ATT_W:2 * ATT_W + F_ROWS, :] = wf_ref[...].astype(BF16)

    def project(h_tile, k_out, qvt_out, logf_out, xy_out):
        z = _rms(h_tile, g_ref[...]).astype(BF16)
        k_out[...] = jnp.concatenate(
            [_dot(z, wk_s[c]) for c in range(wk_s.shape[0])], axis=-1).astype(BF16)
        qvft = _dot_nt(wqvt_s[...], z)
        qvt = qvft[:2 * ATT_W, :].astype(BF16)
        tile = qvt_out.shape[2]
        for s in range(qvt_out.shape[0]):
            qvt_out[s] = qvt[:, s * tile:(s + 1) * tile]
        xy_out[...] = jnp.concatenate(
            [_dot(z, wxy_s[c]) for c in range(wxy_s.shape[0])], axis=-1)
        logf_out[0] = _log_sigmoid(qvft[2 * ATT_W:, :] + bf_ref[...])

    @pl.when(jnp.logical_and(i >= IN_CHUNKS, i < IN_CHUNKS + n_tiles))
    def _():
        project(h_ref[...], k_ref, qvt_ref, logf_ref, xy_ref)

    @pl.when(i == IN_CHUNKS + n_tiles)
    def _():
        project(hm_ref[...], km_ref, qvtm_ref, logfm_ref, xym_ref)


def _inproj(h, hm, g, w_t, wf_t, wxy_t, bf, layer, *, seq, tm, tile):
    n, d = h.shape
    tmeta = hm.shape[0]
    n_tiles = n // tm
    per_seq = seq // tm
    per = tm // tile
    rows = ATT_W // IN_CHUNKS
    rows_k = 2 * rows
    n_k = ATT_W // rows_k
    rows_xy = 2 * REC_W // IN_CHUNKS
    q_blk, k_blk, v_blk = 0, ATT_W // rows_k, 2 * (ATT_W // rows)

    def chunk(i):
        return jnp.minimum(i, IN_CHUNKS - 1)

    def row(i):
        return jnp.clip(i - IN_CHUNKS, 0, n_tiles - 1)

    return pl.pallas_call(
        functools.partial(_inproj_kernel, n_tiles=n_tiles),
        grid=(IN_CHUNKS + n_tiles + 1,),
        in_specs=[
            pl.BlockSpec((tm, d), lambda i: (row(i), 0)),
            _whole(hm),
            _of_layer(g, layer),
            pl.BlockSpec((None, rows, d), lambda i: (layer, q_blk + chunk(i), 0)),
            pl.BlockSpec((None, rows_k, d), lambda i: (layer, k_blk + jnp.minimum(i, n_k - 1), 0)),
            pl.BlockSpec((None, rows, d), lambda i: (layer, v_blk + chunk(i), 0)),
            _of_layer(wf_t, layer),
            pl.BlockSpec((None, rows_xy, d), lambda i: (layer, chunk(i), 0)),
            _of_layer(bf, layer),
        ],
        out_specs=[
            pl.BlockSpec((tm, ATT_W), lambda i: (row(i), 0)),
            pl.BlockSpec((per, 2 * ATT_W, tile), lambda i: (row(i), 0, 0)),
            pl.BlockSpec((1, F_ROWS, tm), lambda i: (row(i) // per_seq, 0, row(i) % per_seq)),
            pl.BlockSpec((tm, 2 * REC_W), lambda i: (row(i), 0)),
            pl.BlockSpec((tmeta, ATT_W), lambda i: (0, 0)),
            pl.BlockSpec((1, 2 * ATT_W, tmeta), lambda i: (0, 0, 0)),
            pl.BlockSpec((1, F_ROWS, tmeta), lambda i: (0, 0, 0)),
            pl.BlockSpec((tmeta, 2 * REC_W), lambda i: (0, 0)),
        ],
        out_shape=[
            jax.ShapeDtypeStruct((n, ATT_W), BF16),
            jax.ShapeDtypeStruct((n // tile, 2 * ATT_W, tile), BF16),
            jax.ShapeDtypeStruct((n // seq, F_ROWS, seq), F32),
            jax.ShapeDtypeStruct((n, 2 * REC_W), F32),
            jax.ShapeDtypeStruct((tmeta, ATT_W), BF16),
            jax.ShapeDtypeStruct((1, 2 * ATT_W, tmeta), BF16),
            jax.ShapeDtypeStruct((1, F_ROWS, tmeta), F32),
            jax.ShapeDtypeStruct((tmeta, 2 * REC_W), F32),
        ],
        scratch_shapes=[
            pltpu.VMEM((n_k, d, rows_k), BF16),
            pltpu.VMEM((2 * ATT_W + F_ROWS, d), BF16),
            pltpu.VMEM((IN_CHUNKS, d, rows_xy), BF16),
        ],
        compiler_params=pltpu.CompilerParams(
            dimension_semantics=("arbitrary",), vmem_limit_bytes=VMEM_LIMIT),
        name="inproj",
    )(h, hm, g, w_t, w_t, w_t, wf_t, wxy_t, bf)


def _neg_cumsum_cols(lf_ref, out_ref):
    r = lax.broadcasted_iota(jnp.int32, (LANES, LANES), 0)
    c = lax.broadcasted_iota(jnp.int32, (LANES, LANES), 1)
    lower = (r >= c).astype(BF16)
    pad = jnp.zeros((LANES - F_ROWS, LANES), F32)
    carry = jnp.zeros((1, LANES), F32)
    for blk in range(lf_ref.shape[2] // LANES):
        x = jnp.concatenate([lf_ref[0, :, blk * LANES:(blk + 1) * LANES], pad], axis=0)
        hi = x.astype(BF16)
        r1 = x - hi.astype(F32)
        mid = r1.astype(BF16)
        lo = (r1 - mid.astype(F32)).astype(BF16)
        cs = _dot_nt(lower, hi) + _dot_nt(lower, mid) + _dot_nt(lower, lo) + carry
        out_ref[blk * LANES:(blk + 1) * LANES, :] = cs * (-LOG2E)
        carry = cs[LANES - 1:LANES, :]


def _scores_t(kh, qth, bias_col, mask):
    st = _dot(kh, qth) + bias_col
    if mask is not None:
        st = jnp.where(mask, st, -jnp.inf)
    return st


def _pv_and_sum(vth, p):
    ones = jnp.ones((SUM_ROWS, vth.shape[1]), BF16)
    return _dot(jnp.concatenate([vth, ones], axis=0), p)


def _softmax_first(st, vth):
    m = jnp.max(st, axis=0, keepdims=True)
    return m, _pv_and_sum(vth, jnp.exp2(st - m).astype(BF16))


def _softmax_next(st, vth, m, acc):
    m_new = jnp.maximum(m, jnp.max(st, axis=0, keepdims=True))
    alpha = jnp.exp2(m - m_new)
    return m_new, alpha * acc + _pv_and_sum(vth, jnp.exp2(st - m_new).astype(BF16))


def _normalized(acc):
    return acc[:HEAD_DIM, :] / acc[HEAD_DIM:HEAD_DIM + 1, :]


def _meta_attn_kernel(k_ref, qt_ref, vt_ref, lf_ref, g_ref, o_ref, bm_ref, b_s, ot_s):
    t = k_ref.shape[0]
    _neg_cumsum_cols(lf_ref, b_s)
    row = lax.broadcasted_iota(jnp.int32, (t, LANES), 0)
    bm_ref[...] = jnp.where(row < N_META, b_s[...] - b_s[N_META - 1:N_META, :], -jnp.inf)
    key = lax.broadcasted_iota(jnp.int32, (t, t), 0)
    qry = lax.broadcasted_iota(jnp.int32, (t, t), 1)
    causal = key <= qry
    for h in range(N_HEADS):
        sl = slice(h * HEAD_DIM, (h + 1) * HEAD_DIM)
        st = _scores_t(k_ref[:, sl], qt_ref[0, sl, :], b_s[:, h:h + 1], causal)
        _, acc = _softmax_first(st, vt_ref[0, sl, :])
        ot_s[sl, :] = _normalized(acc)
    o_ref[...] = _rms(ot_s[...].T, g_ref[...]).astype(BF16)


def _meta_attn(k, qvt, logf, g, layer):
    t = k.shape[0]
    return pl.pallas_call(
        _meta_attn_kernel,
        grid=(1,),
        in_specs=[
            pl.BlockSpec((t, ATT_W), lambda i: (0, 0)),
            pl.BlockSpec((1, ATT_W, t), lambda i: (0, 0, 0)),
            pl.BlockSpec((1, ATT_W, t), lambda i: (0, 1, 0)),
            pl.BlockSpec((1, F_ROWS, t), lambda i: (0, 0, 0)),
            _of_layer(g, layer),
        ],
        out_specs=[
            pl.BlockSpec((t, ATT_W), lambda i: (0, 0)),
            pl.BlockSpec((t, LANES), lambda i: (0, 0)),
        ],
        out_shape=[
            jax.ShapeDtypeStruct((t, ATT_W), BF16),
            jax.ShapeDtypeStruct((t, LANES), F32),
        ],
        scratch_shapes=[pltpu.VMEM((t, LANES), F32), pltpu.VMEM((ATT_W, t), F32)],
        name="meta_attn",
    )(k, qvt, qvt, logf, g)


def _attn_kernel(qt_ref, k_ref, vt_ref, lf_ref, km_ref, vtm_ref, bm_ref, g_ref,
                 o_ref, b_s, m_s, acc_s, ot_s, st0_s, *, tq):
    j = pl.program_id(1)

    @pl.when(j == 0)
    def _():
        _neg_cumsum_cols(lf_ref, b_s)

    heads = [slice(h * HEAD_DIM, (h + 1) * HEAD_DIM) for h in range(N_HEADS)]
    accs = [slice(h * ACC_ROWS, (h + 1) * ACC_ROWS) for h in range(N_HEADS)]

    sts = [_scores_t(km_ref[0:N_META, sl], qt_ref[0, sl, :], bm_ref[0:N_META, h:h + 1], None)
           for h, sl in enumerate(heads)]
    for h, sl in enumerate(heads):
        m, acc = _softmax_first(sts[h], vtm_ref[0, sl, 0:N_META])
        m_s[h:h + 1, :] = m
        acc_s[accs[h], :] = acc

    def scores(h, i):
        off = pl.multiple_of(i * tq, tq)
        sl = heads[h]
        return _scores_t(k_ref[pl.ds(off, tq), sl], qt_ref[0, sl, :],
                         b_s[pl.ds(off, tq), h:h + 1], None)

    def tile(i, mask, finish):
        sts = [st0_s[...]]
        for h, sl in enumerate(heads):
            if h + 1 < N_HEADS:
                sts.append(scores(h + 1, i))
            elif not finish:
                st0_s[...] = scores(0, i + 1)
            st = sts[h] if mask is None else jnp.where(mask, sts[h], -jnp.inf)
            m, acc = _softmax_next(st, vt_ref[i, sl, :], m_s[h:h + 1, :], acc_s[accs[h], :])
            if finish:
                ot_s[sl, :] = _normalized(acc)
            else:
                m_s[h:h + 1, :] = m
                acc_s[accs[h], :] = acc

    st0_s[...] = scores(0, 0)

    def body(i, carry):
        tile(i, None, False)
        return carry

    lax.fori_loop(0, j, body, 0)

    key = lax.broadcasted_iota(jnp.int32, (tq, tq), 0)
    qry = lax.broadcasted_iota(jnp.int32, (tq, tq), 1)
    tile(j, key <= qry, True)
    o_ref[...] = _rms(ot_s[...].T, g_ref[...]).astype(BF16)


def _attn(k, qvt, logf, k_meta, qvt_meta, bias_meta, g, layer, *, seq, tq):
    n = k.shape[0]
    nb = n // seq
    per_seq = seq // tq
    tmeta = k_meta.shape[0]
    return pl.pallas_call(
        functools.partial(_attn_kernel, tq=tq),
        grid=(nb, per_seq),
        in_specs=[
            pl.BlockSpec((1, ATT_W, tq), lambda b, j: (b * per_seq + j, 0, 0)),
            pl.BlockSpec((seq, ATT_W), lambda b, j: (b, 0)),
            pl.BlockSpec((per_seq, ATT_W, tq), lambda b, j: (b, 1, 0)),
            pl.BlockSpec((1, F_ROWS, seq), lambda b, j: (b, 0, 0)),
            pl.BlockSpec((tmeta, ATT_W), lambda b, j: (0, 0)),
            pl.BlockSpec((1, ATT_W, tmeta), lambda b, j: (0, 1, 0)),
            _whole(bias_meta),
            _of_layer(g, layer),
        ],
        out_specs=pl.BlockSpec((tq, ATT_W), lambda b, j: (b * per_seq + j, 0)),
        out_shape=jax.ShapeDtypeStruct((n, ATT_W), BF16),
        scratch_shapes=[
            pltpu.VMEM((seq, LANES), F32),
            pltpu.VMEM((N_HEADS, tq), F32),
            pltpu.VMEM((N_HEADS * ACC_ROWS, tq), F32),
            pltpu.VMEM((ATT_W, tq), F32),
            pltpu.VMEM((tq, tq), F32),
        ],
        compiler_params=pltpu.CompilerParams(
            dimension_semantics=("arbitrary", "arbitrary"), vmem_limit_bytes=VMEM_LIMIT),
        name="attn",
    )(qvt, k, qvt, logf, k_meta, qvt_meta, bias_meta, g)


def _rec_init(h0_ref, tail0_ref, ext_s, h_s, *, nb, tc):
    for b in range(nb):
        ext_s[b, tc:tc + SUBLANES, :] = tail0_ref[0]
    h_s[...] = jnp.broadcast_to(h0_ref[...], h_s.shape)


class _RecRefs:
    def __init__(self, cw_ref, cb_ref, wg_ref, bg_ref, lru_ref, g_ref, ext_s, a_s, u_s, h_s,
                 *, nb, tc):
        self.cw, self.cb, self.wg, self.bg, self.lru, self.g = (
            cw_ref, cb_ref, wg_ref, bg_ref, lru_ref, g_ref)
        self.ext_s, self.a_s, self.u_s, self.h_s = ext_s, a_s, u_s, h_s
        self.nb, self.tc = nb, tc
        self.pitch = tc + SUBLANES
        self.n_slab = REC_W // LANES


def _rec_conv_gates(rr, b, xr_b):
    tc, half = rr.tc, REC_W // 2
    rr.ext_s[b, 0:SUBLANES, :] = rr.ext_s[b, tc:tc + SUBLANES, :]
    rr.ext_s[b, SUBLANES:SUBLANES + tc, :] = xr_b
    ext = rr.ext_s[b]
    run = rr.cw[0:1, :] * ext
    for k in range(1, CONV_WIDTH):
        run = pltpu.roll(run, 1, 0) + rr.cw[k:k + 1, :] * ext
    xc = run[SUBLANES:, :] + rr.cb[...]
    xcb = xc.astype(BF16)
    g0 = _dot(xcb[:, :half], rr.wg[0])
    g1 = _dot(xcb[:, half:], rr.wg[1])
    ra = jnp.concatenate([g0[:, :half], g1[:, :half]], axis=-1) + rr.bg[0:1, :]
    rx = jnp.concatenate([g0[:, half:], g1[:, half:]], axis=-1) + rr.bg[1:2, :]
    return xc, ra, rx


def _rec_decay(rr, b, xc, ra, rx):
    tc = rr.tc
    r = jax.nn.sigmoid(ra)
    i = jax.nn.sigmoid(rx)
    log_a = RG_C * r * _log_sigmoid(rr.lru[...])
    a = jnp.exp(log_a)
    th = jnp.tanh(log_a)
    om = -2.0 * th / (1.0 - th)
    mult = jnp.where(om > 0.0, om * lax.rsqrt(om), 0.0)
    u = mult * i * xc
    lo = b * rr.pitch
    for s in range(rr.n_slab):
        rr.a_s[s, lo:lo + tc, :] = a[:, s * LANES:(s + 1) * LANES]
        rr.u_s[s, lo:lo + tc, :] = u[:, s * LANES:(s + 1) * LANES]


def _rec_scan(rr, *, unroll):
    def rows(t):
        return pl.ds(t, rr.nb, stride=rr.pitch) if rr.nb > 1 else pl.ds(t, 1)

    def scan_body(t, hs):
        out = []
        for s in range(rr.n_slab):
            hn = rr.a_s[s, rows(t), :] * hs[s] + rr.u_s[s, rows(t), :]
            rr.u_s[s, rows(t), :] = hn
            out.append(hn)
        return tuple(out)

    hs = tuple(rr.h_s[:, s * LANES:(s + 1) * LANES] for s in range(rr.n_slab))
    if unroll:
        for t in range(rr.tc):
            hs = scan_body(t, hs)
    else:
        hs = lax.fori_loop(0, rr.tc, scan_body, hs, unroll=8)
    rr.h_s[...] = jnp.concatenate(hs, axis=-1)


def _rec_output(rr, b, yr_b):
    lo = b * rr.pitch
    hb = jnp.concatenate(
        [rr.u_s[s, lo:lo + rr.tc, :] for s in range(rr.n_slab)], axis=-1)
    return _rms(hb * _gelu_tanh(yr_b), rr.g[...])


def _rec_kernel(xr_ref, yr_ref, cw_ref, cb_ref, wg_ref, bg_ref, lru_ref, g_ref,
                h0_ref, tail0_ref, out_ref, hn_ref, tailn_ref,
                ext_s, a_s, u_s, h_s, *, nb, tc, pitch, state_t):
    @pl.when(pl.program_id(0) == 0)
    def _():
        _rec_init(h0_ref, tail0_ref, ext_s, h_s, nb=nb, tc=tc)

    rr = _RecRefs(cw_ref, cb_ref, wg_ref, bg_ref, lru_ref, g_ref, ext_s, a_s, u_s, h_s,
                  nb=nb, tc=tc)
    for b in range(nb):
        _rec_decay(rr, b, *_rec_conv_gates(rr, b, xr_ref[b]))
    _rec_scan(rr, unroll=False)
    for b in range(nb):
        out_ref[b] = _rec_output(rr, b, yr_ref[b]).astype(BF16)

    n_slab = REC_W // LANES
    rows = pl.ds(state_t, nb, stride=pitch) if nb > 1 else pl.ds(state_t, 1)
    hn_ref[...] = jnp.concatenate([u_s[s, rows, :] for s in range(n_slab)], axis=-1)
    for b in range(nb):
        tailn_ref[b] = ext_s[b, state_t + 1:state_t + 1 + SUBLANES, :]


def _rec(xy, cw, cb, wg, bg, lru, g, layer, h0, tail0, *, nb, seq, tc, state_t):
    xy3 = xy.reshape(nb, seq, 2 * REC_W)
    pitch = tc + SUBLANES
    n_slab = REC_W // LANES
    out, hn, tailn = pl.pallas_call(
        functools.partial(_rec_kernel, nb=nb, tc=tc, pitch=pitch, state_t=state_t),
        grid=(seq // tc,),
        in_specs=[
            pl.BlockSpec((nb, tc, REC_W), lambda t: (0, t, 0)),
            pl.BlockSpec((nb, tc, REC_W), lambda t: (0, t, 1)),
            _of_layer(cw, layer),
            _of_layer(cb, layer),
            _of_layer(wg, layer),
            _of_layer(bg, layer),
            _of_layer(lru, layer),
            _of_layer(g, layer),
            _whole(h0),
            _whole(tail0),
        ],
        out_specs=[
            pl.BlockSpec((nb, tc, REC_W), lambda t: (0, t, 0)),
            pl.BlockSpec((nb, REC_W), lambda t: (0, 0)),
            pl.BlockSpec((nb, SUBLANES, REC_W), lambda t: (0, 0, 0)),
        ],
        out_shape=[
            jax.ShapeDtypeStruct((nb, seq, REC_W), BF16),
            jax.ShapeDtypeStruct((nb, REC_W), F32),
            jax.ShapeDtypeStruct((nb, SUBLANES, REC_W), F32),
        ],
        scratch_shapes=[
            pltpu.VMEM((nb, tc + SUBLANES, REC_W), F32),
            pltpu.VMEM((n_slab, nb * pitch, LANES), F32),
            pltpu.VMEM((n_slab, nb * pitch, LANES), F32),
            pltpu.VMEM((nb, REC_W), F32),
        ],
        compiler_params=pltpu.CompilerParams(
            dimension_semantics=("arbitrary",), vmem_limit_bytes=VMEM_LIMIT),
        name="rec",
    )(xy3, xy3, cw, cb, wg, bg, lru, g, h0, tail0)
    return out.reshape(nb * seq, REC_W), hn, tailn


def _post_kernel(h_ref, an_ref, xy_ref, hm_ref, anm_ref, rnm_ref,
                 wo_ref, g_ref, wu_ref, wd_ref, fg_ref,
                 cw_ref, cb_ref, wg_ref, bg_ref, lru_ref, grec_ref, h0_ref, tail0_ref,
                 o_ref, *rest, n_tiles, final, nb, tc):
    om_ref = None if final else rest[0]
    wo_s, wu_s, wd_s, rn_s, ext_s, a_s, u_s, hst_s = rest[-8:]
    i = pl.program_id(0)
    rows_o = wo_ref.shape[0]
    rows_d = wd_ref.shape[0]
    first_tile = W_CHUNKS + 1

    @pl.when(i < W_CHUNKS)
    def _():
        wo_s[pl.ds(pl.multiple_of(i * rows_o, rows_o), rows_o), :] = wo_ref[...].astype(BF16)
        wu_s[i] = wu_ref[...].astype(BF16)
        wd_s[pl.ds(pl.multiple_of(i * rows_d, rows_d), rows_d), :] = wd_ref[...].astype(BF16)

    @pl.when(i == 0)
    def _():
        _rec_init(h0_ref, tail0_ref, ext_s, hst_s, nb=nb, tc=tc)

    rr = _RecRefs(cw_ref, cb_ref, wg_ref, bg_ref, lru_ref, grec_ref, ext_s, a_s, u_s, hst_s,
                  nb=nb, tc=tc)

    def rec_gates(b):
        _rec_decay(rr, b, *_rec_conv_gates(rr, b, xy_ref[b, :, 0:REC_W]))

    def rec_output(b, slot):
        out = _rec_output(rr, b, xy_ref[b, :, REC_W:2 * REC_W])
        rn_s[slot, b * tc:(b + 1) * tc, :] = out.astype(BF16)

    @pl.when(i == W_CHUNKS)
    def _():
        for b in range(nb):
            rec_gates(b)
        _rec_scan(rr, unroll=True)
        for b in range(nb):
            rec_output(b, 0)

    def mix_and_mlp(h_tile, an, rn, between=None):
        h1 = h_tile + _dot(an, wo_s[0:ATT_W, :]) + _dot(rn, wo_s[ATT_W:ATT_W + REC_W, :])
        z = _rms(h1, g_ref[...]).astype(BF16)
        down = None
        for c in range(W_CHUNKS):
            if between is not None:
                between(c)
            u = jnp.maximum(_dot(z, wu_s[c]), 0.0)
            part = _dot((u * u).astype(BF16), wd_s[c * rows_d:(c + 1) * rows_d, :])
            down = part if down is None else down + part
        return h1 + down

    @pl.when(jnp.logical_and(i >= first_tile, i < first_tile + n_tiles))
    def _():
        s = i - first_tile
        nxt = lax.rem(s + 1, 2)
        half = W_CHUNKS // 2

        def next_tile_rec(c):
            if c == half:
                _rec_scan(rr, unroll=True)
            for b in range((c % half) * nb // half, (c % half + 1) * nb // half):
                if c < half:
                    rec_gates(b)
                else:
                    rec_output(b, nxt)

        rn = rn_s[lax.rem(s, 2)]
        h_tile = jnp.concatenate([h_ref[b] for b in range(nb)], axis=0)
        an = jnp.concatenate([an_ref[b] for b in range(nb)], axis=0)
        acc = mix_and_mlp(h_tile, an, rn, next_tile_rec)
        if final:
            acc = _rms(acc, fg_ref[...])
        for b in range(nb):
            o_ref[b] = acc[b * tc:(b + 1) * tc, :]

    if not final:
        @pl.when(i == first_tile + n_tiles)
        def _():
            om_ref[...] = mix_and_mlp(hm_ref[...], anm_ref[...], rnm_ref[...])


def _post(h, an, xy, hm, an_m, rn_m, w_out, w_up, w_down, g, fg,
          cw, cb, wg, bg, lru, g_rec, h0, tail0, layer, *, nb, seq, tc, final):
    n, d = h.shape
    n_tiles = seq // tc
    d_mix, d_ff = w_out.shape[1], w_up.shape[2]
    rows_o, cols_u, rows_d = d_mix // W_CHUNKS, d_ff // W_CHUNKS, d_ff // W_CHUNKS
    pitch = tc + SUBLANES
    n_slab = REC_W // LANES
    h3 = h.reshape(nb, seq, d)
    an3 = an.reshape(nb, seq, ATT_W)
    xy3 = xy.reshape(nb, seq, 2 * REC_W)

    def tile(i):
        return (0, jnp.clip(i - (W_CHUNKS + 1), 0, n_tiles - 1), 0)

    def next_tile(i):
        return (0, jnp.clip(i - W_CHUNKS, 0, n_tiles - 1), 0)

    def chunk(i):
        return jnp.minimum(i, W_CHUNKS - 1)

    out_specs = [pl.BlockSpec((nb, tc, d), tile)]
    out_shape = [jax.ShapeDtypeStruct((nb, seq, d), F32)]
    if not final:
        out_specs.append(pl.BlockSpec(hm.shape, lambda i: (0, 0)))
        out_shape.append(jax.ShapeDtypeStruct(hm.shape, F32))

    outs = pl.pallas_call(
        functools.partial(_post_kernel, n_tiles=n_tiles, final=final, nb=nb, tc=tc),
        grid=(W_CHUNKS + 1 + n_tiles + (0 if final else 1),),
        in_specs=[
            pl.BlockSpec((nb, tc, d), tile),
            pl.BlockSpec((nb, tc, ATT_W), tile),
            pl.BlockSpec((nb, tc, 2 * REC_W), next_tile),
            _whole(hm),
            _whole(an_m),
            _whole(rn_m),
            pl.BlockSpec((None, rows_o, d), lambda i: (layer, chunk(i), 0)),
            _of_layer(g, layer),
            pl.BlockSpec((None, d, cols_u), lambda i: (layer, 0, chunk(i))),
            pl.BlockSpec((None, rows_d, d), lambda i: (layer, chunk(i), 0)),
            _whole(fg),
            _of_layer(cw, layer),
            _of_layer(cb, layer),
            _of_layer(wg, layer),
            _of_layer(bg, layer),
            _of_layer(lru, layer),
            _of_layer(g_rec, layer),
            _whole(h0),
            _whole(tail0),
        ],
        out_specs=out_specs,
        out_shape=out_shape,
        scratch_shapes=[
            pltpu.VMEM((d_mix, d), BF16),
            pltpu.VMEM((W_CHUNKS, d, cols_u), BF16),
            pltpu.VMEM((d_ff, d), BF16),
            pltpu.VMEM((2, nb * tc, REC_W), BF16),
            pltpu.VMEM((nb, tc + SUBLANES, REC_W), F32),
            pltpu.VMEM((n_slab, nb * pitch, LANES), F32),
            pltpu.VMEM((n_slab, nb * pitch, LANES), F32),
            pltpu.VMEM((nb, REC_W), F32),
        ],
        compiler_params=pltpu.CompilerParams(
            dimension_semantics=("arbitrary",), vmem_limit_bytes=VMEM_LIMIT),
        name="post",
    )(h3, an3, xy3, hm, an_m, rn_m, w_out, g, w_up, w_down, fg,
      cw, cb, wg, bg, lru, g_rec, h0, tail0)
    h_new = outs[0].reshape(n, d)
    return (h_new, hm) if final else (h_new, outs[1])


def _block_diag_gates(w_a, w_x):
    depth, nblk, blk, _ = w_a.shape
    per = nblk // 2
    eye = jnp.eye(per, dtype=w_a.dtype)

    def bd(w):
        w5 = w.reshape(depth, 2, per, blk, blk)
        full = jnp.einsum('lgpij,pq->lgpiqj', w5, eye)
        return full.reshape(depth, 2, per * blk, per * blk)

    return jnp.concatenate([bd(w_a), bd(w_x)], axis=-1).astype(BF16)


def kernel(x, meta, attn_norm_g, w_in, b_f, conv_w, conv_b, w_gate_a, b_gate_a,
           w_gate_x, b_gate_x, lru_L, attn_out_g, rec_out_g, w_out, mlp_norm_g,
           w_up, w_down, final_g):
    nb, seq, d = x.shape
    depth = w_in.shape[0]
    c_f, c_x = 3 * ATT_W, 3 * ATT_W + N_HEADS

    w_t = jnp.swapaxes(w_in, 1, 2)
    wf_t = jnp.pad(w_t[:, c_f:c_x, :], ((0, 0), (0, F_ROWS - N_HEADS), (0, 0)))
    wxy_t = w_t[:, c_x:, :]
    bf = jnp.pad(b_f, ((0, 0), (0, F_ROWS - N_HEADS))).reshape(depth, F_ROWS, 1)
    g_in = attn_norm_g.reshape(depth, 1, d)
    cb = conv_b.reshape(depth, 1, REC_W)
    wg = _block_diag_gates(w_gate_a, w_gate_x)
    bg = jnp.stack([b_gate_a, b_gate_x], axis=1)
    lru = lru_L.reshape(depth, 1, REC_W)
    g_att = attn_out_g.reshape(depth, 1, ATT_W)
    g_rec = rec_out_g.reshape(depth, 1, REC_W)
    g_mlp = mlp_norm_g.reshape(depth, 1, d)
    fg = final_g.reshape(1, d)
    h0_meta = jnp.zeros((1, REC_W), F32)
    tail0_meta = jnp.zeros((1, SUBLANES, REC_W), F32)

    h = x.reshape(nb * seq, d)
    hm = jnp.pad(meta, ((0, META_PAD - N_META), (0, 0)))

    for l in range(depth):
        last = l == depth - 1
        k, qvt, logf, xy, k_m, qvt_m, logf_m, xy_m = _inproj(
            h, hm, g_in, w_t, wf_t, wxy_t, bf, l, seq=seq, tm=ROW_TILE, tile=ATT_TILE)

        an_m, bias_m = _meta_attn(k_m, qvt_m, logf_m, g_att, l)
        rn_m, h_state, x_tail = _rec(
            xy_m, conv_w, cb, wg, bg, lru, g_rec, l, h0_meta, tail0_meta,
            nb=1, seq=META_PAD, tc=META_PAD, state_t=N_META - 1)

        an = _attn(k, qvt, logf, k_m, qvt_m, bias_m, g_att, l, seq=seq, tq=ATT_TILE)
        h, hm = _post(h, an, xy, hm, an_m, rn_m, w_out, w_up, w_down, g_mlp, fg,
                      conv_w, cb, wg, bg, lru, g_rec, h_state, x_tail, l,
                      nb=nb, seq=seq, tc=TIME_TILE, final=last)

    return h.reshape(nb, seq, d)
```

```python
import functools
import math

import jax
import jax.numpy as jnp
from jax import lax
from jax.experimental import pallas as pl
from jax.experimental.pallas import tpu as pltpu

F32 = jnp.float32
BF16 = jnp.bfloat16

N_META = 16
META_PAD = 128
N_HEADS = 8
HEAD_DIM = 64
ATT_W = N_HEADS * HEAD_DIM
REC_W = 512
CONV_WIDTH = 4
RG_C = 8.0
NORM_EPS = 1e-6
LANES = 128
SUBLANES = 8
LOG2E = math.log2(math.e)
ATT_TILE = 512
ROW_TILE = 1024
TIME_TILE = 64
W_CHUNKS = 8
IN_CHUNKS = 4
SCORES_AHEAD = 1
F_ROWS = 16
SUM_ROWS = 16
ACC_ROWS = HEAD_DIM + SUM_ROWS
VMEM_LIMIT = 56 * 1024 * 1024


def _rms(x, g):
    ms = jnp.mean(x * x, axis=-1, keepdims=True)
    return x * lax.rsqrt(ms + NORM_EPS) * g


def _log_sigmoid(x):
    return jnp.minimum(x, 0.0) - jnp.log1p(jnp.exp(-jnp.abs(x)))


def _gelu_tanh(x):
    c = math.sqrt(2.0 / math.pi)
    return x * (0.5 * (1.0 + jnp.tanh(c * (x + 0.044715 * (x * x * x)))))


def _dot(a, b):
    return jnp.dot(a, b, preferred_element_type=F32)


def _dot_nt(a, b):
    return lax.dot_general(a, b, (((1,), (1,)), ((), ())), preferred_element_type=F32)


def _whole(arr):
    nd = arr.ndim
    return pl.BlockSpec(arr.shape, lambda *_: (0,) * nd, pipeline_mode=pl.Buffered(1))


def _of_layer(arr, layer):
    nd = arr.ndim - 1
    return pl.BlockSpec((None,) + arr.shape[1:], lambda *_: (layer,) + (0,) * nd,
                        pipeline_mode=pl.Buffered(1))


def _inproj_kernel(h_ref, hm_ref, g_ref, wq_ref, wk_ref, wv_ref, wf_ref, wxy_ref, bf_ref,
                   k_ref, qvt_ref, logf_ref, xy_ref, km_ref, qvtm_ref, logfm_ref, xym_ref,
                   wk_s, wqvt_s, wxy_s, *, n_tiles):
    i = pl.program_id(0)

    @pl.when(i < IN_CHUNKS)
    def _():
        rows = wq_ref.shape[0]
        r0 = pl.multiple_of(i * rows, rows)
        wq = wq_ref[...] * (LOG2E / math.sqrt(HEAD_DIM))
        wqvt_s[pl.ds(r0, rows), :] = wq.astype(BF16)
        wqvt_s[pl.ds(ATT_W + r0, rows), :] = wv_ref[...].astype(BF16)
        wxy_s[i] = wxy_ref[...].T.astype(BF16)

    @pl.when(i < wk_s.shape[0])
    def _():
        wk_s[i] = wk_ref[...].T.astype(BF16)

    @pl.when(i == 0)
    def _():
        wqvt_s[2 * ATT_W:2 * ATT_W + F_ROWS, :] = wf_ref[...].astype(BF16)

    def project(h_tile, k_out, qvt_out, logf_out, xy_out):
        z = _rms(h_tile, g_ref[...]).astype(BF16)
        k_out[...] = jnp.concatenate(
            [_dot(z, wk_s[c]) for c in range(wk_s.shape[0])], axis=-1).astype(BF16)
        qvft = _dot_nt(wqvt_s[...], z)
        qvt = qvft[:2 * ATT_W, :].astype(BF16)
        tile = qvt_out.shape[2]
        for s in range(qvt_out.shape[0]):
            qvt_out[s] = qvt[:, s * tile:(s + 1) * tile]
        xy_out[...] = jnp.concatenate(
            [_dot(z, wxy_s[c]) for c in range(wxy_s.shape[0])], axis=-1)
        logf_out[0] = _log_sigmoid(qvft[2 * ATT_W:, :] + bf_ref[...])

    @pl.when(jnp.logical_and(i >= IN_CHUNKS, i < IN_CHUNKS + n_tiles))
    def _():
        project(h_ref[...], k_ref, qvt_ref, logf_ref, xy_ref)

    @pl.when(i == IN_CHUNKS + n_tiles)
    def _():
        project(hm_ref[...], km_ref, qvtm_ref, logfm_ref, xym_ref)


def _inproj(h, hm, g, w_t, wf_t, xy_row0, bf, layer, *, seq, tm, tile):
    n, d = h.shape
    tmeta = hm.shape[0]
    n_tiles = n // tm
    per_seq = seq // tm
    per = tm // tile
    rows = ATT_W // IN_CHUNKS
    rows_k = 2 * rows
    n_k = ATT_W // rows_k
    rows_xy = 2 * REC_W // IN_CHUNKS
    q_blk, k_blk, v_blk = 0, ATT_W // rows_k, 2 * (ATT_W // rows)

    def chunk(i):
        return jnp.minimum(i, IN_CHUNKS - 1)

    def row(i):
        return jnp.clip(i - IN_CHUNKS, 0, n_tiles - 1)

    return pl.pallas_call(
        functools.partial(_inproj_kernel, n_tiles=n_tiles),
        grid=(IN_CHUNKS + n_tiles + 1,),
        in_specs=[
            pl.BlockSpec((tm, d), lambda i: (row(i), 0)),
            _whole(hm),
            _of_layer(g, layer),
            pl.BlockSpec((None, rows, d), lambda i: (layer, q_blk + chunk(i), 0)),
            pl.BlockSpec((None, rows_k, d), lambda i: (layer, k_blk + jnp.minimum(i, n_k - 1), 0)),
            pl.BlockSpec((None, rows, d), lambda i: (layer, v_blk + chunk(i), 0)),
            _of_layer(wf_t, layer),
            pl.BlockSpec((None, pl.Element(rows_xy), pl.Element(d)),
                         lambda i: (layer, pl.multiple_of(xy_row0 + chunk(i) * rows_xy, SUBLANES), 0)),
            _of_layer(bf, layer),
        ],
        out_specs=[
            pl.BlockSpec((tm, ATT_W), lambda i: (row(i), 0)),
            pl.BlockSpec((per, 2 * ATT_W, tile), lambda i: (row(i), 0, 0)),
            pl.BlockSpec((1, F_ROWS, tm), lambda i: (row(i) // per_seq, 0, row(i) % per_seq)),
            pl.BlockSpec((tm, 2 * REC_W), lambda i: (row(i), 0)),
            pl.BlockSpec((tmeta, ATT_W), lambda i: (0, 0)),
            pl.BlockSpec((1, 2 * ATT_W, tmeta), lambda i: (0, 0, 0)),
            pl.BlockSpec((1, F_ROWS, tmeta), lambda i: (0, 0, 0)),
            pl.BlockSpec((tmeta, 2 * REC_W), lambda i: (0, 0)),
        ],
        out_shape=[
            jax.ShapeDtypeStruct((n, ATT_W), BF16),
            jax.ShapeDtypeStruct((n // tile, 2 * ATT_W, tile), BF16),
            jax.ShapeDtypeStruct((n // seq, F_ROWS, seq), F32),
            jax.ShapeDtypeStruct((n, 2 * REC_W), F32),
            jax.ShapeDtypeStruct((tmeta, ATT_W), BF16),
            jax.ShapeDtypeStruct((1, 2 * ATT_W, tmeta), BF16),
            jax.ShapeDtypeStruct((1, F_ROWS, tmeta), F32),
            jax.ShapeDtypeStruct((tmeta, 2 * REC_W), F32),
        ],
        scratch_shapes=[
            pltpu.VMEM((n_k, d, rows_k), BF16),
            pltpu.VMEM((2 * ATT_W + F_ROWS, d), BF16),
            pltpu.VMEM((IN_CHUNKS, d, rows_xy), BF16),
        ],
        compiler_params=pltpu.CompilerParams(
            dimension_semantics=("arbitrary",), vmem_limit_bytes=VMEM_LIMIT),
        name="inproj",
    )(h, hm, g, w_t, w_t, w_t, wf_t, w_t, bf)


def _neg_cumsum_cols(lf_ref, out_ref):
    r = lax.broadcasted_iota(jnp.int32, (LANES, LANES), 0)
    c = lax.broadcasted_iota(jnp.int32, (LANES, LANES), 1)
    lower = (r >= c).astype(BF16)
    pad = jnp.zeros((LANES - F_ROWS, LANES), F32)
    carry = jnp.zeros((1, LANES), F32)
    for blk in range(lf_ref.shape[2] // LANES):
        x = jnp.concatenate([lf_ref[0, :, blk * LANES:(blk + 1) * LANES], pad], axis=0)
        hi = x.astype(BF16)
        r1 = x - hi.astype(F32)
        mid = r1.astype(BF16)
        lo = (r1 - mid.astype(F32)).astype(BF16)
        cs = _dot_nt(lower, hi) + _dot_nt(lower, mid) + _dot_nt(lower, lo) + carry
        out_ref[blk * LANES:(blk + 1) * LANES, :] = cs * (-LOG2E)
        carry = cs[LANES - 1:LANES, :]


def _scores_t(kh, qth, bias_col, mask):
    st = _dot(kh, qth) + bias_col
    if mask is not None:
        st = jnp.where(mask, st, -jnp.inf)
    return st


def _pv_and_sum(vth, p):
    ones = jnp.ones((SUM_ROWS, vth.shape[1]), BF16)
    return _dot(jnp.concatenate([vth, ones], axis=0), p)


def _softmax_first(st, vth):
    m = jnp.max(st, axis=0, keepdims=True)
    return m, _pv_and_sum(vth, jnp.exp2(st - m).astype(BF16))


def _softmax_next(st, vth, m, acc):
    m_new = jnp.maximum(m, jnp.max(st, axis=0, keepdims=True))
    alpha = jnp.exp2(m - m_new)
    return m_new, alpha * acc + _pv_and_sum(vth, jnp.exp2(st - m_new).astype(BF16))


def _normalized(acc):
    return acc[:HEAD_DIM, :] / acc[HEAD_DIM:HEAD_DIM + 1, :]


def _meta_attn_kernel(k_ref, qt_ref, vt_ref, lf_ref, g_ref, o_ref, bm_ref, b_s, ot_s):
    t = k_ref.shape[0]
    _neg_cumsum_cols(lf_ref, b_s)
    row = lax.broadcasted_iota(jnp.int32, (t, LANES), 0)
    bm_ref[...] = jnp.where(row < N_META, b_s[...] - b_s[N_META - 1:N_META, :], -jnp.inf)
    key = lax.broadcasted_iota(jnp.int32, (t, t), 0)
    qry = lax.broadcasted_iota(jnp.int32, (t, t), 1)
    causal = key <= qry
    for h in range(N_HEADS):
        sl = slice(h * HEAD_DIM, (h + 1) * HEAD_DIM)
        st = _scores_t(k_ref[:, sl], qt_ref[0, sl, :], b_s[:, h:h + 1], causal)
        _, acc = _softmax_first(st, vt_ref[0, sl, :])
        ot_s[sl, :] = _normalized(acc)
    o_ref[...] = _rms(ot_s[...].T, g_ref[...]).astype(BF16)


def _meta_attn(k, qvt, logf, g, layer):
    t = k.shape[0]
    return pl.pallas_call(
        _meta_attn_kernel,
        grid=(1,),
        in_specs=[
            pl.BlockSpec((t, ATT_W), lambda i: (0, 0)),
            pl.BlockSpec((1, ATT_W, t), lambda i: (0, 0, 0)),
            pl.BlockSpec((1, ATT_W, t), lambda i: (0, 1, 0)),
            pl.BlockSpec((1, F_ROWS, t), lambda i: (0, 0, 0)),
            _of_layer(g, layer),
        ],
        out_specs=[
            pl.BlockSpec((t, ATT_W), lambda i: (0, 0)),
            pl.BlockSpec((t, LANES), lambda i: (0, 0)),
        ],
        out_shape=[
            jax.ShapeDtypeStruct((t, ATT_W), BF16),
            jax.ShapeDtypeStruct((t, LANES), F32),
        ],
        scratch_shapes=[pltpu.VMEM((t, LANES), F32), pltpu.VMEM((ATT_W, t), F32)],
        name="meta_attn",
    )(k, qvt, qvt, logf, g)


def _attn_kernel(qt_ref, k_ref, vt_ref, lf_ref, km_ref, vtm_ref, bm_ref, g_ref,
                 o_ref, b_s, m_s, acc_s, ot_s, st0_s, *, tq):
    j = pl.program_id(1)

    @pl.when(j == 0)
    def _():
        _neg_cumsum_cols(lf_ref, b_s)

    heads = [slice(h * HEAD_DIM, (h + 1) * HEAD_DIM) for h in range(N_HEADS)]
    accs = [slice(h * ACC_ROWS, (h + 1) * ACC_ROWS) for h in range(N_HEADS)]

    def scores(h, i):
        off = pl.multiple_of(i * tq, tq)
        sl = heads[h]
        return _scores_t(k_ref[pl.ds(off, tq), sl], qt_ref[0, sl, :],
                         b_s[pl.ds(off, tq), h:h + 1], None)

    def tile(i, mask, finish):
        ahead = st0_s.shape[0]
        sts = [st0_s[p] for p in range(ahead)]
        for h, sl in enumerate(heads):
            if h + ahead < N_HEADS:
                sts.append(scores(h + ahead, i))
            elif not finish:
                st0_s[h + ahead - N_HEADS] = scores(h + ahead - N_HEADS, i + 1)
            st = sts[h] if mask is None else jnp.where(mask, sts[h], -jnp.inf)
            m, acc = _softmax_next(st, vt_ref[i, sl, :], m_s[h:h + 1, :], acc_s[accs[h], :])
            if finish:
                ot_s[sl, :] = _normalized(acc)
            else:
                m_s[h:h + 1, :] = m
                acc_s[accs[h], :] = acc

    for p in range(st0_s.shape[0]):
        st0_s[p] = scores(p, 0)

    sts = [_scores_t(km_ref[0:N_META, sl], qt_ref[0, sl, :], bm_ref[0:N_META, h:h + 1], None)
           for h, sl in enumerate(heads)]
    for h, sl in enumerate(heads):
        m, acc = _softmax_first(sts[h], vtm_ref[0, sl, 0:N_META])
        m_s[h:h + 1, :] = m
        acc_s[accs[h], :] = acc

    def body(i, carry):
        tile(i, None, False)
        return carry

    lax.fori_loop(0, j, body, 0)

    key = lax.broadcasted_iota(jnp.int32, (tq, tq), 0)
    qry = lax.broadcasted_iota(jnp.int32, (tq, tq), 1)
    tile(j, key <= qry, True)
    o_ref[...] = _rms(ot_s[...].T, g_ref[...]).astype(BF16)


def _attn(k, qvt, logf, k_meta, qvt_meta, bias_meta, g, layer, *, seq, tq):
    n = k.shape[0]
    nb = n // seq
    per_seq = seq // tq
    tmeta = k_meta.shape[0]
    return pl.pallas_call(
        functools.partial(_attn_kernel, tq=tq),
        grid=(nb, per_seq),
        in_specs=[
            pl.BlockSpec((1, ATT_W, tq), lambda b, j: (b * per_seq + j, 0, 0)),
            pl.BlockSpec((seq, ATT_W), lambda b, j: (b, 0)),
            pl.BlockSpec((per_seq, ATT_W, tq), lambda b, j: (b, 1, 0)),
            pl.BlockSpec((1, F_ROWS, seq), lambda b, j: (b, 0, 0)),
            pl.BlockSpec((tmeta, ATT_W), lambda b, j: (0, 0)),
            pl.BlockSpec((1, ATT_W, tmeta), lambda b, j: (0, 1, 0)),
            _whole(bias_meta),
            _of_layer(g, layer),
        ],
        out_specs=pl.BlockSpec((tq, ATT_W), lambda b, j: (b * per_seq + j, 0)),
        out_shape=jax.ShapeDtypeStruct((n, ATT_W), BF16),
        scratch_shapes=[
            pltpu.VMEM((seq, LANES), F32),
            pltpu.VMEM((N_HEADS, tq), F32),
            pltpu.VMEM((N_HEADS * ACC_ROWS, tq), F32),
            pltpu.VMEM((ATT_W, tq), F32),
            pltpu.VMEM((SCORES_AHEAD, tq, tq), F32),
        ],
        compiler_params=pltpu.CompilerParams(
            dimension_semantics=("arbitrary", "arbitrary"), vmem_limit_bytes=VMEM_LIMIT),
        name="attn",
    )(qvt, k, qvt, logf, k_meta, qvt_meta, bias_meta, g)


def _rec_init(h0_ref, tail0_ref, ext_s, h_s, *, nb, tc):
    for b in range(nb):
        ext_s[b, tc:tc + SUBLANES, :] = tail0_ref[0]
    h_s[...] = jnp.broadcast_to(h0_ref[...], h_s.shape)


class _RecRefs:
    def __init__(self, cw_ref, cb_ref, wg_ref, bg_ref, lru_ref, g_ref, ext_s, a_s, u_s, h_s,
                 *, nb, tc):
        self.cw, self.cb, self.wg, self.bg, self.lru, self.g = (
            cw_ref, cb_ref, wg_ref, bg_ref, lru_ref, g_ref)
        self.ext_s, self.a_s, self.u_s, self.h_s = ext_s, a_s, u_s, h_s
        self.nb, self.tc = nb, tc
        self.pitch = tc + SUBLANES
        self.n_slab = REC_W // LANES


def _rec_conv_gates(rr, b, xr_b):
    tc, half = rr.tc, REC_W // 2
    rr.ext_s[b, 0:SUBLANES, :] = rr.ext_s[b, tc:tc + SUBLANES, :]
    rr.ext_s[b, SUBLANES:SUBLANES + tc, :] = xr_b
    ext = rr.ext_s[b]
    run = rr.cw[0:1, :] * ext
    for k in range(1, CONV_WIDTH):
        run = pltpu.roll(run, 1, 0) + rr.cw[k:k + 1, :] * ext
    xc = run[SUBLANES:, :] + rr.cb[...]
    xcb = xc.astype(BF16)
    g0 = _dot(xcb[:, :half], rr.wg[0])
    g1 = _dot(xcb[:, half:], rr.wg[1])
    ra = jnp.concatenate([g0[:, :half], g1[:, :half]], axis=-1) + rr.bg[0:1, :]
    rx = jnp.concatenate([g0[:, half:], g1[:, half:]], axis=-1) + rr.bg[1:2, :]
    return xc, ra, rx


def _rec_decay(rr, b, xc, ra, rx):
    tc = rr.tc
    r = jax.nn.sigmoid(ra)
    i = jax.nn.sigmoid(rx)
    log_a = RG_C * r * _log_sigmoid(rr.lru[...])
    a = jnp.exp(log_a)
    th = jnp.tanh(log_a)
    om = -2.0 * th / (1.0 - th)
    mult = jnp.where(om > 0.0, om * lax.rsqrt(om), 0.0)
    u = mult * i * xc
    lo = b * rr.pitch
    for s in range(rr.n_slab):
        rr.a_s[s, lo:lo + tc, :] = a[:, s * LANES:(s + 1) * LANES]
        rr.u_s[s, lo:lo + tc, :] = u[:, s * LANES:(s + 1) * LANES]


def _rec_scan(rr, *, unroll):
    def rows(t):
        return pl.ds(t, rr.nb, stride=rr.pitch) if rr.nb > 1 else pl.ds(t, 1)

    def scan_body(t, hs):
        out = []
        for s in range(rr.n_slab):
            hn = rr.a_s[s, rows(t), :] * hs[s] + rr.u_s[s, rows(t), :]
            rr.u_s[s, rows(t), :] = hn
            out.append(hn)
        return tuple(out)

    hs = tuple(rr.h_s[:, s * LANES:(s + 1) * LANES] for s in range(rr.n_slab))
    if unroll:
        for t in range(rr.tc):
            hs = scan_body(t, hs)
    else:
        hs = lax.fori_loop(0, rr.tc, scan_body, hs, unroll=8)
    rr.h_s[...] = jnp.concatenate(hs, axis=-1)


def _rec_output(rr, b, yr_b):
    lo = b * rr.pitch
    hb = jnp.concatenate(
        [rr.u_s[s, lo:lo + rr.tc, :] for s in range(rr.n_slab)], axis=-1)
    return _rms(hb * _gelu_tanh(yr_b), rr.g[...])


def _rec_kernel(xr_ref, yr_ref, cw_ref, cb_ref, wg_ref, bg_ref, lru_ref, g_ref,
                h0_ref, tail0_ref, out_ref, hn_ref, tailn_ref,
                ext_s, a_s, u_s, h_s, *, nb, tc, pitch, state_t):
    @pl.when(pl.program_id(0) == 0)
    def _():
        _rec_init(h0_ref, tail0_ref, ext_s, h_s, nb=nb, tc=tc)

    rr = _RecRefs(cw_ref, cb_ref, wg_ref, bg_ref, lru_ref, g_ref, ext_s, a_s, u_s, h_s,
                  nb=nb, tc=tc)
    for b in range(nb):
        _rec_decay(rr, b, *_rec_conv_gates(rr, b, xr_ref[b]))
    _rec_scan(rr, unroll=False)
    for b in range(nb):
        out_ref[b] = _rec_output(rr, b, yr_ref[b]).astype(BF16)

    n_slab = REC_W // LANES
    rows = pl.ds(state_t, nb, stride=pitch) if nb > 1 else pl.ds(state_t, 1)
    hn_ref[...] = jnp.concatenate([u_s[s, rows, :] for s in range(n_slab)], axis=-1)
    for b in range(nb):
        tailn_ref[b] = ext_s[b, state_t + 1:state_t + 1 + SUBLANES, :]


def _rec(xy, cw, cb, wg, bg, lru, g, layer, h0, tail0, *, nb, seq, tc, state_t):
    xy3 = xy.reshape(nb, seq, 2 * REC_W)
    pitch = tc + SUBLANES
    n_slab = REC_W // LANES
    out, hn, tailn = pl.pallas_call(
        functools.partial(_rec_kernel, nb=nb, tc=tc, pitch=pitch, state_t=state_t),
        grid=(seq // tc,),
        in_specs=[
            pl.BlockSpec((nb, tc, REC_W), lambda t: (0, t, 0)),
            pl.BlockSpec((nb, tc, REC_W), lambda t: (0, t, 1)),
            _of_layer(cw, layer),
            _of_layer(cb, layer),
            _of_layer(wg, layer),
            _of_layer(bg, layer),
            _of_layer(lru, layer),
            _of_layer(g, layer),
            _whole(h0),
            _whole(tail0),
        ],
        out_specs=[
            pl.BlockSpec((nb, tc, REC_W), lambda t: (0, t, 0)),
            pl.BlockSpec((nb, REC_W), lambda t: (0, 0)),
            pl.BlockSpec((nb, SUBLANES, REC_W), lambda t: (0, 0, 0)),
        ],
        out_shape=[
            jax.ShapeDtypeStruct((nb, seq, REC_W), BF16),
            jax.ShapeDtypeStruct((nb, REC_W), F32),
            jax.ShapeDtypeStruct((nb, SUBLANES, REC_W), F32),
        ],
        scratch_shapes=[
            pltpu.VMEM((nb, tc + SUBLANES, REC_W), F32),
            pltpu.VMEM((n_slab, nb * pitch, LANES), F32),
            pltpu.VMEM((n_slab, nb * pitch, LANES), F32),
            pltpu.VMEM((nb, REC_W), F32),
        ],
        compiler_params=pltpu.CompilerParams(
            dimension_semantics=("arbitrary",), vmem_limit_bytes=VMEM_LIMIT),
        name="rec",
    )(xy3, xy3, cw, cb, wg, bg, lru, g, h0, tail0)
    return out.reshape(nb * seq, REC_W), hn, tailn


def _post_kernel(h_ref, an_ref, xy_ref, hm_ref, anm_ref, rnm_ref,
                 wo_ref, g_ref, wu_ref, wd_ref, fg_ref,
                 cw_ref, cb_ref, wg_ref, bg_ref, lru_ref, grec_ref, h0_ref, tail0_ref,
                 o_ref, *rest, n_tiles, final, nb, tc):
    om_ref = None if final else rest[0]
    wo_s, wu_s, wd_s, rn_s, ext_s, a_s, u_s, hst_s = rest[-8:]
    i = pl.program_id(0)
    rows_o = wo_ref.shape[0]
    rows_d = wd_ref.shape[0]
    first_tile = W_CHUNKS + 1

    @pl.when(i < W_CHUNKS)
    def _():
        wo_s[pl.ds(pl.multiple_of(i * rows_o, rows_o), rows_o), :] = wo_ref[...].astype(BF16)
        wu_s[i] = wu_ref[...].astype(BF16)
        wd_s[pl.ds(pl.multiple_of(i * rows_d, rows_d), rows_d), :] = wd_ref[...].astype(BF16)

    @pl.when(i == 0)
    def _():
        _rec_init(h0_ref, tail0_ref, ext_s, hst_s, nb=nb, tc=tc)

    rr = _RecRefs(cw_ref, cb_ref, wg_ref, bg_ref, lru_ref, grec_ref, ext_s, a_s, u_s, hst_s,
                  nb=nb, tc=tc)

    def rec_gates(b):
        _rec_decay(rr, b, *_rec_conv_gates(rr, b, xy_ref[b, :, 0:REC_W]))

    def rec_output(b, slot):
        out = _rec_output(rr, b, xy_ref[b, :, REC_W:2 * REC_W])
        rn_s[slot, b * tc:(b + 1) * tc, :] = out.astype(BF16)

    @pl.when(i == W_CHUNKS)
    def _():
        for b in range(nb):
            rec_gates(b)
        _rec_scan(rr, unroll=True)
        for b in range(nb):
            rec_output(b, 0)

    def mix_and_mlp(h_tile, an, rn, between=None):
        h1 = h_tile + _dot(an, wo_s[0:ATT_W, :]) + _dot(rn, wo_s[ATT_W:ATT_W + REC_W, :])
        z = _rms(h1, g_ref[...]).astype(BF16)
        down = None
        for c in range(W_CHUNKS):
            if between is not None:
                between(c)
            u = jnp.maximum(_dot(z, wu_s[c]), 0.0)
            part = _dot((u * u).astype(BF16), wd_s[c * rows_d:(c + 1) * rows_d, :])
            down = part if down is None else down + part
        return h1 + down

    @pl.when(jnp.logical_and(i >= first_tile, i < first_tile + n_tiles))
    def _():
        s = i - first_tile
        nxt = lax.rem(s + 1, 2)
        half = W_CHUNKS // 2

        def next_tile_rec(c):
            if c == half:
                _rec_scan(rr, unroll=True)
            for b in range((c % half) * nb // half, (c % half + 1) * nb // half):
                if c < half:
                    rec_gates(b)
                else:
                    rec_output(b, nxt)

        rn = rn_s[lax.rem(s, 2)]
        h_tile = jnp.concatenate([h_ref[b] for b in range(nb)], axis=0)
        an = jnp.concatenate([an_ref[b] for b in range(nb)], axis=0)
        acc = mix_and_mlp(h_tile, an, rn, next_tile_rec)
        if final:
            acc = _rms(acc, fg_ref[...])
        for b in range(nb):
            o_ref[b] = acc[b * tc:(b + 1) * tc, :]

    if not final:
        @pl.when(i == first_tile + n_tiles)
        def _():
            om_ref[...] = mix_and_mlp(hm_ref[...], anm_ref[...], rnm_ref[...])


def _post(h, an, xy, hm, an_m, rn_m, w_out, w_up, w_down, g, fg,
          cw, cb, wg, bg, lru, g_rec, h0, tail0, layer, *, nb, seq, tc, final):
    n, d = h.shape
    n_tiles = seq // tc
    d_mix, d_ff = w_out.shape[1], w_up.shape[2]
    rows_o, cols_u, rows_d = d_mix // W_CHUNKS, d_ff // W_CHUNKS, d_ff // W_CHUNKS
    pitch = tc + SUBLANES
    n_slab = REC_W // LANES
    h3 = h.reshape(nb, seq, d)
    an3 = an.reshape(nb, seq, ATT_W)
    xy3 = xy.reshape(nb, seq, 2 * REC_W)

    def tile(i):
        return (0, jnp.clip(i - (W_CHUNKS + 1), 0, n_tiles - 1), 0)

    def next_tile(i):
        return (0, jnp.clip(i - W_CHUNKS, 0, n_tiles - 1), 0)

    def chunk(i):
        return jnp.minimum(i, W_CHUNKS - 1)

    out_specs = [pl.BlockSpec((nb, tc, d), tile)]
    out_shape = [jax.ShapeDtypeStruct((nb, seq, d), F32)]
    if not final:
        out_specs.append(pl.BlockSpec(hm.shape, lambda i: (0, 0)))
        out_shape.append(jax.ShapeDtypeStruct(hm.shape, F32))

    outs = pl.pallas_call(
        functools.partial(_post_kernel, n_tiles=n_tiles, final=final, nb=nb, tc=tc),
        grid=(W_CHUNKS + 1 + n_tiles + (0 if final else 1),),
        in_specs=[
            pl.BlockSpec((nb, tc, d), tile),
            pl.BlockSpec((nb, tc, ATT_W), tile),
            pl.BlockSpec((nb, tc, 2 * REC_W), next_tile),
            _whole(hm),
            _whole(an_m),
            _whole(rn_m),
            pl.BlockSpec((None, rows_o, d), lambda i: (layer, chunk(i), 0)),
            _of_layer(g, layer),
            pl.BlockSpec((None, d, cols_u), lambda i: (layer, 0, chunk(i))),
            pl.BlockSpec((None, rows_d, d), lambda i: (layer, chunk(i), 0)),
            _whole(fg),
            _of_layer(cw, layer),
            _of_layer(cb, layer),
            _of_layer(wg, layer),
            _of_layer(bg, layer),
            _of_layer(lru, layer),
            _of_layer(g_rec, layer),
            _whole(h0),
            _whole(tail0),
        ],
        out_specs=out_specs,
        out_shape=out_shape,
        scratch_shapes=[
            pltpu.VMEM((d_mix, d), BF16),
            pltpu.VMEM((W_CHUNKS, d, cols_u), BF16),
            pltpu.VMEM((d_ff, d), BF16),
            pltpu.VMEM((2, nb * tc, REC_W), BF16),
            pltpu.VMEM((nb, tc + SUBLANES, REC_W), F32),
            pltpu.VMEM((n_slab, nb * pitch, LANES), F32),
            pltpu.VMEM((n_slab, nb * pitch, LANES), F32),
            pltpu.VMEM((nb, REC_W), F32),
        ],
        compiler_params=pltpu.CompilerParams(
            dimension_semantics=("arbitrary",), vmem_limit_bytes=VMEM_LIMIT),
        name="post",
    )(h3, an3, xy3, hm, an_m, rn_m, w_out, g, w_up, w_down, fg,
      cw, cb, wg, bg, lru, g_rec, h0, tail0)
    h_new = outs[0].reshape(n, d)
    return (h_new, hm) if final else (h_new, outs[1])


def _block_diag_gates(w_a, w_x):
    depth, nblk, blk, _ = w_a.shape
    per = nblk // 2
    eye = jnp.eye(per, dtype=w_a.dtype)

    def bd(w):
        w5 = w.reshape(depth, 2, per, blk, blk)
        full = jnp.einsum('lgpij,pq->lgpiqj', w5, eye)
        return full.reshape(depth, 2, per * blk, per * blk)

    return jnp.concatenate([bd(w_a), bd(w_x)], axis=-1).astype(BF16)


def kernel(x, meta, attn_norm_g, w_in, b_f, conv_w, conv_b, w_gate_a, b_gate_a,
           w_gate_x, b_gate_x, lru_L, attn_out_g, rec_out_g, w_out, mlp_norm_g,
           w_up, w_down, final_g):
    nb, seq, d = x.shape
    depth = w_in.shape[0]
    c_f, c_x = 3 * ATT_W, 3 * ATT_W + N_HEADS

    w_t = jnp.swapaxes(w_in, 1, 2)
    wf_t = jnp.pad(w_t[:, c_f:c_x, :], ((0, 0), (0, F_ROWS - N_HEADS), (0, 0)))
    bf = jnp.pad(b_f, ((0, 0), (0, F_ROWS - N_HEADS))).reshape(depth, F_ROWS, 1)
    g_in = attn_norm_g.reshape(depth, 1, d)
    cb = conv_b.reshape(depth, 1, REC_W)
    wg = _block_diag_gates(w_gate_a, w_gate_x)
    bg = jnp.stack([b_gate_a, b_gate_x], axis=1)
    lru = lru_L.reshape(depth, 1, REC_W)
    g_att = attn_out_g.reshape(depth, 1, ATT_W)
    g_rec = rec_out_g.reshape(depth, 1, REC_W)
    g_mlp = mlp_norm_g.reshape(depth, 1, d)
    fg = final_g.reshape(1, d)
    h0_meta = jnp.zeros((1, REC_W), F32)
    tail0_meta = jnp.zeros((1, SUBLANES, REC_W), F32)

    h = x.reshape(nb * seq, d)
    hm = jnp.pad(meta, ((0, META_PAD - N_META), (0, 0)))

    for l in range(depth):
        last = l == depth - 1
        k, qvt, logf, xy, k_m, qvt_m, logf_m, xy_m = _inproj(
            h, hm, g_in, w_t, wf_t, c_x, bf, l, seq=seq, tm=ROW_TILE, tile=ATT_TILE)

        an_m, bias_m = _meta_attn(k_m, qvt_m, logf_m, g_att, l)
        rn_m, h_state, x_tail = _rec(
            xy_m, conv_w, cb, wg, bg, lru, g_rec, l, h0_meta, tail0_meta,
            nb=1, seq=META_PAD, tc=META_PAD, state_t=N_META - 1)

        an = _attn(k, qvt, logf, k_m, qvt_m, bias_m, g_att, l, seq=seq, tq=ATT_TILE)
        h, hm = _post(h, an, xy, hm, an_m, rn_m, w_out, w_up, w_down, g_mlp, fg,
                      conv_w, cb, wg, bg, lru, g_rec, h_state, x_tail, l,
                      nb=nb, seq=seq, tc=TIME_TILE, final=last)

    return h.reshape(nb, seq, d)
```

```python
import functools
import math

import jax
import jax.numpy as jnp
from jax import lax
from jax.experimental import pallas as pl
from jax.experimental.pallas import tpu as pltpu

F32 = jnp.float32
BF16 = jnp.bfloat16

N_META = 16
META_PAD = 128
N_HEADS = 8
HEAD_DIM = 64
ATT_W = N_HEADS * HEAD_DIM
REC_W = 512
CONV_WIDTH = 4
RG_C = 8.0
NORM_EPS = 1e-6
LANES = 128
SUBLANES = 8
LOG2E = math.log2(math.e)
ATT_TILE = 512
ROW_TILE = 1024
TIME_TILE = 64
W_CHUNKS = 8
IN_CHUNKS = 4
SCORES_AHEAD = 1
F_ROWS = 16
SUM_ROWS = 16
ACC_ROWS = HEAD_DIM + SUM_ROWS
VMEM_LIMIT = 56 * 1024 * 1024


def _rms(x, g):
    ms = jnp.mean(x * x, axis=-1, keepdims=True)
    return x * lax.rsqrt(ms + NORM_EPS) * g


def _log_sigmoid(x):
    return jnp.minimum(x, 0.0) - jnp.log1p(jnp.exp(-jnp.abs(x)))


def _gelu_tanh(x):
    c = math.sqrt(2.0 / math.pi)
    return x * (0.5 * (1.0 + jnp.tanh(c * (x + 0.044715 * (x * x * x)))))


def _dot(a, b):
    return jnp.dot(a, b, preferred_element_type=F32)


def _dot_nt(a, b):
    return lax.dot_general(a, b, (((1,), (1,)), ((), ())), preferred_element_type=F32)


def _whole(arr):
    nd = arr.ndim
    return pl.BlockSpec(arr.shape, lambda *_: (0,) * nd, pipeline_mode=pl.Buffered(1))


def _of_layer(arr, layer):
    nd = arr.ndim - 1
    return pl.BlockSpec((None,) + arr.shape[1:], lambda *_: (layer,) + (0,) * nd,
                        pipeline_mode=pl.Buffered(1))


def _inproj_kernel(h_ref, hm_ref, g_ref, wq_ref, wk_ref, wv_ref, wf_ref, wxy_ref, bf_ref,
                   k_ref, qvt_ref, logf_ref, xy_ref, km_ref, qvtm_ref, logfm_ref, xym_ref,
                   wk_s, wqvt_s, wxy_s, *, n_tiles):
    i = pl.program_id(0)

    @pl.when(i < IN_CHUNKS)
    def _():
        rows = wq_ref.shape[0]
        r0 = pl.multiple_of(i * rows, rows)
        wq = wq_ref[...] * (LOG2E / math.sqrt(HEAD_DIM))
        wqvt_s[pl.ds(r0, rows), :] = wq.astype(BF16)
        wqvt_s[pl.ds(ATT_W + r0, rows), :] = wv_ref[...].astype(BF16)
        wxy_s[i] = wxy_ref[...].T.astype(BF16)

    @pl.when(i < wk_s.shape[0])
    def _():
        wk_s[i] = wk_ref[...].T.astype(BF16)

    @pl.when(i == 0)
    def _():
        wqvt_s[2 * ATT_W:2 * ATT_W + F_ROWS, :] = wf_ref[...].astype(BF16)

    def project(h_tile, k_out, qvt_out, logf_out, xy_out):
        z = _rms(h_tile, g_ref[...]).astype(BF16)
        k_out[...] = jnp.concatenate(
            [_dot(z, wk_s[c]) for c in range(wk_s.shape[0])], axis=-1).astype(BF16)
        qvft = _dot_nt(wqvt_s[...], z)
        qvt = qvft[:2 * ATT_W, :].astype(BF16)
        tile = qvt_out.shape[2]
        for s in range(qvt_out.shape[0]):
            qvt_out[s] = qvt[:, s * tile:(s + 1) * tile]
        xy_out[...] = jnp.concatenate(
            [_dot(z, wxy_s[c]) for c in range(wxy_s.shape[0])], axis=-1)
        logf_out[0] = _log_sigmoid(qvft[2 * ATT_W:, :] + bf_ref[...])

    @pl.when(jnp.logical_and(i >= IN_CHUNKS, i < IN_CHUNKS + n_tiles))
    def _():
        project(h_ref[...], k_ref, qvt_ref, logf_ref, xy_ref)

    @pl.when(i == IN_CHUNKS + n_tiles)
    def _():
        project(hm_ref[...], km_ref, qvtm_ref, logfm_ref, xym_ref)


def _inproj(h, hm, g, w_t, wf_t, xy_row0, bf, layer, *, seq, tm, tile):
    n, d = h.shape
    tmeta = hm.shape[0]
    n_tiles = n // tm
    per_seq = seq // tm
    per = tm // tile
    rows = ATT_W // IN_CHUNKS
    rows_k = 2 * rows
    n_k = ATT_W // rows_k
    rows_xy = 2 * REC_W // IN_CHUNKS
    q_blk, k_blk, v_blk = 0, ATT_W // rows_k, 2 * (ATT_W // rows)

    def chunk(i):
        return jnp.minimum(i, IN_CHUNKS - 1)

    def row(i):
        return jnp.clip(i - IN_CHUNKS, 0, n_tiles - 1)

    return pl.pallas_call(
        functools.partial(_inproj_kernel, n_tiles=n_tiles),
        grid=(IN_CHUNKS + n_tiles + 1,),
        in_specs=[
            pl.BlockSpec((tm, d), lambda i: (row(i), 0)),
            _whole(hm),
            _of_layer(g, layer),
            pl.BlockSpec((None, rows, d), lambda i: (layer, q_blk + chunk(i), 0)),
            pl.BlockSpec((None, rows_k, d), lambda i: (layer, k_blk + jnp.minimum(i, n_k - 1), 0)),
            pl.BlockSpec((None, rows, d), lambda i: (layer, v_blk + chunk(i), 0)),
            _of_layer(wf_t, layer),
            pl.BlockSpec((None, pl.Element(rows_xy), pl.Element(d)),
                         lambda i: (layer, pl.multiple_of(xy_row0 + chunk(i) * rows_xy, SUBLANES), 0)),
            _of_layer(bf, layer),
        ],
        out_specs=[
            pl.BlockSpec((tm, ATT_W), lambda i: (row(i), 0)),
            pl.BlockSpec((per, 2 * ATT_W, tile), lambda i: (row(i), 0, 0)),
            pl.BlockSpec((1, F_ROWS, tm), lambda i: (row(i) // per_seq, 0, row(i) % per_seq)),
            pl.BlockSpec((tm, 2 * REC_W), lambda i: (row(i), 0)),
            pl.BlockSpec((tmeta, ATT_W), lambda i: (0, 0)),
            pl.BlockSpec((1, 2 * ATT_W, tmeta), lambda i: (0, 0, 0)),
            pl.BlockSpec((1, F_ROWS, tmeta), lambda i: (0, 0, 0)),
            pl.BlockSpec((tmeta, 2 * REC_W), lambda i: (0, 0)),
        ],
        out_shape=[
            jax.ShapeDtypeStruct((n, ATT_W), BF16),
            jax.ShapeDtypeStruct((n // tile, 2 * ATT_W, tile), BF16),
            jax.ShapeDtypeStruct((n // seq, F_ROWS, seq), F32),
            jax.ShapeDtypeStruct((n, 2 * REC_W), F32),
            jax.ShapeDtypeStruct((tmeta, ATT_W), BF16),
            jax.ShapeDtypeStruct((1, 2 * ATT_W, tmeta), BF16),
            jax.ShapeDtypeStruct((1, F_ROWS, tmeta), F32),
            jax.ShapeDtypeStruct((tmeta, 2 * REC_W), F32),
        ],
        scratch_shapes=[
            pltpu.VMEM((n_k, d, rows_k), BF16),
            pltpu.VMEM((2 * ATT_W + F_ROWS, d), BF16),
            pltpu.VMEM((IN_CHUNKS, d, rows_xy), BF16),
        ],
        compiler_params=pltpu.CompilerParams(
            dimension_semantics=("arbitrary",), vmem_limit_bytes=VMEM_LIMIT),
        name="inproj",
    )(h, hm, g, w_t, w_t, w_t, wf_t, w_t, bf)


def _neg_cumsum_cols(lf_ref, out_ref):
    r = lax.broadcasted_iota(jnp.int32, (LANES, LANES), 0)
    c = lax.broadcasted_iota(jnp.int32, (LANES, LANES), 1)
    lower = (r >= c).astype(BF16)
    pad = jnp.zeros((LANES - F_ROWS, LANES), F32)
    carry = jnp.zeros((1, LANES), F32)
    for blk in range(lf_ref.shape[2] // LANES):
        x = jnp.concatenate([lf_ref[0, :, blk * LANES:(blk + 1) * LANES], pad], axis=0)
        hi = x.astype(BF16)
        r1 = x - hi.astype(F32)
        mid = r1.astype(BF16)
        lo = (r1 - mid.astype(F32)).astype(BF16)
        cs = _dot_nt(lower, hi) + _dot_nt(lower, mid) + _dot_nt(lower, lo) + carry
        out_ref[blk * LANES:(blk + 1) * LANES, :] = cs * (-LOG2E)
        carry = cs[LANES - 1:LANES, :]


def _scores_t(kh, qth, bias_col, mask):
    st = _dot(kh, qth) + bias_col
    if mask is not None:
        st = jnp.where(mask, st, -jnp.inf)
    return st


def _pv_and_sum(vth, p):
    ones = jnp.ones((SUM_ROWS, vth.shape[1]), BF16)
    return _dot(jnp.concatenate([vth, ones], axis=0), p)


def _softmax_first(st, vth):
    m = jnp.max(st, axis=0, keepdims=True)
    return m, _pv_and_sum(vth, jnp.exp2(st - m).astype(BF16))


def _softmax_next(st, vth, m, acc):
    m_new = jnp.maximum(m, jnp.max(st, axis=0, keepdims=True))
    alpha = jnp.exp2(m - m_new)
    return m_new, alpha * acc + _pv_and_sum(vth, jnp.exp2(st - m_new).astype(BF16))


def _softmax_next_diag(st, vth, m, acc):
    n = st.shape[0]
    key = lax.broadcasted_iota(jnp.int32, (LANES, LANES), 0)
    qry = lax.broadcasted_iota(jnp.int32, (LANES, LANES), 1)
    tri = key <= qry
    ms, ps = [], []
    for c in range(n // LANES):
        rows = LANES * (c + 1)
        cols = slice(LANES * c, rows)
        blk = jnp.where(tri, st[rows - LANES:rows, cols], -jnp.inf)
        if c > 0:
            blk = jnp.concatenate([st[0:rows - LANES, cols], blk], axis=0)
        m_c = jnp.maximum(m[:, cols], jnp.max(blk, axis=0, keepdims=True))
        p_c = jnp.exp2(blk - m_c).astype(BF16)
        if rows < n:
            p_c = jnp.concatenate([p_c, jnp.zeros((n - rows, LANES), BF16)], axis=0)
        ms.append(m_c)
        ps.append(p_c)
    m_new = jnp.concatenate(ms, axis=1)
    alpha = jnp.exp2(m - m_new)
    return m_new, alpha * acc + _pv_and_sum(vth, jnp.concatenate(ps, axis=1))


def _normalized(acc):
    return acc[:HEAD_DIM, :] / acc[HEAD_DIM:HEAD_DIM + 1, :]


def _meta_attn_kernel(k_ref, qt_ref, vt_ref, lf_ref, g_ref, o_ref, bm_ref, b_s, ot_s):
    t = k_ref.shape[0]
    _neg_cumsum_cols(lf_ref, b_s)
    row = lax.broadcasted_iota(jnp.int32, (t, LANES), 0)
    bm_ref[...] = jnp.where(row < N_META, b_s[...] - b_s[N_META - 1:N_META, :], -jnp.inf)
    key = lax.broadcasted_iota(jnp.int32, (t, t), 0)
    qry = lax.broadcasted_iota(jnp.int32, (t, t), 1)
    causal = key <= qry
    for h in range(N_HEADS):
        sl = slice(h * HEAD_DIM, (h + 1) * HEAD_DIM)
        st = _scores_t(k_ref[:, sl], qt_ref[0, sl, :], b_s[:, h:h + 1], causal)
        _, acc = _softmax_first(st, vt_ref[0, sl, :])
        ot_s[sl, :] = _normalized(acc)
    o_ref[...] = _rms(ot_s[...].T, g_ref[...]).astype(BF16)


def _meta_attn(k, qvt, logf, g, layer):
    t = k.shape[0]
    return pl.pallas_call(
        _meta_attn_kernel,
        grid=(1,),
        in_specs=[
            pl.BlockSpec((t, ATT_W), lambda i: (0, 0)),
            pl.BlockSpec((1, ATT_W, t), lambda i: (0, 0, 0)),
            pl.BlockSpec((1, ATT_W, t), lambda i: (0, 1, 0)),
            pl.BlockSpec((1, F_ROWS, t), lambda i: (0, 0, 0)),
            _of_layer(g, layer),
        ],
        out_specs=[
            pl.BlockSpec((t, ATT_W), lambda i: (0, 0)),
            pl.BlockSpec((t, LANES), lambda i: (0, 0)),
        ],
        out_shape=[
            jax.ShapeDtypeStruct((t, ATT_W), BF16),
            jax.ShapeDtypeStruct((t, LANES), F32),
        ],
        scratch_shapes=[pltpu.VMEM((t, LANES), F32), pltpu.VMEM((ATT_W, t), F32)],
        name="meta_attn",
    )(k, qvt, qvt, logf, g)


def _attn_kernel(qt_ref, k_ref, vt_ref, lf_ref, km_ref, vtm_ref, bm_ref, g_ref,
                 o_ref, b_s, m_s, acc_s, ot_s, st0_s, *, tq):
    j = pl.program_id(1)

    @pl.when(j == 0)
    def _():
        _neg_cumsum_cols(lf_ref, b_s)

    heads = [slice(h * HEAD_DIM, (h + 1) * HEAD_DIM) for h in range(N_HEADS)]
    accs = [slice(h * ACC_ROWS, (h + 1) * ACC_ROWS) for h in range(N_HEADS)]

    def scores(h, i):
        off = pl.multiple_of(i * tq, tq)
        sl = heads[h]
        return _scores_t(k_ref[pl.ds(off, tq), sl], qt_ref[0, sl, :],
                         b_s[pl.ds(off, tq), h:h + 1], None)

    def tile(i, mask, finish):
        ahead = st0_s.shape[0]
        sts = [st0_s[p] for p in range(ahead)]
        for h, sl in enumerate(heads):
            if h + ahead < N_HEADS:
                sts.append(scores(h + ahead, i))
            elif not finish:
                st0_s[h + ahead - N_HEADS] = scores(h + ahead - N_HEADS, i + 1)
            step = _softmax_next_diag if finish else _softmax_next
            m, acc = step(sts[h], vt_ref[i, sl, :], m_s[h:h + 1, :], acc_s[accs[h], :])
            if finish:
                ot_s[sl, :] = _normalized(acc)
            else:
                m_s[h:h + 1, :] = m
                acc_s[accs[h], :] = acc

    for p in range(st0_s.shape[0]):
        st0_s[p] = scores(p, 0)

    sts = [_scores_t(km_ref[0:N_META, sl], qt_ref[0, sl, :], bm_ref[0:N_META, h:h + 1], None)
           for h, sl in enumerate(heads)]
    for h, sl in enumerate(heads):
        m, acc = _softmax_first(sts[h], vtm_ref[0, sl, 0:N_META])
        m_s[h:h + 1, :] = m
        acc_s[accs[h], :] = acc

    def body(i, carry):
        tile(i, None, False)
        return carry

    lax.fori_loop(0, j, body, 0)

    key = lax.broadcasted_iota(jnp.int32, (tq, tq), 0)
    qry = lax.broadcasted_iota(jnp.int32, (tq, tq), 1)
    tile(j, key <= qry, True)
    o_ref[...] = _rms(ot_s[...].T, g_ref[...]).astype(BF16)


def _attn(k, qvt, logf, k_meta, qvt_meta, bias_meta, g, layer, *, seq, tq):
    n = k.shape[0]
    nb = n // seq
    per_seq = seq // tq
    tmeta = k_meta.shape[0]
    return pl.pallas_call(
        functools.partial(_attn_kernel, tq=tq),
        grid=(nb, per_seq),
        in_specs=[
            pl.BlockSpec((1, ATT_W, tq), lambda b, j: (b * per_seq + j, 0, 0)),
            pl.BlockSpec((seq, ATT_W), lambda b, j: (b, 0)),
            pl.BlockSpec((per_seq, ATT_W, tq), lambda b, j: (b, 1, 0)),
            pl.BlockSpec((1, F_ROWS, seq), lambda b, j: (b, 0, 0)),
            pl.BlockSpec((tmeta, ATT_W), lambda b, j: (0, 0)),
            pl.BlockSpec((1, ATT_W, tmeta), lambda b, j: (0, 1, 0)),
            _whole(bias_meta),
            _of_layer(g, layer),
        ],
        out_specs=pl.BlockSpec((tq, ATT_W), lambda b, j: (b * per_seq + j, 0)),
        out_shape=jax.ShapeDtypeStruct((n, ATT_W), BF16),
        scratch_shapes=[
            pltpu.VMEM((seq, LANES), F32),
            pltpu.VMEM((N_HEADS, tq), F32),
            pltpu.VMEM((N_HEADS * ACC_ROWS, tq), F32),
            pltpu.VMEM((ATT_W, tq), F32),
            pltpu.VMEM((SCORES_AHEAD, tq, tq), F32),
        ],
        compiler_params=pltpu.CompilerParams(
            dimension_semantics=("arbitrary", "arbitrary"), vmem_limit_bytes=VMEM_LIMIT),
        name="attn",
    )(qvt, k, qvt, logf, k_meta, qvt_meta, bias_meta, g)


def _rec_init(h0_ref, tail0_ref, ext_s, h_s, *, nb, tc):
    for b in range(nb):
        ext_s[b, tc:tc + SUBLANES, :] = tail0_ref[0]
    h_s[...] = jnp.broadcast_to(h0_ref[...], h_s.shape)


class _RecRefs:
    def __init__(self, cw_ref, cb_ref, wg_ref, bg_ref, lru_ref, g_ref, ext_s, a_s, u_s, h_s,
                 *, nb, tc):
        self.cw, self.cb, self.wg, self.bg, self.lru, self.g = (
            cw_ref, cb_ref, wg_ref, bg_ref, lru_ref, g_ref)
        self.ext_s, self.a_s, self.u_s, self.h_s = ext_s, a_s, u_s, h_s
        self.nb, self.tc = nb, tc
        self.pitch = tc + SUBLANES
        self.n_slab = REC_W // LANES


def _rec_conv_gates(rr, b, xr_b):
    tc, half = rr.tc, REC_W // 2
    rr.ext_s[b, 0:SUBLANES, :] = rr.ext_s[b, tc:tc + SUBLANES, :]
    rr.ext_s[b, SUBLANES:SUBLANES + tc, :] = xr_b
    ext = rr.ext_s[b]
    run = rr.cw[0:1, :] * ext
    for k in range(1, CONV_WIDTH):
        run = pltpu.roll(run, 1, 0) + rr.cw[k:k + 1, :] * ext
    xc = run[SUBLANES:, :] + rr.cb[...]
    xcb = xc.astype(BF16)
    g0 = _dot(xcb[:, :half], rr.wg[0])
    g1 = _dot(xcb[:, half:], rr.wg[1])
    ra = jnp.concatenate([g0[:, :half], g1[:, :half]], axis=-1) + rr.bg[0:1, :]
    rx = jnp.concatenate([g0[:, half:], g1[:, half:]], axis=-1) + rr.bg[1:2, :]
    return xc, ra, rx


def _rec_decay(rr, b, xc, ra, rx):
    tc = rr.tc
    r = jax.nn.sigmoid(ra)
    i = jax.nn.sigmoid(rx)
    log_a = RG_C * r * _log_sigmoid(rr.lru[...])
    a = jnp.exp(log_a)
    th = jnp.tanh(log_a)
    om = -2.0 * th / (1.0 - th)
    mult = jnp.where(om > 0.0, om * lax.rsqrt(om), 0.0)
    u = mult * i * xc
    lo = b * rr.pitch
    for s in range(rr.n_slab):
        rr.a_s[s, lo:lo + tc, :] = a[:, s * LANES:(s + 1) * LANES]
        rr.u_s[s, lo:lo + tc, :] = u[:, s * LANES:(s + 1) * LANES]


def _rec_scan(rr, *, unroll):
    def rows(t):
        return pl.ds(t, rr.nb, stride=rr.pitch) if rr.nb > 1 else pl.ds(t, 1)

    def scan_body(t, hs):
        out = []
        for s in range(rr.n_slab):
            hn = rr.a_s[s, rows(t), :] * hs[s] + rr.u_s[s, rows(t), :]
            rr.u_s[s, rows(t), :] = hn
            out.append(hn)
        return tuple(out)

    hs = tuple(rr.h_s[:, s * LANES:(s + 1) * LANES] for s in range(rr.n_slab))
    if unroll:
        for t in range(rr.tc):
            hs = scan_body(t, hs)
    else:
        hs = lax.fori_loop(0, rr.tc, scan_body, hs, unroll=8)
    rr.h_s[...] = jnp.concatenate(hs, axis=-1)


def _rec_output(rr, b, yr_b):
    lo = b * rr.pitch
    hb = jnp.concatenate(
        [rr.u_s[s, lo:lo + rr.tc, :] for s in range(rr.n_slab)], axis=-1)
    return _rms(hb * _gelu_tanh(yr_b), rr.g[...])


def _rec_kernel(xr_ref, yr_ref, cw_ref, cb_ref, wg_ref, bg_ref, lru_ref, g_ref,
                h0_ref, tail0_ref, out_ref, hn_ref, tailn_ref,
                ext_s, a_s, u_s, h_s, *, nb, tc, pitch, state_t):
    @pl.when(pl.program_id(0) == 0)
    def _():
        _rec_init(h0_ref, tail0_ref, ext_s, h_s, nb=nb, tc=tc)

    rr = _RecRefs(cw_ref, cb_ref, wg_ref, bg_ref, lru_ref, g_ref, ext_s, a_s, u_s, h_s,
                  nb=nb, tc=tc)
    for b in range(nb):
        _rec_decay(rr, b, *_rec_conv_gates(rr, b, xr_ref[b]))
    _rec_scan(rr, unroll=False)
    for b in range(nb):
        out_ref[b] = _rec_output(rr, b, yr_ref[b]).astype(BF16)

    n_slab = REC_W // LANES
    rows = pl.ds(state_t, nb, stride=pitch) if nb > 1 else pl.ds(state_t, 1)
    hn_ref[...] = jnp.concatenate([u_s[s, rows, :] for s in range(n_slab)], axis=-1)
    for b in range(nb):
        tailn_ref[b] = ext_s[b, state_t + 1:state_t + 1 + SUBLANES, :]


def _rec(xy, cw, cb, wg, bg, lru, g, layer, h0, tail0, *, nb, seq, tc, state_t):
    xy3 = xy.reshape(nb, seq, 2 * REC_W)
    pitch = tc + SUBLANES
    n_slab = REC_W // LANES
    out, hn, tailn = pl.pallas_call(
        functools.partial(_rec_kernel, nb=nb, tc=tc, pitch=pitch, state_t=state_t),
        grid=(seq // tc,),
        in_specs=[
            pl.BlockSpec((nb, tc, REC_W), lambda t: (0, t, 0)),
            pl.BlockSpec((nb, tc, REC_W), lambda t: (0, t, 1)),
            _of_layer(cw, layer),
            _of_layer(cb, layer),
            _of_layer(wg, layer),
            _of_layer(bg, layer),
            _of_layer(lru, layer),
            _of_layer(g, layer),
            _whole(h0),
            _whole(tail0),
        ],
        out_specs=[
            pl.BlockSpec((nb, tc, REC_W), lambda t: (0, t, 0)),
            pl.BlockSpec((nb, REC_W), lambda t: (0, 0)),
            pl.BlockSpec((nb, SUBLANES, REC_W), lambda t: (0, 0, 0)),
        ],
        out_shape=[
            jax.ShapeDtypeStruct((nb, seq, REC_W), BF16),
            jax.ShapeDtypeStruct((nb, REC_W), F32),
            jax.ShapeDtypeStruct((nb, SUBLANES, REC_W), F32),
        ],
        scratch_shapes=[
            pltpu.VMEM((nb, tc + SUBLANES, REC_W), F32),
            pltpu.VMEM((n_slab, nb * pitch, LANES), F32),
            pltpu.VMEM((n_slab, nb * pitch, LANES), F32),
            pltpu.VMEM((nb, REC_W), F32),
        ],
        compiler_params=pltpu.CompilerParams(
            dimension_semantics=("arbitrary",), vmem_limit_bytes=VMEM_LIMIT),
        name="rec",
    )(xy3, xy3, cw, cb, wg, bg, lru, g, h0, tail0)
    return out.reshape(nb * seq, REC_W), hn, tailn


def _post_kernel(h_ref, an_ref, xy_ref, hm_ref, anm_ref, rnm_ref,
                 wo_ref, g_ref, wu_ref, wd_ref, fg_ref,
                 cw_ref, cb_ref, wg_ref, bg_ref, lru_ref, grec_ref, h0_ref, tail0_ref,
                 o_ref, *rest, n_tiles, final, nb, tc):
    om_ref = None if final else rest[0]
    wo_s, wu_s, wd_s, rn_s, ext_s, a_s, u_s, hst_s = rest[-8:]
    i = pl.program_id(0)
    rows_o = wo_ref.shape[0]
    rows_d = wd_ref.shape[0]
    first_tile = W_CHUNKS + 1

    @pl.when(i < W_CHUNKS)
    def _():
        wo_s[pl.ds(pl.multiple_of(i * rows_o, rows_o), rows_o), :] = wo_ref[...].astype(BF16)
        wu_s[i] = wu_ref[...].astype(BF16)
        wd_s[pl.ds(pl.multiple_of(i * rows_d, rows_d), rows_d), :] = wd_ref[...].astype(BF16)

    @pl.when(i == 0)
    def _():
        _rec_init(h0_ref, tail0_ref, ext_s, hst_s, nb=nb, tc=tc)

    rr = _RecRefs(cw_ref, cb_ref, wg_ref, bg_ref, lru_ref, grec_ref, ext_s, a_s, u_s, hst_s,
                  nb=nb, tc=tc)

    def rec_gates(b):
        _rec_decay(rr, b, *_rec_conv_gates(rr, b, xy_ref[b, :, 0:REC_W]))

    def rec_output(b, slot):
        out = _rec_output(rr, b, xy_ref[b, :, REC_W:2 * REC_W])
        rn_s[slot, b * tc:(b + 1) * tc, :] = out.astype(BF16)

    @pl.when(i == W_CHUNKS)
    def _():
        for b in range(nb):
            rec_gates(b)
        _rec_scan(rr, unroll=True)
        for b in range(nb):
            rec_output(b, 0)

    def mix_and_mlp(h_tile, an, rn, between=None):
        h1 = h_tile + _dot(an, wo_s[0:ATT_W, :]) + _dot(rn, wo_s[ATT_W:ATT_W + REC_W, :])
        z = _rms(h1, g_ref[...]).astype(BF16)
        down = None
        for c in range(W_CHUNKS):
            if between is not None:
                between(c)
            u = jnp.maximum(_dot(z, wu_s[c]), 0.0)
            part = _dot((u * u).astype(BF16), wd_s[c * rows_d:(c + 1) * rows_d, :])
            down = part if down is None else down + part
        return h1 + down

    @pl.when(jnp.logical_and(i >= first_tile, i < first_tile + n_tiles))
    def _():
        s = i - first_tile
        nxt = lax.rem(s + 1, 2)
        half = W_CHUNKS // 2

        def next_tile_rec(c):
            if c == half:
                _rec_scan(rr, unroll=True)
            for b in range((c % half) * nb // half, (c % half + 1) * nb // half):
                if c < half:
                    rec_gates(b)
                else:
                    rec_output(b, nxt)

        rn = rn_s[lax.rem(s, 2)]
        h_tile = jnp.concatenate([h_ref[b] for b in range(nb)], axis=0)
        an = jnp.concatenate([an_ref[b] for b in range(nb)], axis=0)
        acc = mix_and_mlp(h_tile, an, rn, next_tile_rec)
        if final:
            acc = _rms(acc, fg_ref[...])
        for b in range(nb):
            o_ref[b] = acc[b * tc:(b + 1) * tc, :]

    if not final:
        @pl.when(i == first_tile + n_tiles)
        def _():
            om_ref[...] = mix_and_mlp(hm_ref[...], anm_ref[...], rnm_ref[...])


def _post(h, an, xy, hm, an_m, rn_m, w_out, w_up, w_down, g, fg,
          cw, cb, wg, bg, lru, g_rec, h0, tail0, layer, *, nb, seq, tc, final):
    n, d = h.shape
    n_tiles = seq // tc
    d_mix, d_ff = w_out.shape[1], w_up.shape[2]
    rows_o, cols_u, rows_d = d_mix // W_CHUNKS, d_ff // W_CHUNKS, d_ff // W_CHUNKS
    pitch = tc + SUBLANES
    n_slab = REC_W // LANES
    h3 = h.reshape(nb, seq, d)
    an3 = an.reshape(nb, seq, ATT_W)
    xy3 = xy.reshape(nb, seq, 2 * REC_W)

    def tile(i):
        return (0, jnp.clip(i - (W_CHUNKS + 1), 0, n_tiles - 1), 0)

    def next_tile(i):
        return (0, jnp.clip(i - W_CHUNKS, 0, n_tiles - 1), 0)

    def chunk(i):
        return jnp.minimum(i, W_CHUNKS - 1)

    out_specs = [pl.BlockSpec((nb, tc, d), tile)]
    out_shape = [jax.ShapeDtypeStruct((nb, seq, d), F32)]
    if not final:
        out_specs.append(pl.BlockSpec(hm.shape, lambda i: (0, 0)))
        out_shape.append(jax.ShapeDtypeStruct(hm.shape, F32))

    outs = pl.pallas_call(
        functools.partial(_post_kernel, n_tiles=n_tiles, final=final, nb=nb, tc=tc),
        grid=(W_CHUNKS + 1 + n_tiles + (0 if final else 1),),
        in_specs=[
            pl.BlockSpec((nb, tc, d), tile),
            pl.BlockSpec((nb, tc, ATT_W), tile),
            pl.BlockSpec((nb, tc, 2 * REC_W), next_tile),
            _whole(hm),
            _whole(an_m),
            _whole(rn_m),
            pl.BlockSpec((None, rows_o, d), lambda i: (layer, chunk(i), 0)),
            _of_layer(g, layer),
            pl.BlockSpec((None, d, cols_u), lambda i: (layer, 0, chunk(i))),
            pl.BlockSpec((None, rows_d, d), lambda i: (layer, chunk(i), 0)),
            _whole(fg),
            _of_layer(cw, layer),
            _of_layer(cb, layer),
            _of_layer(wg, layer),
            _of_layer(bg, layer),
            _of_layer(lru, layer),
            _of_layer(g_rec, layer),
            _whole(h0),
            _whole(tail0),
        ],
        out_specs=out_specs,
        out_shape=out_shape,
        scratch_shapes=[
            pltpu.VMEM((d_mix, d), BF16),
            pltpu.VMEM((W_CHUNKS, d, cols_u), BF16),
            pltpu.VMEM((d_ff, d), BF16),
            pltpu.VMEM((2, nb * tc, REC_W), BF16),
            pltpu.VMEM((nb, tc + SUBLANES, REC_W), F32),
            pltpu.VMEM((n_slab, nb * pitch, LANES), F32),
            pltpu.VMEM((n_slab, nb * pitch, LANES), F32),
            pltpu.VMEM((nb, REC_W), F32),
        ],
        compiler_params=pltpu.CompilerParams(
            dimension_semantics=("arbitrary",), vmem_limit_bytes=VMEM_LIMIT),
        name="post",
    )(h3, an3, xy3, hm, an_m, rn_m, w_out, g, w_up, w_down, fg,
      cw, cb, wg, bg, lru, g_rec, h0, tail0)
    h_new = outs[0].reshape(n, d)
    return (h_new, hm) if final else (h_new, outs[1])


def _block_diag_gates(w_a, w_x):
    depth, nblk, blk, _ = w_a.shape
    per = nblk // 2
    eye = jnp.eye(per, dtype=w_a.dtype)

    def bd(w):
        w5 = w.reshape(depth, 2, per, blk, blk)
        full = jnp.einsum('lgpij,pq->lgpiqj', w5, eye)
        return full.reshape(depth, 2, per * blk, per * blk)

    return jnp.concatenate([bd(w_a), bd(w_x)], axis=-1).astype(BF16)


def kernel(x, meta, attn_norm_g, w_in, b_f, conv_w, conv_b, w_gate_a, b_gate_a,
           w_gate_x, b_gate_x, lru_L, attn_out_g, rec_out_g, w_out, mlp_norm_g,
           w_up, w_down, final_g):
    nb, seq, d = x.shape
    depth = w_in.shape[0]
    c_f, c_x = 3 * ATT_W, 3 * ATT_W + N_HEADS

    w_t = jnp.swapaxes(w_in, 1, 2)
    wf_t = jnp.pad(w_t[:, c_f:c_x, :], ((0, 0), (0, F_ROWS - N_HEADS), (0, 0)))
    bf = jnp.pad(b_f, ((0, 0), (0, F_ROWS - N_HEADS))).reshape(depth, F_ROWS, 1)
    g_in = attn_norm_g.reshape(depth, 1, d)
    cb = conv_b.reshape(depth, 1, REC_W)
    wg = _block_diag_gates(w_gate_a, w_gate_x)
    bg = jnp.stack([b_gate_a, b_gate_x], axis=1)
    lru = lru_L.reshape(depth, 1, REC_W)
    g_att = attn_out_g.reshape(depth, 1, ATT_W)
    g_rec = rec_out_g.reshape(depth, 1, REC_W)
    g_mlp = mlp_norm_g.reshape(depth, 1, d)
    fg = final_g.reshape(1, d)
    h0_meta = jnp.zeros((1, REC_W), F32)
    tail0_meta = jnp.zeros((1, SUBLANES, REC_W), F32)

    h = x.reshape(nb * seq, d)
    hm = jnp.pad(meta, ((0, META_PAD - N_META), (0, 0)))

    for l in range(depth):
        last = l == depth - 1
        k, qvt, logf, xy, k_m, qvt_m, logf_m, xy_m = _inproj(
            h, hm, g_in, w_t, wf_t, c_x, bf, l, seq=seq, tm=ROW_TILE, tile=ATT_TILE)

        an_m, bias_m = _meta_attn(k_m, qvt_m, logf_m, g_att, l)
        rn_m, h_state, x_tail = _rec(
            xy_m, conv_w, cb, wg, bg, lru, g_rec, l, h0_meta, tail0_meta,
            nb=1, seq=META_PAD, tc=META_PAD, state_t=N_META - 1)

        an = _attn(k, qvt, logf, k_m, qvt_m, bias_m, g_att, l, seq=seq, tq=ATT_TILE)
        h, hm = _post(h, an, xy, hm, an_m, rn_m, w_out, w_up, w_down, g_mlp, fg,
                      conv_w, cb, wg, bg, lru, g_rec, h_state, x_tail, l,
                      nb=nb, seq=seq, tc=TIME_TILE, final=last)

    return h.reshape(nb, seq, d)
```

```python
import functools
import math

import jax
import jax.numpy as jnp
from jax import lax
from jax.experimental import pallas as pl
from jax.experimental.pallas import tpu as pltpu

F32 = jnp.float32
BF16 = jnp.bfloat16

N_META = 16
META_PAD = 128
N_HEADS = 8
HEAD_DIM = 64
ATT_W = N_HEADS * HEAD_DIM
REC_W = 512
CONV_WIDTH = 4
RG_C = 8.0
NORM_EPS = 1e-6
LANES = 128
SUBLANES = 8
LOG2E = math.log2(math.e)
ATT_TILE = 512
ROW_TILE = 1024
TIME_TILE = 64
W_CHUNKS = 8
IN_CHUNKS = 4
SCORES_AHEAD = 1
F_ROWS = 16
SUM_ROWS = 16
ACC_ROWS = HEAD_DIM + SUM_ROWS
VMEM_LIMIT = 56 * 1024 * 1024


def _rms(x, g):
    ms = jnp.mean(x * x, axis=-1, keepdims=True)
    return x * lax.rsqrt(ms + NORM_EPS) * g


def _log_sigmoid(x):
    return jnp.minimum(x, 0.0) - jnp.log1p(jnp.exp(-jnp.abs(x)))


def _gelu_tanh(x):
    c = math.sqrt(2.0 / math.pi)
    return x * (0.5 * (1.0 + jnp.tanh(c * (x + 0.044715 * (x * x * x)))))


def _dot(a, b):
    return jnp.dot(a, b, preferred_element_type=F32)


def _dot_nt(a, b):
    return lax.dot_general(a, b, (((1,), (1,)), ((), ())), preferred_element_type=F32)


def _whole(arr):
    nd = arr.ndim
    return pl.BlockSpec(arr.shape, lambda *_: (0,) * nd, pipeline_mode=pl.Buffered(1))


def _of_layer(arr, layer):
    nd = arr.ndim - 1
    return pl.BlockSpec((None,) + arr.shape[1:], lambda *_: (layer,) + (0,) * nd,
                        pipeline_mode=pl.Buffered(1))


def _inproj_kernel(h_ref, hm_ref, g_ref, wq_ref, wk_ref, wv_ref, wf_ref, wxy_ref, bf_ref,
                   k_ref, qvt_ref, logf_ref, xy_ref, km_ref, qvtm_ref, logfm_ref, xym_ref,
                   wk_s, wqvt_s, wxy_s, *, n_tiles):
    i = pl.program_id(0)

    @pl.when(i < IN_CHUNKS)
    def _():
        rows = wq_ref.shape[0]
        r0 = pl.multiple_of(i * rows, rows)
        wq = wq_ref[...] * (LOG2E / math.sqrt(HEAD_DIM))
        wqvt_s[pl.ds(r0, rows), :] = wq.astype(BF16)
        wqvt_s[pl.ds(ATT_W + r0, rows), :] = wv_ref[...].astype(BF16)
        wxy_s[i] = wxy_ref[...].T.astype(BF16)

    @pl.when(i < wk_s.shape[0])
    def _():
        wk_s[i] = wk_ref[...].T.astype(BF16)

    @pl.when(i == 0)
    def _():
        wqvt_s[2 * ATT_W:2 * ATT_W + F_ROWS, :] = wf_ref[...].astype(BF16)

    def project(h_tile, k_out, qvt_out, logf_out, xy_out):
        z = _rms(h_tile, g_ref[...]).astype(BF16)
        k_out[...] = jnp.concatenate(
            [_dot(z, wk_s[c]) for c in range(wk_s.shape[0])], axis=-1).astype(BF16)
        qvft = _dot_nt(wqvt_s[...], z)
        qvt = qvft[:2 * ATT_W, :].astype(BF16)
        tile = qvt_out.shape[2]
        for s in range(qvt_out.shape[0]):
            qvt_out[s] = qvt[:, s * tile:(s + 1) * tile]
        xy_out[...] = jnp.concatenate(
            [_dot(z, wxy_s[c]) for c in range(wxy_s.shape[0])], axis=-1)
        logf_out[0] = _log_sigmoid(qvft[2 * ATT_W:, :] + bf_ref[...])

    @pl.when(jnp.logical_and(i >= IN_CHUNKS, i < IN_CHUNKS + n_tiles))
    def _():
        project(h_ref[...], k_ref, qvt_ref, logf_ref, xy_ref)

    @pl.when(i == IN_CHUNKS + n_tiles)
    def _():
        project(hm_ref[...], km_ref, qvtm_ref, logfm_ref, xym_ref)


def _inproj(h, hm, g, w_t, wf_t, xy_row0, bf, layer, *, seq, tm, tile):
    n, d = h.shape
    tmeta = hm.shape[0]
    n_tiles = n // tm
    per_seq = seq // tm
    per = tm // tile
    rows = ATT_W // IN_CHUNKS
    rows_k = 2 * rows
    n_k = ATT_W // rows_k
    rows_xy = 2 * REC_W // IN_CHUNKS
    q_blk, k_blk, v_blk = 0, ATT_W // rows_k, 2 * (ATT_W // rows)

    def chunk(i):
        return jnp.minimum(i, IN_CHUNKS - 1)

    def row(i):
        return jnp.clip(i - IN_CHUNKS, 0, n_tiles - 1)

    return pl.pallas_call(
        functools.partial(_inproj_kernel, n_tiles=n_tiles),
        grid=(IN_CHUNKS + n_tiles + 1,),
        in_specs=[
            pl.BlockSpec((tm, d), lambda i: (row(i), 0)),
            _whole(hm),
            _of_layer(g, layer),
            pl.BlockSpec((None, rows, d), lambda i: (layer, q_blk + chunk(i), 0)),
            pl.BlockSpec((None, rows_k, d), lambda i: (layer, k_blk + jnp.minimum(i, n_k - 1), 0)),
            pl.BlockSpec((None, rows, d), lambda i: (layer, v_blk + chunk(i), 0)),
            _of_layer(wf_t, layer),
            pl.BlockSpec((None, pl.Element(rows_xy), pl.Element(d)),
                         lambda i: (layer, pl.multiple_of(xy_row0 + chunk(i) * rows_xy, SUBLANES), 0)),
            _of_layer(bf, layer),
        ],
        out_specs=[
            pl.BlockSpec((tm, ATT_W), lambda i: (row(i), 0)),
            pl.BlockSpec((per, 2 * ATT_W, tile), lambda i: (row(i), 0, 0)),
            pl.BlockSpec((1, F_ROWS, tm), lambda i: (row(i) // per_seq, 0, row(i) % per_seq)),
            pl.BlockSpec((tm, 2 * REC_W), lambda i: (row(i), 0)),
            pl.BlockSpec((tmeta, ATT_W), lambda i: (0, 0)),
            pl.BlockSpec((1, 2 * ATT_W, tmeta), lambda i: (0, 0, 0)),
            pl.BlockSpec((1, F_ROWS, tmeta), lambda i: (0, 0, 0)),
            pl.BlockSpec((tmeta, 2 * REC_W), lambda i: (0, 0)),
        ],
        out_shape=[
            jax.ShapeDtypeStruct((n, ATT_W), BF16),
            jax.ShapeDtypeStruct((n // tile, 2 * ATT_W, tile), BF16),
            jax.ShapeDtypeStruct((n // seq, F_ROWS, seq), F32),
            jax.ShapeDtypeStruct((n, 2 * REC_W), F32),
            jax.ShapeDtypeStruct((tmeta, ATT_W), BF16),
            jax.ShapeDtypeStruct((1, 2 * ATT_W, tmeta), BF16),
            jax.ShapeDtypeStruct((1, F_ROWS, tmeta), F32),
            jax.ShapeDtypeStruct((tmeta, 2 * REC_W), F32),
        ],
        scratch_shapes=[
            pltpu.VMEM((n_k, d, rows_k), BF16),
            pltpu.VMEM((2 * ATT_W + F_ROWS, d), BF16),
            pltpu.VMEM((IN_CHUNKS, d, rows_xy), BF16),
        ],
        compiler_params=pltpu.CompilerParams(
            dimension_semantics=("arbitrary",), vmem_limit_bytes=VMEM_LIMIT),
        name="inproj",
    )(h, hm, g, w_t, w_t, w_t, wf_t, w_t, bf)


def _neg_cumsum_cols(lf_ref, out_ref):
    r = lax.broadcasted_iota(jnp.int32, (LANES, LANES), 0)
    c = lax.broadcasted_iota(jnp.int32, (LANES, LANES), 1)
    lower = (r >= c).astype(BF16)
    pad = jnp.zeros((LANES - F_ROWS, LANES), F32)
    carry = jnp.zeros((1, LANES), F32)
    for blk in range(lf_ref.shape[2] // LANES):
        x = jnp.concatenate([lf_ref[0, :, blk * LANES:(blk + 1) * LANES], pad], axis=0)
        hi = x.astype(BF16)
        r1 = x - hi.astype(F32)
        mid = r1.astype(BF16)
        lo = (r1 - mid.astype(F32)).astype(BF16)
        cs = _dot_nt(lower, hi) + _dot_nt(lower, mid) + _dot_nt(lower, lo) + carry
        out_ref[blk * LANES:(blk + 1) * LANES, :] = cs * (-LOG2E)
        carry = cs[LANES - 1:LANES, :]


def _scores_t(kh, qth, bias_col, mask):
    st = _dot(kh, qth) + bias_col
    if mask is not None:
        st = jnp.where(mask, st, -jnp.inf)
    return st


def _pv_and_sum(vth, p):
    ones = jnp.ones((SUM_ROWS, vth.shape[1]), BF16)
    return _dot(jnp.concatenate([vth, ones], axis=0), p)


def _softmax_first(st, vth):
    m = jnp.max(st, axis=0, keepdims=True)
    return m, _pv_and_sum(vth, jnp.exp2(st - m).astype(BF16))


def _softmax_next(st, vth, m, acc):
    m_new = jnp.maximum(m, jnp.max(st, axis=0, keepdims=True))
    alpha = jnp.exp2(m - m_new)
    return m_new, alpha * acc + _pv_and_sum(vth, jnp.exp2(st - m_new).astype(BF16))


def _softmax_next_diag(st, vth, m, acc):
    n = st.shape[0]
    key = lax.broadcasted_iota(jnp.int32, (LANES, LANES), 0)
    qry = lax.broadcasted_iota(jnp.int32, (LANES, LANES), 1)
    tri = key <= qry
    ms, ps = [], []
    for c in range(n // LANES):
        rows = LANES * (c + 1)
        cols = slice(LANES * c, rows)
        blk = jnp.where(tri, st[rows - LANES:rows, cols], -jnp.inf)
        if c > 0:
            blk = jnp.concatenate([st[0:rows - LANES, cols], blk], axis=0)
        m_c = jnp.maximum(m[:, cols], jnp.max(blk, axis=0, keepdims=True))
        p_c = jnp.exp2(blk - m_c).astype(BF16)
        if rows < n:
            p_c = jnp.concatenate([p_c, jnp.zeros((n - rows, LANES), BF16)], axis=0)
        ms.append(m_c)
        ps.append(p_c)
    m_new = jnp.concatenate(ms, axis=1)
    alpha = jnp.exp2(m - m_new)
    return m_new, alpha * acc + _pv_and_sum(vth, jnp.concatenate(ps, axis=1))


def _normalized(acc):
    return acc[:HEAD_DIM, :] / acc[HEAD_DIM:HEAD_DIM + 1, :]


def _meta_attn_kernel(k_ref, qt_ref, vt_ref, lf_ref, g_ref, o_ref, bm_ref, b_s, ot_s):
    t = k_ref.shape[0]
    _neg_cumsum_cols(lf_ref, b_s)
    row = lax.broadcasted_iota(jnp.int32, (t, LANES), 0)
    bm_ref[...] = jnp.where(row < N_META, b_s[...] - b_s[N_META - 1:N_META, :], -jnp.inf)
    key = lax.broadcasted_iota(jnp.int32, (t, t), 0)
    qry = lax.broadcasted_iota(jnp.int32, (t, t), 1)
    causal = key <= qry
    for h in range(N_HEADS):
        sl = slice(h * HEAD_DIM, (h + 1) * HEAD_DIM)
        st = _scores_t(k_ref[:, sl], qt_ref[0, sl, :], b_s[:, h:h + 1], causal)
        _, acc = _softmax_first(st, vt_ref[0, sl, :])
        ot_s[sl, :] = _normalized(acc)
    o_ref[...] = _rms(ot_s[...].T, g_ref[...]).astype(BF16)


def _meta_attn(k, qvt, logf, g, layer):
    t = k.shape[0]
    return pl.pallas_call(
        _meta_attn_kernel,
        grid=(1,),
        in_specs=[
            pl.BlockSpec((t, ATT_W), lambda i: (0, 0)),
            pl.BlockSpec((1, ATT_W, t), lambda i: (0, 0, 0)),
            pl.BlockSpec((1, ATT_W, t), lambda i: (0, 1, 0)),
            pl.BlockSpec((1, F_ROWS, t), lambda i: (0, 0, 0)),
            _of_layer(g, layer),
        ],
        out_specs=[
            pl.BlockSpec((t, ATT_W), lambda i: (0, 0)),
            pl.BlockSpec((t, LANES), lambda i: (0, 0)),
        ],
        out_shape=[
            jax.ShapeDtypeStruct((t, ATT_W), BF16),
            jax.ShapeDtypeStruct((t, LANES), F32),
        ],
        scratch_shapes=[pltpu.VMEM((t, LANES), F32), pltpu.VMEM((ATT_W, t), F32)],
        name="meta_attn",
    )(k, qvt, qvt, logf, g)


def _attn_kernel(qt_ref, k_ref, vt_ref, lf_ref, km_ref, vtm_ref, bm_ref, g_ref,
                 o_ref, b_s, m_s, acc_s, ot_s, st0_s, *, tq):
    j = pl.program_id(1)

    @pl.when(j == 0)
    def _():
        _neg_cumsum_cols(lf_ref, b_s)

    heads = [slice(h * HEAD_DIM, (h + 1) * HEAD_DIM) for h in range(N_HEADS)]
    accs = [slice(h * ACC_ROWS, (h + 1) * ACC_ROWS) for h in range(N_HEADS)]

    def scores(h, i):
        off = pl.multiple_of(i * tq, tq)
        sl = heads[h]
        return _scores_t(k_ref[pl.ds(off, tq), sl], qt_ref[0, sl, :],
                         b_s[pl.ds(off, tq), h:h + 1], None)

    def tile(i, finish):
        ahead = st0_s.shape[0]
        sts = [st0_s[p] for p in range(ahead)]
        for h, sl in enumerate(heads):
            if h + ahead < N_HEADS:
                sts.append(scores(h + ahead, i))
            elif not finish:
                st0_s[h + ahead - N_HEADS] = scores(h + ahead - N_HEADS, i + 1)
            step = _softmax_next_diag if finish else _softmax_next
            m, acc = step(sts[h], vt_ref[i, sl, :], m_s[h:h + 1, :], acc_s[accs[h], :])
            if finish:
                ot_s[sl, :] = _normalized(acc)
            else:
                m_s[h:h + 1, :] = m
                acc_s[accs[h], :] = acc

    for p in range(st0_s.shape[0]):
        st0_s[p] = scores(p, 0)

    sts = [_scores_t(km_ref[0:N_META, sl], qt_ref[0, sl, :], bm_ref[0:N_META, h:h + 1], None)
           for h, sl in enumerate(heads)]
    for h, sl in enumerate(heads):
        m, acc = _softmax_first(sts[h], vtm_ref[0, sl, 0:N_META])
        m_s[h:h + 1, :] = m
        acc_s[accs[h], :] = acc

    def body(i, carry):
        tile(i, False)
        return carry

    lax.fori_loop(0, j, body, 0)

    tile(j, True)
    o_ref[...] = _rms(ot_s[...].T, g_ref[...]).astype(BF16)


def _attn(k, qvt, logf, k_meta, qvt_meta, bias_meta, g, layer, *, seq, tq):
    n = k.shape[0]
    nb = n // seq
    per_seq = seq // tq
    tmeta = k_meta.shape[0]
    return pl.pallas_call(
        functools.partial(_attn_kernel, tq=tq),
        grid=(nb, per_seq),
        in_specs=[
            pl.BlockSpec((1, ATT_W, tq), lambda b, j: (b * per_seq + j, 0, 0)),
            pl.BlockSpec((seq, ATT_W), lambda b, j: (b, 0)),
            pl.BlockSpec((per_seq, ATT_W, tq), lambda b, j: (b, 1, 0)),
            pl.BlockSpec((1, F_ROWS, seq), lambda b, j: (b, 0, 0)),
            pl.BlockSpec((tmeta, ATT_W), lambda b, j: (0, 0)),
            pl.BlockSpec((1, ATT_W, tmeta), lambda b, j: (0, 1, 0)),
            _whole(bias_meta),
            _of_layer(g, layer),
        ],
        out_specs=pl.BlockSpec((tq, ATT_W), lambda b, j: (b * per_seq + j, 0)),
        out_shape=jax.ShapeDtypeStruct((n, ATT_W), BF16),
        scratch_shapes=[
            pltpu.VMEM((seq, LANES), F32),
            pltpu.VMEM((N_HEADS, tq), F32),
            pltpu.VMEM((N_HEADS * ACC_ROWS, tq), F32),
            pltpu.VMEM((ATT_W, tq), F32),
            pltpu.VMEM((SCORES_AHEAD, tq, tq), F32),
        ],
        compiler_params=pltpu.CompilerParams(
            dimension_semantics=("arbitrary", "arbitrary"), vmem_limit_bytes=VMEM_LIMIT),
        name="attn",
    )(qvt, k, qvt, logf, k_meta, qvt_meta, bias_meta, g)


def _rec_init(h0_ref, tail0_ref, ext_s, h_s, *, nb, tc):
    for b in range(nb):
        ext_s[b, tc:tc + SUBLANES, :] = tail0_ref[0]
    h_s[...] = jnp.broadcast_to(h0_ref[...], h_s.shape)


class _RecRefs:
    def __init__(self, cw_ref, cb_ref, wg_ref, bg_ref, lru_ref, g_ref, ext_s, a_s, u_s, h_s,
                 *, nb, tc):
        self.cw, self.cb, self.wg, self.bg, self.lru, self.g = (
            cw_ref, cb_ref, wg_ref, bg_ref, lru_ref, g_ref)
        self.ext_s, self.a_s, self.u_s, self.h_s = ext_s, a_s, u_s, h_s
        self.nb, self.tc = nb, tc
        self.pitch = tc + SUBLANES
        self.n_slab = REC_W // LANES


def _rec_conv_gates(rr, b, xr_b):
    tc, half = rr.tc, REC_W // 2
    rr.ext_s[b, 0:SUBLANES, :] = rr.ext_s[b, tc:tc + SUBLANES, :]
    rr.ext_s[b, SUBLANES:SUBLANES + tc, :] = xr_b
    ext = rr.ext_s[b]
    run = rr.cw[0:1, :] * ext
    for k in range(1, CONV_WIDTH):
        run = pltpu.roll(run, 1, 0) + rr.cw[k:k + 1, :] * ext
    xc = run[SUBLANES:, :] + rr.cb[...]
    xcb = xc.astype(BF16)
    g0 = _dot(xcb[:, :half], rr.wg[0])
    g1 = _dot(xcb[:, half:], rr.wg[1])
    ra = jnp.concatenate([g0[:, :half], g1[:, :half]], axis=-1) + rr.bg[0:1, :]
    rx = jnp.concatenate([g0[:, half:], g1[:, half:]], axis=-1) + rr.bg[1:2, :]
    return xc, ra, rx


def _rec_decay(rr, b, xc, ra, rx):
    tc = rr.tc
    r = jax.nn.sigmoid(ra)
    i = jax.nn.sigmoid(rx)
    log_a = RG_C * r * _log_sigmoid(rr.lru[...])
    a = jnp.exp(log_a)
    th = jnp.tanh(log_a)
    om = -2.0 * th / (1.0 - th)
    mult = jnp.where(om > 0.0, om * lax.rsqrt(om), 0.0)
    u = mult * i * xc
    lo = b * rr.pitch
    for s in range(rr.n_slab):
        rr.a_s[s, lo:lo + tc, :] = a[:, s * LANES:(s + 1) * LANES]
        rr.u_s[s, lo:lo + tc, :] = u[:, s * LANES:(s + 1) * LANES]


def _rec_scan(rr, *, unroll):
    def rows(t):
        return pl.ds(t, rr.nb, stride=rr.pitch) if rr.nb > 1 else pl.ds(t, 1)

    def scan_body(t, hs):
        out = []
        for s in range(rr.n_slab):
            hn = rr.a_s[s, rows(t), :] * hs[s] + rr.u_s[s, rows(t), :]
            rr.u_s[s, rows(t), :] = hn
            out.append(hn)
        return tuple(out)

    hs = tuple(rr.h_s[:, s * LANES:(s + 1) * LANES] for s in range(rr.n_slab))
    if unroll:
        for t in range(rr.tc):
            hs = scan_body(t, hs)
    else:
        hs = lax.fori_loop(0, rr.tc, scan_body, hs, unroll=8)
    rr.h_s[...] = jnp.concatenate(hs, axis=-1)


def _rec_output(rr, b, yr_b):
    lo = b * rr.pitch
    hb = jnp.concatenate(
        [rr.u_s[s, lo:lo + rr.tc, :] for s in range(rr.n_slab)], axis=-1)
    return _rms(hb * _gelu_tanh(yr_b), rr.g[...])


def _rec_kernel(xr_ref, yr_ref, cw_ref, cb_ref, wg_ref, bg_ref, lru_ref, g_ref,
                h0_ref, tail0_ref, out_ref, hn_ref, tailn_ref,
                ext_s, a_s, u_s, h_s, *, nb, tc, pitch, state_t):
    @pl.when(pl.program_id(0) == 0)
    def _():
        _rec_init(h0_ref, tail0_ref, ext_s, h_s, nb=nb, tc=tc)

    rr = _RecRefs(cw_ref, cb_ref, wg_ref, bg_ref, lru_ref, g_ref, ext_s, a_s, u_s, h_s,
                  nb=nb, tc=tc)
    for b in range(nb):
        _rec_decay(rr, b, *_rec_conv_gates(rr, b, xr_ref[b]))
    _rec_scan(rr, unroll=False)
    for b in range(nb):
        out_ref[b] = _rec_output(rr, b, yr_ref[b]).astype(BF16)

    n_slab = REC_W // LANES
    rows = pl.ds(state_t, nb, stride=pitch) if nb > 1 else pl.ds(state_t, 1)
    hn_ref[...] = jnp.concatenate([u_s[s, rows, :] for s in range(n_slab)], axis=-1)
    for b in range(nb):
        tailn_ref[b] = ext_s[b, state_t + 1:state_t + 1 + SUBLANES, :]


def _rec(xy, cw, cb, wg, bg, lru, g, layer, h0, tail0, *, nb, seq, tc, state_t):
    xy3 = xy.reshape(nb, seq, 2 * REC_W)
    pitch = tc + SUBLANES
    n_slab = REC_W // LANES
    out, hn, tailn = pl.pallas_call(
        functools.partial(_rec_kernel, nb=nb, tc=tc, pitch=pitch, state_t=state_t),
        grid=(seq // tc,),
        in_specs=[
            pl.BlockSpec((nb, tc, REC_W), lambda t: (0, t, 0)),
            pl.BlockSpec((nb, tc, REC_W), lambda t: (0, t, 1)),
            _of_layer(cw, layer),
            _of_layer(cb, layer),
            _of_layer(wg, layer),
            _of_layer(bg, layer),
            _of_layer(lru, layer),
            _of_layer(g, layer),
            _whole(h0),
            _whole(tail0),
        ],
        out_specs=[
            pl.BlockSpec((nb, tc, REC_W), lambda t: (0, t, 0)),
            pl.BlockSpec((nb, REC_W), lambda t: (0, 0)),
            pl.BlockSpec((nb, SUBLANES, REC_W), lambda t: (0, 0, 0)),
        ],
        out_shape=[
            jax.ShapeDtypeStruct((nb, seq, REC_W), BF16),
            jax.ShapeDtypeStruct((nb, REC_W), F32),
            jax.ShapeDtypeStruct((nb, SUBLANES, REC_W), F32),
        ],
        scratch_shapes=[
            pltpu.VMEM((nb, tc + SUBLANES, REC_W), F32),
            pltpu.VMEM((n_slab, nb * pitch, LANES), F32),
            pltpu.VMEM((n_slab, nb * pitch, LANES), F32),
            pltpu.VMEM((nb, REC_W), F32),
        ],
        compiler_params=pltpu.CompilerParams(
            dimension_semantics=("arbitrary",), vmem_limit_bytes=VMEM_LIMIT),
        name="rec",
    )(xy3, xy3, cw, cb, wg, bg, lru, g, h0, tail0)
    return out.reshape(nb * seq, REC_W), hn, tailn


def _post_kernel(h_ref, an_ref, xy_ref, hm_ref, anm_ref, rnm_ref,
                 wo_ref, g_ref, wu_ref, wd_ref, fg_ref,
                 cw_ref, cb_ref, wg_ref, bg_ref, lru_ref, grec_ref, h0_ref, tail0_ref,
                 o_ref, *rest, n_tiles, final, nb, tc):
    om_ref = None if final else rest[0]
    wo_s, wu_s, wd_s, rn_s, ext_s, a_s, u_s, hst_s = rest[-8:]
    i = pl.program_id(0)
    rows_o = wo_ref.shape[0]
    rows_d = wd_ref.shape[0]
    first_tile = W_CHUNKS + 1

    @pl.when(i < W_CHUNKS)
    def _():
        wo_s[pl.ds(pl.multiple_of(i * rows_o, rows_o), rows_o), :] = wo_ref[...].astype(BF16)
        wu_s[i] = wu_ref[...].astype(BF16)
        wd_s[pl.ds(pl.multiple_of(i * rows_d, rows_d), rows_d), :] = wd_ref[...].astype(BF16)

    @pl.when(i == 0)
    def _():
        _rec_init(h0_ref, tail0_ref, ext_s, hst_s, nb=nb, tc=tc)

    rr = _RecRefs(cw_ref, cb_ref, wg_ref, bg_ref, lru_ref, grec_ref, ext_s, a_s, u_s, hst_s,
                  nb=nb, tc=tc)

    def rec_gates(b):
        _rec_decay(rr, b, *_rec_conv_gates(rr, b, xy_ref[b, :, 0:REC_W]))

    def rec_output(b, slot):
        out = _rec_output(rr, b, xy_ref[b, :, REC_W:2 * REC_W])
        rn_s[slot, b * tc:(b + 1) * tc, :] = out.astype(BF16)

    @pl.when(i == W_CHUNKS)
    def _():
        for b in range(nb):
            rec_gates(b)
        _rec_scan(rr, unroll=True)
        for b in range(nb):
            rec_output(b, 0)

    def mix_and_mlp(h_tile, an, rn, between=None):
        h1 = h_tile + _dot(an, wo_s[0:ATT_W, :]) + _dot(rn, wo_s[ATT_W:ATT_W + REC_W, :])
        z = _rms(h1, g_ref[...]).astype(BF16)
        down = None
        for c in range(W_CHUNKS):
            if between is not None:
                between(c)
            u = jnp.maximum(_dot(z, wu_s[c]), 0.0)
            part = _dot((u * u).astype(BF16), wd_s[c * rows_d:(c + 1) * rows_d, :])
            down = part if down is None else down + part
        return h1 + down

    @pl.when(jnp.logical_and(i >= first_tile, i < first_tile + n_tiles))
    def _():
        s = i - first_tile
        nxt = lax.rem(s + 1, 2)
        half = W_CHUNKS // 2

        def next_tile_rec(c):
            if c == half:
                _rec_scan(rr, unroll=True)
            for b in range((c % half) * nb // half, (c % half + 1) * nb // half):
                if c < half:
                    rec_gates(b)
                else:
                    rec_output(b, nxt)

        rn = rn_s[lax.rem(s, 2)]
        h_tile = jnp.concatenate([h_ref[b] for b in range(nb)], axis=0)
        an = jnp.concatenate([an_ref[b] for b in range(nb)], axis=0)
        acc = mix_and_mlp(h_tile, an, rn, next_tile_rec)
        if final:
            acc = _rms(acc, fg_ref[...])
        for b in range(nb):
            o_ref[b] = acc[b * tc:(b + 1) * tc, :]

    if not final:
        @pl.when(i == first_tile + n_tiles)
        def _():
            om_ref[...] = mix_and_mlp(hm_ref[...], anm_ref[...], rnm_ref[...])


def _post(h, an, xy, hm, an_m, rn_m, w_out, w_up, w_down, g, fg,
          cw, cb, wg, bg, lru, g_rec, h0, tail0, layer, *, nb, seq, tc, final):
    n, d = h.shape
    n_tiles = seq // tc
    d_mix, d_ff = w_out.shape[1], w_up.shape[2]
    rows_o, cols_u, rows_d = d_mix // W_CHUNKS, d_ff // W_CHUNKS, d_ff // W_CHUNKS
    pitch = tc + SUBLANES
    n_slab = REC_W // LANES
    h3 = h.reshape(nb, seq, d)
    an3 = an.reshape(nb, seq, ATT_W)
    xy3 = xy.reshape(nb, seq, 2 * REC_W)

    def tile(i):
        return (0, jnp.clip(i - (W_CHUNKS + 1), 0, n_tiles - 1), 0)

    def next_tile(i):
        return (0, jnp.clip(i - W_CHUNKS, 0, n_tiles - 1), 0)

    def chunk(i):
        return jnp.minimum(i, W_CHUNKS - 1)

    out_specs = [pl.BlockSpec((nb, tc, d), tile)]
    out_shape = [jax.ShapeDtypeStruct((nb, seq, d), F32)]
    if not final:
        out_specs.append(pl.BlockSpec(hm.shape, lambda i: (0, 0)))
        out_shape.append(jax.ShapeDtypeStruct(hm.shape, F32))

    outs = pl.pallas_call(
        functools.partial(_post_kernel, n_tiles=n_tiles, final=final, nb=nb, tc=tc),
        grid=(W_CHUNKS + 1 + n_tiles + (0 if final else 1),),
        in_specs=[
            pl.BlockSpec((nb, tc, d), tile),
            pl.BlockSpec((nb, tc, ATT_W), tile),
            pl.BlockSpec((nb, tc, 2 * REC_W), next_tile),
            _whole(hm),
            _whole(an_m),
            _whole(rn_m),
            pl.BlockSpec((None, rows_o, d), lambda i: (layer, chunk(i), 0)),
            _of_layer(g, layer),
            pl.BlockSpec((None, d, cols_u), lambda i: (layer, 0, chunk(i))),
            pl.BlockSpec((None, rows_d, d), lambda i: (layer, chunk(i), 0)),
            _whole(fg),
            _of_layer(cw, layer),
            _of_layer(cb, layer),
            _of_layer(wg, layer),
            _of_layer(bg, layer),
            _of_layer(lru, layer),
            _of_layer(g_rec, layer),
            _whole(h0),
            _whole(tail0),
        ],
        out_specs=out_specs,
        out_shape=out_shape,
        scratch_shapes=[
            pltpu.VMEM((d_mix, d), BF16),
            pltpu.VMEM((W_CHUNKS, d, cols_u), BF16),
            pltpu.VMEM((d_ff, d), BF16),
            pltpu.VMEM((2, nb * tc, REC_W), BF16),
            pltpu.VMEM((nb, tc + SUBLANES, REC_W), F32),
            pltpu.VMEM((n_slab, nb * pitch, LANES), F32),
            pltpu.VMEM((n_slab, nb * pitch, LANES), F32),
            pltpu.VMEM((nb, REC_W), F32),
        ],
        compiler_params=pltpu.CompilerParams(
            dimension_semantics=("arbitrary",), vmem_limit_bytes=VMEM_LIMIT),
        name="post",
    )(h3, an3, xy3, hm, an_m, rn_m, w_out, g, w_up, w_down, fg,
      cw, cb, wg, bg, lru, g_rec, h0, tail0)
    h_new = outs[0].reshape(n, d)
    return (h_new, hm) if final else (h_new, outs[1])


def _block_diag_gates(w_a, w_x):
    depth, nblk, blk, _ = w_a.shape
    per = nblk // 2
    eye = jnp.eye(per, dtype=w_a.dtype)

    def bd(w):
        w5 = w.reshape(depth, 2, per, blk, blk)
        full = jnp.einsum('lgpij,pq->lgpiqj', w5, eye)
        return full.reshape(depth, 2, per * blk, per * blk)

    return jnp.concatenate([bd(w_a), bd(w_x)], axis=-1).astype(BF16)


def kernel(x, meta, attn_norm_g, w_in, b_f, conv_w, conv_b, w_gate_a, b_gate_a,
           w_gate_x, b_gate_x, lru_L, attn_out_g, rec_out_g, w_out, mlp_norm_g,
           w_up, w_down, final_g):
    nb, seq, d = x.shape
    depth = w_in.shape[0]
    c_f, c_x = 3 * ATT_W, 3 * ATT_W + N_HEADS

    w_t = jnp.swapaxes(w_in, 1, 2)
    wf_t = jnp.pad(w_t[:, c_f:c_x, :], ((0, 0), (0, F_ROWS - N_HEADS), (0, 0)))
    bf = jnp.pad(b_f, ((0, 0), (0, F_ROWS - N_HEADS))).reshape(depth, F_ROWS, 1)
    g_in = attn_norm_g.reshape(depth, 1, d)
    cb = conv_b.reshape(depth, 1, REC_W)
    wg = _block_diag_gates(w_gate_a, w_gate_x)
    bg = jnp.stack([b_gate_a, b_gate_x], axis=1)
    lru = lru_L.reshape(depth, 1, REC_W)
    g_att = attn_out_g.reshape(depth, 1, ATT_W)
    g_rec = rec_out_g.reshape(depth, 1, REC_W)
    g_mlp = mlp_norm_g.reshape(depth, 1, d)
    fg = final_g.reshape(1, d)
    h0_meta = jnp.zeros((1, REC_W), F32)
    tail0_meta = jnp.zeros((1, SUBLANES, REC_W), F32)

    h = x.reshape(nb * seq, d)
    hm = jnp.pad(meta, ((0, META_PAD - N_META), (0, 0)))

    for l in range(depth):
        last = l == depth - 1
        k, qvt, logf, xy, k_m, qvt_m, logf_m, xy_m = _inproj(
            h, hm, g_in, w_t, wf_t, c_x, bf, l, seq=seq, tm=ROW_TILE, tile=ATT_TILE)

        an_m, bias_m = _meta_attn(k_m, qvt_m, logf_m, g_att, l)
        rn_m, h_state, x_tail = _rec(
            xy_m, conv_w, cb, wg, bg, lru, g_rec, l, h0_meta, tail0_meta,
            nb=1, seq=META_PAD, tc=META_PAD, state_t=N_META - 1)

        an = _attn(k, qvt, logf, k_m, qvt_m, bias_m, g_att, l, seq=seq, tq=ATT_TILE)
        h, hm = _post(h, an, xy, hm, an_m, rn_m, w_out, w_up, w_down, g_mlp, fg,
                      conv_w, cb, wg, bg, lru, g_rec, h_state, x_tail, l,
                      nb=nb, seq=seq, tc=TIME_TILE, final=last)

    return h.reshape(nb, seq, d)
```
